```python
import jax, jax.numpy as jnp
from jax import lax
import numpy as np

D_MODEL = 1024
BATCH = 16
SEQ = 2048
DEPTH = 2
DEC_BATCH = 128
DEC_SEQ = 4
PAST_LEN = 16384
PAGE_SIZE = 128

HEAD_DIM = 64
N_HEADS = (D_MODEL // 2) // HEAD_DIM
N_KV_HEADS = 2
GQ = N_HEADS // N_KV_HEADS
WINDOW = 128
ALIBI_MAX = 8.0
D_LRU = D_MODEL // 4
LRU_BLOCKS = 4
LRU_BLOCK_W = D_LRU // LRU_BLOCKS
LRU_CONV = 4
LRU_C = 8.0
D_SC = D_MODEL // 4
SC_CONV = 3
Q_W = N_HEADS * HEAD_DIM
KV_W = N_KV_HEADS * HEAD_DIM
D_MIX = Q_W + D_LRU + D_SC
IN_W = Q_W + 2 * KV_W + 2 * D_LRU + 3 * D_SC
SPLIT_POINTS = (Q_W, Q_W + KV_W, Q_W + 2 * KV_W, Q_W + 2 * KV_W + D_LRU,
                Q_W + 2 * KV_W + 2 * D_LRU, Q_W + 2 * KV_W + 2 * D_LRU + D_SC,
                Q_W + 2 * KV_W + 2 * D_LRU + 2 * D_SC)
N_GROUPS = 4
EXPERTS_PER_GROUP = 4
N_EXPERTS = N_GROUPS * EXPERTS_PER_GROUP
TOP_K = 2
D_EXPERT = D_MODEL // 4
ALPHA = (2 * DEPTH) ** 0.25
BETA = (8 * DEPTH) ** -0.25
LN_EPS = 1e-5
RMS_EPS = 1e-6
NEG_INF = -1e30

kernel_name = 'hymba_style_lru_swa_conv_hmoe_step'


def _layer_norm(x, g, b):
    xf = x.astype(jnp.float32)
    mu = jnp.mean(xf, -1, keepdims=True)
    var = jnp.mean(jnp.square(xf - mu), -1, keepdims=True)
    return ((xf - mu) * lax.rsqrt(var + LN_EPS) * g + b).astype(x.dtype)


def _rms(y, g):
    yf = y.astype(jnp.float32)
    return (yf * lax.rsqrt(jnp.mean(yf * yf, -1, keepdims=True) + RMS_EPS) * g).astype(y.dtype)


def _causal_dwconv(buf, u, w):
    K = w.shape[0]
    T = u.shape[1]
    up = jnp.concatenate([buf, u], axis=1)
    y = sum(up[:, k:k + T] * w[k] for k in range(K))
    return y, up[:, T:]


def _alibi_slopes():
    h = jnp.arange(1, N_HEADS + 1, dtype=jnp.float32)
    return (2.0 ** (-ALIBI_MAX * h / N_HEADS)).reshape(N_KV_HEADS, GQ)


def _sink_attention(q, k, v, dist, valid, sinks):
    s = jnp.einsum('bnqhgd,bnkhd->bnhgqk', q, k).astype(jnp.float32) * (HEAD_DIM ** -0.5)
    s = s - _alibi_slopes()[:, :, None, None] * dist[:, None, None].astype(jnp.float32)
    s = jnp.where(valid[:, None, None], s, NEG_INF)
    sink = sinks.astype(jnp.float32).reshape(N_KV_HEADS, GQ)[:, :, None, None]
    m = jnp.maximum(jnp.max(s, -1, keepdims=True), sink)
    p = jnp.exp(s - m)
    denom = jnp.sum(p, -1, keepdims=True) + jnp.exp(sink - m)
    return jnp.einsum('bnhgqk,bnkhd->bnqhgd', (p / denom).astype(v.dtype), v)


def _attn_prompt(q, k, v, sinks):
    B_, S, _ = q.shape
    nb = S // WINDOW
    qb = q.reshape(B_, nb, WINDOW, N_KV_HEADS, GQ, HEAD_DIM)
    kb = k.reshape(B_, nb, WINDOW, N_KV_HEADS, HEAD_DIM)
    vb = v.reshape(B_, nb, WINDOW, N_KV_HEADS, HEAD_DIM)
    kk = jnp.concatenate([jnp.concatenate([jnp.zeros_like(kb[:, :1]), kb[:, :-1]], 1), kb], axis=2)
    vv = jnp.concatenate([jnp.concatenate([jnp.zeros_like(vb[:, :1]), vb[:, :-1]], 1), vb], axis=2)
    blk = jnp.arange(nb)[:, None] * WINDOW
    qpos = blk + jnp.arange(WINDOW)[None, :]
    kpos = blk - WINDOW + jnp.arange(2 * WINDOW)[None, :]
    dist = qpos[:, :, None] - kpos[:, None, :]
    valid = (dist >= 0) & (dist < WINDOW) & (kpos[:, None, :] >= 0)
    o = _sink_attention(qb, kk, vv, dist, valid, sinks)
    return o.reshape(B_, S, Q_W)


def _attn_sample(q, k, v, kbuf, vbuf, sinks):
    B_, T, _ = q.shape
    W = kbuf.shape[1]
    kk = jnp.concatenate([kbuf, k.reshape(B_, T, N_KV_HEADS, HEAD_DIM)], axis=1)
    vv = jnp.concatenate([vbuf, v.reshape(B_, T, N_KV_HEADS, HEAD_DIM)], axis=1)
    dist = (jnp.arange(T)[:, None] - jnp.arange(-W, T)[None, :])[None]
    valid = (dist >= 0) & (dist < WINDOW)
    o = _sink_attention(q.reshape(B_, 1, T, N_KV_HEADS, GQ, HEAD_DIM), kk[:, None], vv[:, None], dist, valid, sinks)
    return o.reshape(B_, T, Q_W), kk[:, -W:], vv[:, -W:]


def _rglru(xc, h0, w_a, b_a, w_x, b_x, lam):
    B_, T, _ = xc.shape
    xb = xc.reshape(B_, T, LRU_BLOCKS, LRU_BLOCK_W)
    r = jax.nn.sigmoid(jnp.einsum('btni,nij->btnj', xb, w_a).reshape(B_, T, D_LRU) + b_a)
    i = jax.nn.sigmoid(jnp.einsum('btni,nij->btnj', xb, w_x).reshape(B_, T, D_LRU) + b_x)
    log_a = -LRU_C * jax.nn.softplus(-lam.astype(jnp.float32)) * r.astype(jnp.float32)
    a = jnp.exp(log_a)
    b = jnp.sqrt(-jnp.expm1(2.0 * log_a)) * (i * xc).astype(jnp.float32)

    def step(h, ab):
        h = ab[0] * h + ab[1]
        return h, h

    hT, hs = lax.scan(step, h0.astype(jnp.float32), (jnp.swapaxes(a, 0, 1), jnp.swapaxes(b, 0, 1)))
    return jnp.swapaxes(hs, 0, 1).astype(xc.dtype), hT.astype(h0.dtype)


def _hier_moe(x, w_grp, b_grp, w_route, b_route, w_gate, w_up, w_down):
    B_, T, D = x.shape
    xf = x.reshape(B_ * T, D)
    glog = (xf @ w_grp + b_grp).astype(jnp.float32)
    gp = jax.nn.softmax(glog, -1)
    gsel = jnp.argmax(glog, -1)
    gw = jnp.take_along_axis(gp, gsel[:, None], -1)
    elog = (xf @ w_route + b_route).astype(jnp.float32).reshape(-1, N_GROUPS, EXPERTS_PER_GROUP)
    elog_g = jnp.take_along_axis(elog, gsel[:, None, None], 1)[:, 0]
    tv, ti = lax.top_k(elog_g, TOP_K)
    tw = jax.nn.softmax(tv, -1) * gw
    e_idx = gsel[:, None] * EXPERTS_PER_GROUP + ti
    gates = jnp.sum(jax.nn.one_hot(e_idx, N_EXPERTS, dtype=jnp.float32) * tw[..., None], axis=1)
    hg = jnp.einsum('nd,edf->nef', xf, w_gate)
    hu = jnp.einsum('nd,edf->nef', xf, w_up)
    h = jax.nn.silu(hg) * hu * gates[:, :, None].astype(x.dtype)
    return jnp.einsum('nef,efd->nd', h, w_down).reshape(B_, T, D)


def _layer(x, kbuf, vbuf, lru_h, lru_buf, sc_buf, is_prompt,
           w_in, attn_sinks, lru_conv_w, lru_conv_b, lru_w_a, lru_b_a, lru_w_x, lru_b_x, lru_lambda,
           sc_conv_w, g_mix, w_out, ln1_g, ln1_b, w_grp, b_grp, w_route, b_route,
           w_gate, w_up, w_down, ln2_g, ln2_b):
    B_, T, _ = x.shape
    proj = x @ w_in
    q, k, v, xr, gr, u, bg, cg = jnp.split(proj, list(SPLIT_POINTS), axis=-1)
    if is_prompt:
        o_att = _attn_prompt(q, k, v, attn_sinks)
        new_k = k.reshape(B_, T, N_KV_HEADS, HEAD_DIM)[:, -WINDOW:]
        new_v = v.reshape(B_, T, N_KV_HEADS, HEAD_DIM)[:, -WINDOW:]
    else:
        o_att, new_k, new_v = _attn_sample(q, k, v, kbuf, vbuf, attn_sinks)
    xc, new_lru_buf = _causal_dwconv(lru_buf, xr, lru_conv_w)
    h, new_h = _rglru(xc + lru_conv_b, lru_h, lru_w_a, lru_b_a, lru_w_x, lru_b_x, lru_lambda)
    y_lru = h * jax.nn.gelu(gr)
    y_conv, new_sc_buf = _causal_dwconv(sc_buf, cg * u, sc_conv_w)
    y_sc = bg * y_conv
    mixed = jnp.concatenate([_rms(o_att, g_mix[:Q_W]),
                             _rms(y_lru, g_mix[Q_W:Q_W + D_LRU]),
                             _rms(y_sc, g_mix[Q_W + D_LRU:])], axis=-1)
    x = _layer_norm(ALPHA * x + mixed @ w_out, ln1_g, ln1_b)
    x = _layer_norm(ALPHA * x + _hier_moe(x, w_grp, b_grp, w_route, b_route, w_gate, w_up, w_down), ln2_g, ln2_b)
    return x, (new_k, new_v, new_h, new_lru_buf, new_sc_buf)


def setup_inputs(seed: int = 0) -> dict:
    key = jax.random.key(seed)
    ks = iter(jax.random.split(key, 40))
    nrm = lambda shape, scale: scale * jax.random.normal(next(ks), shape, jnp.float32)
    w_buf = min(WINDOW, PAST_LEN)
    a8 = jax.random.uniform(next(ks), (DEPTH, D_LRU), jnp.float32, 0.9, 0.999)
    sig = a8 ** (1.0 / LRU_C)
    lru_lambda = jnp.log(sig) - jnp.log1p(-sig)
    return {
        'x_prompt': nrm((BATCH, SEQ, D_MODEL), 1.0),
        'x_sample': nrm((DEC_BATCH, DEC_SEQ, D_MODEL), 1.0),
        'cache_k': nrm((DEPTH, DEC_BATCH, w_buf, N_KV_HEADS, HEAD_DIM), 1.0),
        'cache_v': nrm((DEPTH, DEC_BATCH, w_buf, N_KV_HEADS, HEAD_DIM), 1.0),
        'state_lru_h': nrm((DEPTH, DEC_BATCH, D_LRU), 0.5),
        'state_lru_conv': nrm((DEPTH, DEC_BATCH, LRU_CONV - 1, D_LRU), 1.0),
        'state_sc_conv': nrm((DEPTH, DEC_BATCH, SC_CONV - 1, D_SC), 1.0),
        'w_in': nrm((DEPTH, D_MODEL, IN_W), D_MODEL ** -0.5),
        'attn_sinks': nrm((DEPTH, N_HEADS), 0.5),
        'lru_conv_w': nrm((DEPTH, LRU_CONV, D_LRU), LRU_CONV ** -0.5),
        'lru_conv_b': nrm((DEPTH, D_LRU), 0.01),
        'lru_w_a': nrm((DEPTH, LRU_BLOCKS, LRU_BLOCK_W, LRU_BLOCK_W), LRU_BLOCK_W ** -0.5),
        'lru_b_a': nrm((DEPTH, D_LRU), 0.01),
        'lru_w_x': nrm((DEPTH, LRU_BLOCKS, LRU_BLOCK_W, LRU_BLOCK_W), LRU_BLOCK_W ** -0.5),
        'lru_b_x': nrm((DEPTH, D_LRU), 0.01),
        'lru_lambda': lru_lambda,
        'sc_conv_w': nrm((DEPTH, SC_CONV, D_SC), SC_CONV ** -0.5),
        'g_mix': 1.0 + nrm((DEPTH, D_MIX), 0.02),
        'w_out': nrm((DEPTH, D_MIX, D_MODEL), BETA * D_MIX ** -0.5),
        'ln1_g': 1.0 + nrm((DEPTH, D_MODEL), 0.02),
        'ln1_b': nrm((DEPTH, D_MODEL), 0.02),
        'w_grp': nrm((DEPTH, D_MODEL, N_GROUPS), D_MODEL ** -0.5),
        'b_grp': nrm((DEPTH, N_GROUPS), 0.01),
        'w_route': nrm((DEPTH, D_MODEL, N_EXPERTS), D_MODEL ** -0.5),
        'b_route': nrm((DEPTH, N_EXPERTS), 0.01),
        'w_gate': nrm((DEPTH, N_EXPERTS, D_MODEL, D_EXPERT), D_MODEL ** -0.5),
        'w_up': nrm((DEPTH, N_EXPERTS, D_MODEL, D_EXPERT), D_MODEL ** -0.5),
        'w_down': nrm((DEPTH, N_EXPERTS, D_EXPERT, D_MODEL), BETA * D_EXPERT ** -0.5),
        'ln2_g': 1.0 + nrm((DEPTH, D_MODEL), 0.02),
        'ln2_b': nrm((DEPTH, D_MODEL), 0.02),
    }


def reference(x_prompt, x_sample, cache_k, cache_v, state_lru_h, state_lru_conv, state_sc_conv,
              w_in, attn_sinks, lru_conv_w, lru_conv_b, lru_w_a, lru_b_a, lru_w_x, lru_b_x, lru_lambda,
              sc_conv_w, g_mix, w_out, ln1_g, ln1_b, w_grp, b_grp, w_route, b_route,
              w_gate, w_up, w_down, ln2_g, ln2_b):
    yp, ys = x_prompt, x_sample
    Bp = x_prompt.shape[0]
    dt = x_prompt.dtype
    zero_h = jnp.zeros((Bp, D_LRU), dt)
    zero_lru_buf = jnp.zeros((Bp, LRU_CONV - 1, D_LRU), dt)
    zero_sc_buf = jnp.zeros((Bp, SC_CONV - 1, D_SC), dt)
    outs_p = [[], [], [], [], []]
    outs_s = [[], [], [], [], []]
    for l in range(DEPTH):
        lw = (w_in[l], attn_sinks[l], lru_conv_w[l], lru_conv_b[l], lru_w_a[l], lru_b_a[l],
              lru_w_x[l], lru_b_x[l], lru_lambda[l], sc_conv_w[l], g_mix[l], w_out[l],
              ln1_g[l], ln1_b[l], w_grp[l], b_grp[l], w_route[l], b_route[l],
              w_gate[l], w_up[l], w_down[l], ln2_g[l], ln2_b[l])
        yp, sp = _layer(yp, None, None, zero_h, zero_lru_buf, zero_sc_buf, True, *lw)
        ys, ss = _layer(ys, cache_k[l], cache_v[l], state_lru_h[l], state_lru_conv[l], state_sc_conv[l], False, *lw)
        for lst, arr in zip(outs_p, sp):
            lst.append(arr)
        for lst, arr in zip(outs_s, ss):
            lst.append(arr)
    new_k_prompt = jnp.stack(outs_p[0])
    new_v_prompt = jnp.stack(outs_p[1])
    new_lru_h_prompt = jnp.stack(outs_p[2])
    new_lru_conv_prompt = jnp.stack(outs_p[3])
    new_sc_conv_prompt = jnp.stack(outs_p[4])
    new_k_sample = jnp.stack(outs_s[0])
    new_v_sample = jnp.stack(outs_s[1])
    new_lru_h_sample = jnp.stack(outs_s[2])
    new_lru_conv_sample = jnp.stack(outs_s[3])
    new_sc_conv_sample = jnp.stack(outs_s[4])
    return (yp, ys, new_k_prompt, new_v_prompt, new_lru_h_prompt, new_lru_conv_prompt, new_sc_conv_prompt,
            new_k_sample, new_v_sample, new_lru_h_sample, new_lru_conv_sample, new_sc_conv_sample)
```

```python
import functools
import math

import jax
import jax.numpy as jnp
from jax import lax
from jax.experimental import pallas as pl
from jax.experimental.pallas import tpu as pltpu

D_MODEL = 1024
DEPTH = 2
HEAD_DIM = 64
N_HEADS = 8
N_KV_HEADS = 2
GQ = N_HEADS // N_KV_HEADS
WINDOW = 128
ALIBI_MAX = 8.0
D_LRU = 256
LRU_BLOCKS = 4
LRU_BLOCK_W = D_LRU // LRU_BLOCKS
LRU_CONV = 4
LRU_C = 8.0
D_SC = 256
SC_CONV = 3
Q_W = N_HEADS * HEAD_DIM
KV_W = N_KV_HEADS * HEAD_DIM
IN_W = Q_W + 2 * KV_W + 2 * D_LRU + 3 * D_SC
N_GROUPS = 4
EXPERTS_PER_GROUP = 4
N_EXPERTS = N_GROUPS * EXPERTS_PER_GROUP
D_EXPERT = 256
ALPHA = (2 * DEPTH) ** 0.25
LN_EPS = 1e-5
RMS_EPS = 1e-6
NEG_INF = -1e30

O_K = Q_W
O_V = O_K + KV_W
O_XR = O_V + KV_W
O_GR = O_XR + D_LRU
O_U = O_GR + D_LRU
O_BG = O_U + D_SC
O_CG = O_BG + D_SC

LANES = 128
SUBLANES = 8
N_QTILES = Q_W // LANES
ROUTE_W = LANES
VMEM_LIMIT = 56 * 1024 * 1024

T_PROMPT = 512
T_MOE = 512
BT_SAMPLE = 32
KPAD = WINDOW + SUBLANES

_BF = jnp.bfloat16
_F32 = jnp.float32


def _slopes():
    return [2.0 ** (-ALIBI_MAX * h / N_HEADS) for h in range(1, N_HEADS + 1)]


def _dot(a, b):
    return jnp.dot(a, b, preferred_element_type=_F32)


def _dot_t(a, b):
    return lax.dot_general(a, b, (((1,), (1,)), ((), ())), preferred_element_type=_F32)


def _rms_rows(y, g):
    return y * lax.rsqrt(jnp.mean(y * y, axis=-1, keepdims=True) + RMS_EPS) * g


def _ln_rows(x, g, b):
    mu = jnp.mean(x, axis=-1, keepdims=True)
    xc = x - mu
    var = jnp.mean(xc * xc, axis=-1, keepdims=True)
    return xc * lax.rsqrt(var + LN_EPS) * g + b


def _lru_coeffs(xcb, wlru, b_a, b_x, lam):
    gates = _dot(xcb.astype(_BF), wlru)
    r = jax.nn.sigmoid(gates[:, :D_LRU] + b_a)
    i = jax.nn.sigmoid(gates[:, D_LRU:] + b_x)
    log_a = (-LRU_C * jax.nn.softplus(-lam)) * r
    a = jnp.exp(log_a)
    b = jnp.sqrt(jnp.tanh(-log_a) * (1.0 + a * a)) * (i * xcb)
    return a, b


def _sink_softmax(s, sink):
    m = jnp.maximum(jnp.max(s, axis=-1, keepdims=True), sink)
    p = jnp.exp(s - m)
    denom = jnp.sum(p, axis=-1, keepdims=True) + jnp.exp(sink - m)
    return p, 1.0 / denom


def _mixer_prompt_kernel(sinks_ref, x_ref, win_ref, wlru_ref, wout_ref, cw_ref, v256_ref, scw_ref,
                         gmix_ref, ln_ref,
                         y_ref, knew_ref, vnew_ref, hnew_ref, lrubuf_ref, scbuf_ref,
                         proj, qsc, klo, khi, vext, xrext, zext, hst, a_s, b_s, hl_s, pc_s, bias_s, mixed):
    T = T_PROMPT
    nblk = T // WINDOW
    s_idx = pl.program_id(1)
    lane = lax.broadcasted_iota(jnp.int32, (1, LANES), 1)
    lo_lane = lane < HEAD_DIM

    @pl.when(s_idx == 0)
    def _():
        klo[0:WINDOW, :] = jnp.zeros((WINDOW, LANES), _BF)
        khi[0:WINDOW, :] = jnp.zeros((WINDOW, LANES), _BF)
        vext[0:WINDOW, :] = jnp.zeros((WINDOW, LANES), _BF)
        xrext[0:SUBLANES, :] = jnp.zeros((SUBLANES, D_LRU), _F32)
        zext[0:SUBLANES, :] = jnp.zeros((SUBLANES, D_SC), _F32)
        hst[...] = jnp.zeros((1, D_LRU), _F32)
        qi = lax.broadcasted_iota(jnp.int32, (WINDOW, 2 * WINDOW), 0)
        kc = lax.broadcasted_iota(jnp.int32, (WINDOW, 2 * WINDOW), 1)
        dist = qi + WINDOW - kc
        valid = (dist >= 0) & (dist < WINDOW)
        valid_first = valid & (kc >= WINDOW)
        distf = dist.astype(_F32)
        for h, slope in enumerate(_slopes()):
            bias = -slope * distf
            bias_s[0, h] = jnp.where(valid, bias, NEG_INF)
            bias_s[1, h] = jnp.where(valid_first, bias, NEG_INF)

    proj[...] = _dot(x_ref[0].astype(_BF), win_ref[...])

    for n in range(nblk):
        rows = slice(n * WINDOW, (n + 1) * WINDOW)
        for j in range(N_QTILES):
            qsc[n, j * WINDOW:(j + 1) * WINDOW, :] = (
                proj[rows, j * LANES:(j + 1) * LANES] * (HEAD_DIM ** -0.5)).astype(_BF)
    k = proj[:, O_K:O_K + KV_W]
    v = proj[:, O_V:O_V + KV_W]
    klo[WINDOW:, :] = jnp.where(lo_lane, k, 0.0).astype(_BF)
    khi[WINDOW:, :] = jnp.where(lo_lane, 0.0, k).astype(_BF)
    vext[WINDOW:, :] = v.astype(_BF)
    knew_ref[0] = k[T - WINDOW:, :]
    vnew_ref[0] = v[T - WINDOW:, :]

    def attn_block(n, carry):
        r0 = pl.multiple_of(n * WINDOW, WINDOW)
        first = jnp.logical_and(s_idx == 0, n == 0).astype(jnp.int32)
        qs = qsc[n]
        kl = klo[pl.ds(r0, 2 * WINDOW), :]
        kh = khi[pl.ds(r0, 2 * WINDOW), :]
        vv = vext[pl.ds(r0, 2 * WINDOW), :]
        s_lo = _dot_t(qs, kl)
        s_hi = _dot_t(qs, kh)
        p_lo, p_hi, inv_lo, inv_hi = [], [], [], []
        for j in range(N_QTILES):
            rj = slice(j * WINDOW, (j + 1) * WINDOW)
            p, inv = _sink_softmax(s_lo[rj] + bias_s[first, j], sinks_ref[j])
            p_lo.append(p.astype(_BF))
            inv_lo.append(inv)
            p, inv = _sink_softmax(s_hi[rj] + bias_s[first, GQ + j], sinks_ref[GQ + j])
            p_hi.append(p.astype(_BF))
            inv_hi.append(inv)
        o_lo = _dot(jnp.concatenate(p_lo, axis=0), vv)
        o_hi = _dot(jnp.concatenate(p_hi, axis=0), vv)
        for j in range(N_QTILES):
            rj = slice(j * WINDOW, (j + 1) * WINDOW)
            mixed[pl.ds(r0, WINDOW), j * LANES:(j + 1) * LANES] = jnp.where(
                lo_lane, o_lo[rj] * inv_lo[j], o_hi[rj] * inv_hi[j])
        return carry

    lax.fori_loop(0, nblk, attn_block, 0)
    klo[0:WINDOW, :] = klo[T:T + WINDOW, :]
    khi[0:WINDOW, :] = khi[T:T + WINDOW, :]
    vext[0:WINDOW, :] = vext[T:T + WINDOW, :]

    xr = proj[:, O_XR:O_XR + D_LRU]
    xrext[SUBLANES:, :] = xr
    xc = xr * cw_ref[LRU_CONV - 1:LRU_CONV, :]
    for kk in range(1, LRU_CONV):
        xc = xc + xrext[pl.ds(SUBLANES - kk, T), :] * cw_ref[LRU_CONV - 1 - kk:LRU_CONV - kk, :]
    xcb = xc + v256_ref[0:1, :]
    a, b = _lru_coeffs(xcb, wlru_ref[...], v256_ref[1:2, :], v256_ref[2:3, :], v256_ref[3:4, :])
    NH = D_LRU // LANES
    for hh in range(NH):
        a_s[hh] = a[:, hh * LANES:(hh + 1) * LANES]
        b_s[hh] = b[:, hh * LANES:(hh + 1) * LANES]
    lrubuf_ref[0] = xrext[pl.ds(T + SUBLANES - (LRU_CONV - 1), LRU_CONV - 1), :]
    xrext[0:SUBLANES, :] = xrext[T:T + SUBLANES, :]

    C = T // SUBLANES

    def scan_step(s, carry):
        out = []
        for hh in range(NH):
            h, p = carry[hh]
            at = a_s[hh, pl.ds(s, SUBLANES, stride=C), :]
            bt = b_s[hh, pl.ds(s, SUBLANES, stride=C), :]
            h = at * h + bt
            p = at * p
            hl_s[hh, pl.ds(s, SUBLANES, stride=C), :] = h
            pc_s[hh, pl.ds(s, SUBLANES, stride=C), :] = p
            out.append((h, p))
        return tuple(out)

    init = tuple((jnp.zeros((SUBLANES, LANES), _F32), jnp.ones((SUBLANES, LANES), _F32))
                 for _ in range(NH))
    ends = lax.fori_loop(0, C, scan_step, init)
    h_end = jnp.concatenate([e[0] for e in ends], axis=-1)
    p_end = jnp.concatenate([e[1] for e in ends], axis=-1)
    h_in = hst[...]
    gmix_lru = gmix_ref[:, Q_W:Q_W + D_LRU]
    for c in range(SUBLANES):
        rows = slice(c * C, (c + 1) * C)
        hl = jnp.concatenate([hl_s[hh, rows, :] for hh in range(NH)], axis=-1)
        pc = jnp.concatenate([pc_s[hh, rows, :] for hh in range(NH)], axis=-1)
        hs = hl + pc * h_in
        y_lru = hs * jax.nn.gelu(proj[rows, O_GR:O_GR + D_LRU])
        mixed[rows, Q_W:Q_W + D_LRU] = _rms_rows(y_lru, gmix_lru)
        h_in = p_end[c:c + 1, :] * h_in + h_end[c:c + 1, :]
    hst[...] = h_in
    hnew_ref[0] = h_in

    z = proj[:, O_CG:O_CG + D_SC] * proj[:, O_U:O_U + D_SC]
    zext[SUBLANES:, :] = z
    yc = z * scw_ref[SC_CONV - 1:SC_CONV, :]
    for kk in range(1, SC_CONV):
        yc = yc + zext[pl.ds(SUBLANES - kk, T), :] * scw_ref[SC_CONV - 1 - kk:SC_CONV - kk, :]
    y_sc = proj[:, O_BG:O_BG + D_SC] * yc
    mixed[:, Q_W + D_LRU:] = _rms_rows(y_sc, gmix_ref[:, Q_W + D_LRU:])
    scbuf_ref[0] = zext[pl.ds(T + SUBLANES - (SC_CONV - 1), SC_CONV - 1), :]
    zext[0:SUBLANES, :] = zext[T:T + SUBLANES, :]

    mixed[:, 0:Q_W] = _rms_rows(mixed[:, 0:Q_W], gmix_ref[:, 0:Q_W])
    m = _dot(mixed[...].astype(_BF), wout_ref[...])
    y_ref[0] = _ln_rows(ALPHA * x_ref[0] + m, ln_ref[0:1, :], ln_ref[1:2, :])


def _const_spec(shape):
    nd = len(shape)
    return pl.BlockSpec(shape, lambda *_: (0,) * nd, pipeline_mode=pl.Buffered(1))


def _mixer_prompt(x, sinks, win, wlru, wout, cw, v256, scw, gmix, ln):
    B, S, _ = x.shape
    T = T_PROMPT
    grid = (B, S // T)
    seq_out = lambda shape: pl.BlockSpec((1,) + shape, lambda b, s: (b, 0, 0))
    return pl.pallas_call(
        _mixer_prompt_kernel,
        grid=grid,
        in_specs=[
            pl.BlockSpec(memory_space=pltpu.SMEM),
            pl.BlockSpec((1, T, D_MODEL), lambda b, s: (b, s, 0)),
            _const_spec(win.shape), _const_spec(wlru.shape), _const_spec(wout.shape),
            _const_spec(cw.shape), _const_spec(v256.shape), _const_spec(scw.shape),
            _const_spec(gmix.shape), _const_spec(ln.shape),
        ],
        out_specs=[
            pl.BlockSpec((1, T, D_MODEL), lambda b, s: (b, s, 0)),
            seq_out((WINDOW, KV_W)), seq_out((WINDOW, KV_W)), seq_out((1, D_LRU)),
            seq_out((LRU_CONV - 1, D_LRU)), seq_out((SC_CONV - 1, D_SC)),
        ],
        out_shape=[
            jax.ShapeDtypeStruct((B, S, D_MODEL), _F32),
            jax.ShapeDtypeStruct((B, WINDOW, KV_W), _F32),
            jax.ShapeDtypeStruct((B, WINDOW, KV_W), _F32),
            jax.ShapeDtypeStruct((B, 1, D_LRU), _F32),
            jax.ShapeDtypeStruct((B, LRU_CONV - 1, D_LRU), _F32),
            jax.ShapeDtypeStruct((B, SC_CONV - 1, D_SC), _F32),
        ],
        scratch_shapes=[
            pltpu.VMEM((T, IN_W), _F32),
            pltpu.VMEM((T // WINDOW, N_QTILES * WINDOW, LANES), _BF),
            pltpu.VMEM((T + WINDOW, LANES), _BF),
            pltpu.VMEM((T + WINDOW, LANES), _BF),
            pltpu.VMEM((T + WINDOW, LANES), _BF),
            pltpu.VMEM((T + SUBLANES, D_LRU), _F32),
            pltpu.VMEM((T + SUBLANES, D_SC), _F32),
            pltpu.VMEM((1, D_LRU), _F32),
            pltpu.VMEM((D_LRU // LANES, T, LANES), _F32),
            pltpu.VMEM((D_LRU // LANES, T, LANES), _F32),
            pltpu.VMEM((D_LRU // LANES, T, LANES), _F32),
            pltpu.VMEM((D_LRU // LANES, T, LANES), _F32),
            pltpu.VMEM((2, N_HEADS, WINDOW, 2 * WINDOW), _F32),
            pltpu.VMEM((T, D_MODEL), _F32),
        ],
        compiler_params=pltpu.CompilerParams(
            dimension_semantics=("arbitrary", "arbitrary"), vmem_limit_bytes=VMEM_LIMIT),
        name="mixer_prompt",
    )(sinks, x, win, wlru, wout, cw, v256, scw, gmix, ln)


def _mixer_sample_kernel(sinks_ref, x_ref, ck_ref, cv_ref, h0_ref, lbuf_ref, sbuf_ref,
                         win_ref, wlru_ref, wout_ref, cw_ref, v256_ref, scw_ref, gmix_ref, ln_ref,
                         y_ref, knew_ref, vnew_ref, hnew_ref, lrubuf_ref, scbuf_ref,
                         kx2, vx2, qs2, os2):
    TS = x_ref.shape[0]
    BT = BT_SAMPLE
    NQ = TS * N_QTILES
    lane = lax.broadcasted_iota(jnp.int32, (1, LANES), 1)
    lo_lane = lane < HEAD_DIM

    x = x_ref[...].reshape(TS * BT, D_MODEL)
    proj = _dot(x.astype(_BF), win_ref[...])

    for bb in range(BT):
        kx2[bb * KPAD:bb * KPAD + WINDOW, :] = ck_ref[bb]
        vx2[bb * KPAD:bb * KPAD + WINDOW, :] = cv_ref[bb]
    for t in range(SUBLANES):
        if t < TS:
            rows = slice(t * BT, (t + 1) * BT)
            kx2[pl.ds(WINDOW + t, BT, stride=KPAD), :] = proj[rows, O_K:O_K + KV_W]
            vx2[pl.ds(WINDOW + t, BT, stride=KPAD), :] = proj[rows, O_V:O_V + KV_W]
            for j in range(N_QTILES):
                qs2[pl.ds(t * N_QTILES + j, BT, stride=NQ), :] = (
                    proj[rows, j * LANES:(j + 1) * LANES] * (HEAD_DIM ** -0.5))
        else:
            kx2[pl.ds(WINDOW + t, BT, stride=KPAD), :] = jnp.zeros((BT, LANES), _F32)
            vx2[pl.ds(WINDOW + t, BT, stride=KPAD), :] = jnp.zeros((BT, LANES), _F32)
    for bb in range(BT):
        knew_ref[bb] = kx2[bb * KPAD + TS:bb * KPAD + TS + WINDOW, :]
        vnew_ref[bb] = vx2[bb * KPAD + TS:bb * KPAD + TS + WINDOW, :]

    kall = kx2[...].reshape(BT, KPAD, LANES)
    qs = qs2[...].reshape(BT, NQ, LANES).astype(_BF)
    k_lo = jnp.where(lo_lane, kall, 0.0).astype(_BF)
    k_hi = jnp.where(lo_lane, 0.0, kall).astype(_BF)
    vv = vx2[...].reshape(BT, KPAD, LANES).astype(_BF)
    s_lo = jnp.einsum('bqd,bkd->bqk', qs, k_lo, preferred_element_type=_F32)
    s_hi = jnp.einsum('bqd,bkd->bqk', qs, k_hi, preferred_element_type=_F32)
    ri = lax.broadcasted_iota(jnp.int32, (NQ, KPAD), 0)
    ci = lax.broadcasted_iota(jnp.int32, (NQ, KPAD), 1)
    tq = lax.shift_right_logical(ri, int(math.log2(N_QTILES)))
    jq = ri & (N_QTILES - 1)
    dist = WINDOW + tq - ci
    valid = (dist >= 0) & (dist < WINDOW)
    distf = dist.astype(_F32)
    rj = lax.broadcasted_iota(jnp.int32, (NQ, 1), 0) & (N_QTILES - 1)
    slope_lo = jnp.zeros((NQ, KPAD), _F32)
    slope_hi = jnp.zeros((NQ, KPAD), _F32)
    sink_lo = jnp.zeros((NQ, 1), _F32)
    sink_hi = jnp.zeros((NQ, 1), _F32)
    slopes = _slopes()
    for j in range(N_QTILES):
        slope_lo = jnp.where(jq == j, slopes[j], slope_lo)
        slope_hi = jnp.where(jq == j, slopes[GQ + j], slope_hi)
        sink_lo = jnp.where(rj == j, sinks_ref[j], sink_lo)
        sink_hi = jnp.where(rj == j, sinks_ref[GQ + j], sink_hi)
    bias_lo = jnp.where(valid, -slope_lo * distf, NEG_INF)
    bias_hi = jnp.where(valid, -slope_hi * distf, NEG_INF)
    p_lo, inv_lo = _sink_softmax(s_lo + bias_lo[None], sink_lo[None])
    p_hi, inv_hi = _sink_softmax(s_hi + bias_hi[None], sink_hi[None])
    o_lo = jnp.einsum('bqk,bkd->bqd', p_lo.astype(_BF), vv, preferred_element_type=_F32)
    o_hi = jnp.einsum('bqk,bkd->bqd', p_hi.astype(_BF), vv, preferred_element_type=_F32)
    o = jnp.where(lo_lane[None], o_lo * inv_lo, o_hi * inv_hi)
    os2[...] = o.reshape(BT * NQ, LANES)

    lbuf = lbuf_ref[...]
    sbuf = sbuf_ref[...]
    xr_hist = [lbuf[:, i * D_LRU:(i + 1) * D_LRU] for i in range(LRU_CONV - 1)]
    z_hist = [sbuf[:, i * D_SC:(i + 1) * D_SC] for i in range(SC_CONV - 1)]
    h = h0_ref[...]
    gmix = gmix_ref[...]
    mixed_rows = []
    for t in range(TS):
        rows = slice(t * BT, (t + 1) * BT)
        o_att = jnp.concatenate(
            [os2[pl.ds(t * N_QTILES + j, BT, stride=NQ), :] for j in range(N_QTILES)], axis=-1)
        xr_hist.append(proj[rows, O_XR:O_XR + D_LRU])
        xc = sum(xr_hist[t + kk] * cw_ref[kk:kk + 1, :] for kk in range(LRU_CONV))
        xcb = xc + v256_ref[0:1, :]
        a, b = _lru_coeffs(xcb, wlru_ref[...], v256_ref[1:2, :], v256_ref[2:3, :], v256_ref[3:4, :])
        h = a * h + b
        y_lru = h * jax.nn.gelu(proj[rows, O_GR:O_GR + D_LRU])
        z_hist.append(proj[rows, O_CG:O_CG + D_SC] * proj[rows, O_U:O_U + D_SC])
        yc = sum(z_hist[t + kk] * scw_ref[kk:kk + 1, :] for kk in range(SC_CONV))
        y_sc = proj[rows, O_BG:O_BG + D_SC] * yc
        mixed_rows.append(jnp.concatenate(
            [_rms_rows(o_att, gmix[:, 0:Q_W]),
             _rms_rows(y_lru, gmix[:, Q_W:Q_W + D_LRU]),
             _rms_rows(y_sc, gmix[:, Q_W + D_LRU:])], axis=-1))
    hnew_ref[...] = h
    lrubuf_ref[...] = jnp.concatenate(xr_hist[TS:], axis=-1)
    scbuf_ref[...] = jnp.concatenate(z_hist[TS:], axis=-1)
    mixed = jnp.concatenate(mixed_rows, axis=0)
    m = _dot(mixed.astype(_BF), wout_ref[...])
    y = _ln_rows(ALPHA * x + m, ln_ref[0:1, :], ln_ref[1:2, :])
    y_ref[...] = y.reshape(TS, BT, D_MODEL)


def _mixer_sample(x, ck, cv, h0, lbuf, sbuf, sinks, win, wlru, wout, cw, v256, scw, gmix, ln):
    TS, B, _ = x.shape
    BT = BT_SAMPLE
    rows2 = lambda w: pl.BlockSpec((BT, w), lambda i: (i, 0))
    cache = pl.BlockSpec((BT, WINDOW, KV_W), lambda i: (i, 0, 0))
    xspec = pl.BlockSpec((TS, BT, D_MODEL), lambda i: (0, i, 0))
    return pl.pallas_call(
        _mixer_sample_kernel,
        grid=(B // BT,),
        in_specs=[
            pl.BlockSpec(memory_space=pltpu.SMEM),
            xspec, cache, cache, rows2(D_LRU), rows2((LRU_CONV - 1) * D_LRU), rows2((SC_CONV - 1) * D_SC),
            _const_spec(win.shape), _const_spec(wlru.shape), _const_spec(wout.shape),
            _const_spec(cw.shape), _const_spec(v256.shape), _const_spec(scw.shape),
            _const_spec(gmix.shape), _const_spec(ln.shape),
        ],
        out_specs=[xspec, cache, cache, rows2(D_LRU), rows2((LRU_CONV - 1) * D_LRU),
                   rows2((SC_CONV - 1) * D_SC)],
        out_shape=[
            jax.ShapeDtypeStruct((TS, B, D_MODEL), _F32),
            jax.ShapeDtypeStruct((B, WINDOW, KV_W), _F32),
            jax.ShapeDtypeStruct((B, WINDOW, KV_W), _F32),
            jax.ShapeDtypeStruct((B, D_LRU), _F32),
            jax.ShapeDtypeStruct((B, (LRU_CONV - 1) * D_LRU), _F32),
            jax.ShapeDtypeStruct((B, (SC_CONV - 1) * D_SC), _F32),
        ],
        scratch_shapes=[
            pltpu.VMEM((BT * KPAD, LANES), _F32),
            pltpu.VMEM((BT * KPAD, LANES), _F32),
            pltpu.VMEM((BT * TS * N_QTILES, LANES), _F32),
            pltpu.VMEM((BT * TS * N_QTILES, LANES), _F32),
        ],
        compiler_params=pltpu.CompilerParams(
            dimension_semantics=("arbitrary",), vmem_limit_bytes=VMEM_LIMIT),
        name="mixer_sample",
    )(sinks, x, ck, cv, h0, lbuf, sbuf, win, wlru, wout, cw, v256, scw, gmix, ln)


def _route(logits):
    lane = lax.broadcasted_iota(jnp.int32, logits.shape, 1)
    big = jnp.int32(ROUTE_W)
    is_grp = lane < N_GROUPS
    glog = jnp.where(is_grp, logits, -jnp.inf)
    gmax = jnp.max(glog, axis=-1, keepdims=True)
    gsel = jnp.min(jnp.where(glog == gmax, lane, big), axis=-1, keepdims=True)
    gw = 1.0 / jnp.sum(jnp.exp(glog - gmax), axis=-1, keepdims=True)
    e_lo = N_GROUPS + gsel * EXPERTS_PER_GROUP
    in_grp = (lane >= e_lo) & (lane < e_lo + EXPERTS_PER_GROUP)
    elog = jnp.where(in_grp, logits, -jnp.inf)
    v1 = jnp.max(elog, axis=-1, keepdims=True)
    i1 = jnp.min(jnp.where(elog == v1, lane, big), axis=-1, keepdims=True)
    elog2 = jnp.where(lane == i1, -jnp.inf, elog)
    v2 = jnp.max(elog2, axis=-1, keepdims=True)
    i2 = jnp.min(jnp.where(elog2 == v2, lane, big), axis=-1, keepdims=True)
    e2 = jnp.exp(v2 - v1)
    w1 = gw / (1.0 + e2)
    w2 = gw * e2 / (1.0 + e2)
    return jnp.where(lane == i1, w1, jnp.where(lane == i2, w2, 0.0))


def _moe_kernel(x_ref, wr_ref, br_ref, wg_ref, wu_ref, wd_ref, ln_ref, y_ref, gates_s, hsc, acc):
    x = x_ref[...]
    xb = x.astype(_BF)
    gates_s[...] = _route(_dot(xb, wr_ref[...]) + br_ref[...])

    def group(g, carry):
        for i in range(EXPERTS_PER_GROUP):
            e = g * EXPERTS_PER_GROUP + i
            hg = _dot(xb, wg_ref[e])
            hu = _dot(xb, wu_ref[e])
            lane = lax.broadcasted_iota(jnp.int32, (1, ROUTE_W), 1)
            gate = jnp.sum(jnp.where(lane == N_GROUPS + e, gates_s[...], 0.0), axis=-1, keepdims=True)
            hsc[:, i * D_EXPERT:(i + 1) * D_EXPERT] = (jax.nn.silu(hg) * hu * gate).astype(_BF)
        contrib = _dot(hsc[...], wd_ref[g])

        @pl.when(g == 0)
        def _():
            acc[...] = contrib

        @pl.when(g > 0)
        def _():
            acc[...] += contrib
        return carry

    lax.fori_loop(0, N_GROUPS, group, 0)
    y_ref[...] = _ln_rows(ALPHA * x + acc[...], ln_ref[0:1, :], ln_ref[1:2, :])


def _moe(x, wr, br, wg, wu, wd, ln, tile):
    N = x.shape[0]
    tok = pl.BlockSpec((tile, D_MODEL), lambda i: (i, 0))
    return pl.pallas_call(
        _moe_kernel,
        grid=(N // tile,),
        in_specs=[tok, _const_spec(wr.shape), _const_spec(br.shape), _const_spec(wg.shape),
                  _const_spec(wu.shape), _const_spec(wd.shape), _const_spec(ln.shape)],
        out_specs=tok,
        out_shape=jax.ShapeDtypeStruct((N, D_MODEL), _F32),
        scratch_shapes=[
            pltpu.VMEM((tile, ROUTE_W), _F32),
            pltpu.VMEM((tile, EXPERTS_PER_GROUP * D_EXPERT), _BF),
            pltpu.VMEM((tile, D_MODEL), _F32),
        ],
        compiler_params=pltpu.CompilerParams(
            dimension_semantics=("arbitrary",), vmem_limit_bytes=VMEM_LIMIT),
        name="moe",
    )(x, wr, br, wg, wu, wd, ln)


def _q_perm():
    cols = []
    for j in range(N_QTILES):
        for h in (j, GQ + j):
            cols.extend(range(h * HEAD_DIM, (h + 1) * HEAD_DIM))
    return jnp.asarray(cols, jnp.int32)


def _block_diag(w):
    eye = jnp.eye(LRU_BLOCKS, dtype=w.dtype)
    return jnp.einsum('nij,nm->nimj', w, eye).reshape(D_LRU, D_LRU)


def kernel(x_prompt, x_sample, cache_k, cache_v, state_lru_h, state_lru_conv, state_sc_conv, w_in, attn_sinks, lru_conv_w, lru_conv_b, lru_w_a, lru_b_a, lru_w_x, lru_b_x, lru_lambda, sc_conv_w, g_mix, w_out, ln1_g, ln1_b, w_grp, b_grp, w_route, b_route, w_gate, w_up, w_down, ln2_g, ln2_b):
    B, S, _ = x_prompt.shape
    DB, TS, _ = x_sample.shape
    assert S % T_PROMPT == 0 and (B * S) % T_MOE == 0 and DB % BT_SAMPLE == 0
    assert TS <= SUBLANES and cache_k.shape[2] == WINDOW
    perm = _q_perm()

    win = jnp.concatenate([jnp.take(w_in[:, :, :Q_W], perm, axis=2), w_in[:, :, Q_W:]], axis=2).astype(_BF)
    wout = jnp.concatenate([jnp.take(w_out[:, :Q_W], perm, axis=1), w_out[:, Q_W:]], axis=1).astype(_BF)
    gmix = jnp.concatenate([jnp.take(g_mix[:, :Q_W], perm, axis=1), g_mix[:, Q_W:]], axis=1)[:, None, :]
    wlru = jnp.concatenate([jax.vmap(_block_diag)(lru_w_a), jax.vmap(_block_diag)(lru_w_x)], axis=2).astype(_BF)
    v256 = jnp.stack([lru_conv_b, lru_b_a, lru_b_x, lru_lambda], axis=1)
    ln1 = jnp.stack([ln1_g, ln1_b], axis=1)
    ln2 = jnp.stack([ln2_g, ln2_b], axis=1)
    pad = ROUTE_W - N_GROUPS - N_EXPERTS
    wr = jnp.pad(jnp.concatenate([w_grp, w_route], axis=2), ((0, 0), (0, 0), (0, pad))).astype(_BF)
    br = jnp.pad(jnp.concatenate([b_grp, b_route], axis=1), ((0, 0), (0, pad)))[:, None, :]
    wg = w_gate.astype(_BF)
    wu = w_up.astype(_BF)
    wd = w_down.reshape(DEPTH, N_GROUPS, EXPERTS_PER_GROUP * D_EXPERT, D_MODEL).astype(_BF)

    yp = x_prompt
    ys = jnp.swapaxes(x_sample, 0, 1)
    outs_p = [[] for _ in range(5)]
    outs_s = [[] for _ in range(5)]
    for l in range(DEPTH):
        mix_w = (attn_sinks[l], win[l], wlru[l], wout[l], lru_conv_w[l], v256[l], sc_conv_w[l], gmix[l], ln1[l])
        moe_w = (wr[l], br[l], wg[l], wu[l], wd[l], ln2[l])
        yp, kp, vp, hp, lbp, sbp = _mixer_prompt(yp, *mix_w)
        yp = _moe(yp.reshape(B * S, D_MODEL), *moe_w, T_MOE).reshape(B, S, D_MODEL)
        ys, ks, vs, hs, lbs, sbs = _mixer_sample(
            ys, cache_k[l].reshape(DB, WINDOW, KV_W), cache_v[l].reshape(DB, WINDOW, KV_W),
            state_lru_h[l], state_lru_conv[l].reshape(DB, -1), state_sc_conv[l].reshape(DB, -1), *mix_w)
        ys = _moe(ys.reshape(TS * DB, D_MODEL), *moe_w, TS * DB).reshape(TS, DB, D_MODEL)
        for lst, arr in zip(outs_p, (kp.reshape(B, WINDOW, N_KV_HEADS, HEAD_DIM),
                                     vp.reshape(B, WINDOW, N_KV_HEADS, HEAD_DIM),
                                     hp.reshape(B, D_LRU), lbp, sbp)):
            lst.append(arr)
        for lst, arr in zip(outs_s, (ks.reshape(DB, WINDOW, N_KV_HEADS, HEAD_DIM),
                                     vs.reshape(DB, WINDOW, N_KV_HEADS, HEAD_DIM),
                                     hs, lbs.reshape(DB, LRU_CONV - 1, D_LRU),
                                     sbs.reshape(DB, SC_CONV - 1, D_SC))):
            lst.append(arr)
    return (yp, jnp.swapaxes(ys, 0, 1),
            *[jnp.stack(o) for o in outs_p], *[jnp.stack(o) for o in outs_s])
```

```python
import functools
import math

import jax
import jax.numpy as jnp
from jax import lax
from jax.experimental import pallas as pl
from jax.experimental.pallas import tpu as pltpu

D_MODEL = 1024
DEPTH = 2
HEAD_DIM = 64
N_HEADS = 8
N_KV_HEADS = 2
GQ = N_HEADS // N_KV_HEADS
WINDOW = 128
ALIBI_MAX = 8.0
D_LRU = 256
LRU_BLOCKS = 4
LRU_BLOCK_W = D_LRU // LRU_BLOCKS
LRU_CONV = 4
LRU_C = 8.0
D_SC = 256
SC_CONV = 3
Q_W = N_HEADS * HEAD_DIM
KV_W = N_KV_HEADS * HEAD_DIM
IN_W = Q_W + 2 * KV_W + 2 * D_LRU + 3 * D_SC
N_GROUPS = 4
EXPERTS_PER_GROUP = 4
N_EXPERTS = N_GROUPS * EXPERTS_PER_GROUP
D_EXPERT = 256
ALPHA = (2 * DEPTH) ** 0.25
LN_EPS = 1e-5
RMS_EPS = 1e-6
NEG_INF = -1e30

O_K = Q_W
O_V = O_K + KV_W
O_XR = O_V + KV_W
O_GR = O_XR + D_LRU
O_U = O_GR + D_LRU
O_BG = O_U + D_SC
O_CG = O_BG + D_SC

LANES = 128
SUBLANES = 8
N_QTILES = Q_W // LANES
ROUTE_W = LANES
VMEM_LIMIT = 56 * 1024 * 1024

T_PROMPT = 512
T_DISPATCH = 512
BM = 512
BT_SAMPLE = 32
KPAD = WINDOW + SUBLANES

_BF = jnp.bfloat16
_F32 = jnp.float32


def _slopes():
    return [2.0 ** (-ALIBI_MAX * h / N_HEADS) for h in range(1, N_HEADS + 1)]


def _dot(a, b):
    return jnp.dot(a, b, preferred_element_type=_F32)


def _dot_t(a, b):
    return lax.dot_general(a, b, (((1,), (1,)), ((), ())), preferred_element_type=_F32)


def _rms_rows(y, g):
    return y * lax.rsqrt(jnp.mean(y * y, axis=-1, keepdims=True) + RMS_EPS) * g


def _ln_rows(x, g, b):
    mu = jnp.mean(x, axis=-1, keepdims=True)
    xc = x - mu
    var = jnp.mean(xc * xc, axis=-1, keepdims=True)
    return xc * lax.rsqrt(var + LN_EPS) * g + b


def _lru_coeffs(xcb, wlru, b_a, b_x, lam):
    gates = _dot(xcb.astype(_BF), wlru)
    r = jax.nn.sigmoid(gates[:, :D_LRU] + b_a)
    i = jax.nn.sigmoid(gates[:, D_LRU:] + b_x)
    log_a = (-LRU_C * jax.nn.softplus(-lam)) * r
    a = jnp.exp(log_a)
    b = jnp.sqrt(jnp.tanh(-log_a) * (1.0 + a * a)) * (i * xcb)
    return a, b


def _sink_softmax(s, sink):
    m = jnp.maximum(jnp.max(s, axis=-1, keepdims=True), sink)
    p = jnp.exp(s - m)
    denom = jnp.sum(p, axis=-1, keepdims=True) + jnp.exp(sink - m)
    return p, 1.0 / denom


def _mixer_prompt_kernel(sinks_ref, x_ref, win_ref, wlru_ref, wout_ref, cw_ref, v256_ref, scw_ref,
                         gmix_ref, ln_ref,
                         y_ref, knew_ref, vnew_ref, hnew_ref, lrubuf_ref, scbuf_ref,
                         proj, qsc, klo, khi, vext, xrext, zext, hst, a_s, b_s, hl_s, pc_s, bias_s, mixed):
    T = T_PROMPT
    nblk = T // WINDOW
    s_idx = pl.program_id(1)
    lane = lax.broadcasted_iota(jnp.int32, (1, LANES), 1)
    lo_lane = lane < HEAD_DIM

    @pl.when(s_idx == 0)
    def _():
        klo[0:WINDOW, :] = jnp.zeros((WINDOW, LANES), _BF)
        khi[0:WINDOW, :] = jnp.zeros((WINDOW, LANES), _BF)
        vext[0:WINDOW, :] = jnp.zeros((WINDOW, LANES), _BF)
        xrext[0:SUBLANES, :] = jnp.zeros((SUBLANES, D_LRU), _F32)
        zext[0:SUBLANES, :] = jnp.zeros((SUBLANES, D_SC), _F32)
        hst[...] = jnp.zeros((1, D_LRU), _F32)
        qi = lax.broadcasted_iota(jnp.int32, (WINDOW, 2 * WINDOW), 0)
        kc = lax.broadcasted_iota(jnp.int32, (WINDOW, 2 * WINDOW), 1)
        dist = qi + WINDOW - kc
        valid = (dist >= 0) & (dist < WINDOW)
        valid_first = valid & (kc >= WINDOW)
        distf = dist.astype(_F32)
        for h, slope in enumerate(_slopes()):
            bias = -slope * distf
            bias_s[0, h] = jnp.where(valid, bias, NEG_INF)
            bias_s[1, h] = jnp.where(valid_first, bias, NEG_INF)

    proj[...] = _dot(x_ref[0].astype(_BF), win_ref[...])

    for n in range(nblk):
        rows = slice(n * WINDOW, (n + 1) * WINDOW)
        for j in range(N_QTILES):
            qsc[n, j * WINDOW:(j + 1) * WINDOW, :] = (
                proj[rows, j * LANES:(j + 1) * LANES] * (HEAD_DIM ** -0.5)).astype(_BF)
    k = proj[:, O_K:O_K + KV_W]
    v = proj[:, O_V:O_V + KV_W]
    klo[WINDOW:, :] = jnp.where(lo_lane, k, 0.0).astype(_BF)
    khi[WINDOW:, :] = jnp.where(lo_lane, 0.0, k).astype(_BF)
    vext[WINDOW:, :] = v.astype(_BF)
    knew_ref[0] = k[T - WINDOW:, :]
    vnew_ref[0] = v[T - WINDOW:, :]

    def attn_block(n, carry):
        r0 = pl.multiple_of(n * WINDOW, WINDOW)
        first = jnp.logical_and(s_idx == 0, n == 0).astype(jnp.int32)
        qs = qsc[n]
        kl = klo[pl.ds(r0, 2 * WINDOW), :]
        kh = khi[pl.ds(r0, 2 * WINDOW), :]
        vv = vext[pl.ds(r0, 2 * WINDOW), :]
        s_lo = _dot_t(qs, kl)
        s_hi = _dot_t(qs, kh)
        p_lo, p_hi, inv_lo, inv_hi = [], [], [], []
        for j in range(N_QTILES):
            rj = slice(j * WINDOW, (j + 1) * WINDOW)
            p, inv = _sink_softmax(s_lo[rj] + bias_s[first, j], sinks_ref[j])
            p_lo.append(p.astype(_BF))
            inv_lo.append(inv)
            p, inv = _sink_softmax(s_hi[rj] + bias_s[first, GQ + j], sinks_ref[GQ + j])
            p_hi.append(p.astype(_BF))
            inv_hi.append(inv)
        o_lo = _dot(jnp.concatenate(p_lo, axis=0), vv)
        o_hi = _dot(jnp.concatenate(p_hi, axis=0), vv)
        for j in range(N_QTILES):
            rj = slice(j * WINDOW, (j + 1) * WINDOW)
            mixed[pl.ds(r0, WINDOW), j * LANES:(j + 1) * LANES] = jnp.where(
                lo_lane, o_lo[rj] * inv_lo[j], o_hi[rj] * inv_hi[j])
        return carry

    lax.fori_loop(0, nblk, attn_block, 0)
    klo[0:WINDOW, :] = klo[T:T + WINDOW, :]
    khi[0:WINDOW, :] = khi[T:T + WINDOW, :]
    vext[0:WINDOW, :] = vext[T:T + WINDOW, :]

    xr = proj[:, O_XR:O_XR + D_LRU]
    xrext[SUBLANES:, :] = xr
    xc = xr * cw_ref[LRU_CONV - 1:LRU_CONV, :]
    for kk in range(1, LRU_CONV):
        xc = xc + xrext[pl.ds(SUBLANES - kk, T), :] * cw_ref[LRU_CONV - 1 - kk:LRU_CONV - kk, :]
    xcb = xc + v256_ref[0:1, :]
    a, b = _lru_coeffs(xcb, wlru_ref[...], v256_ref[1:2, :], v256_ref[2:3, :], v256_ref[3:4, :])
    NH = D_LRU // LANES
    for hh in range(NH):
        a_s[hh] = a[:, hh * LANES:(hh + 1) * LANES]
        b_s[hh] = b[:, hh * LANES:(hh + 1) * LANES]
    lrubuf_ref[0] = xrext[pl.ds(T + SUBLANES - (LRU_CONV - 1), LRU_CONV - 1), :]
    xrext[0:SUBLANES, :] = xrext[T:T + SUBLANES, :]

    C = T // SUBLANES

    def scan_step(s, carry):
        out = []
        for hh in range(NH):
            h, p = carry[hh]
            at = a_s[hh, pl.ds(s, SUBLANES, stride=C), :]
            bt = b_s[hh, pl.ds(s, SUBLANES, stride=C), :]
            h = at * h + bt
            p = at * p
            hl_s[hh, pl.ds(s, SUBLANES, stride=C), :] = h
            pc_s[hh, pl.ds(s, SUBLANES, stride=C), :] = p
            out.append((h, p))
        return tuple(out)

    init = tuple((jnp.zeros((SUBLANES, LANES), _F32), jnp.ones((SUBLANES, LANES), _F32))
                 for _ in range(NH))
    ends = lax.fori_loop(0, C, scan_step, init)
    h_end = jnp.concatenate([e[0] for e in ends], axis=-1)
    p_end = jnp.concatenate([e[1] for e in ends], axis=-1)
    h_in = hst[...]
    gmix_lru = gmix_ref[:, Q_W:Q_W + D_LRU]
    for c in range(SUBLANES):
        rows = slice(c * C, (c + 1) * C)
        hl = jnp.concatenate([hl_s[hh, rows, :] for hh in range(NH)], axis=-1)
        pc = jnp.concatenate([pc_s[hh, rows, :] for hh in range(NH)], axis=-1)
        hs = hl + pc * h_in
        y_lru = hs * jax.nn.gelu(proj[rows, O_GR:O_GR + D_LRU])
        mixed[rows, Q_W:Q_W + D_LRU] = _rms_rows(y_lru, gmix_lru)
        h_in = p_end[c:c + 1, :] * h_in + h_end[c:c + 1, :]
    hst[...] = h_in
    hnew_ref[0] = h_in

    z = proj[:, O_CG:O_CG + D_SC] * proj[:, O_U:O_U + D_SC]
    zext[SUBLANES:, :] = z
    yc = z * scw_ref[SC_CONV - 1:SC_CONV, :]
    for kk in range(1, SC_CONV):
        yc = yc + zext[pl.ds(SUBLANES - kk, T), :] * scw_ref[SC_CONV - 1 - kk:SC_CONV - kk, :]
    y_sc = proj[:, O_BG:O_BG + D_SC] * yc
    mixed[:, Q_W + D_LRU:] = _rms_rows(y_sc, gmix_ref[:, Q_W + D_LRU:])
    scbuf_ref[0] = zext[pl.ds(T + SUBLANES - (SC_CONV - 1), SC_CONV - 1), :]
    zext[0:SUBLANES, :] = zext[T:T + SUBLANES, :]

    mixed[:, 0:Q_W] = _rms_rows(mixed[:, 0:Q_W], gmix_ref[:, 0:Q_W])
    m = _dot(mixed[...].astype(_BF), wout_ref[...])
    y_ref[0] = _ln_rows(ALPHA * x_ref[0] + m, ln_ref[0:1, :], ln_ref[1:2, :])


def _const_spec(shape):
    nd = len(shape)
    return pl.BlockSpec(shape, lambda *_: (0,) * nd, pipeline_mode=pl.Buffered(1))


def _mixer_prompt(x, sinks, win, wlru, wout, cw, v256, scw, gmix, ln):
    B, S, _ = x.shape
    T = T_PROMPT
    grid = (B, S // T)
    seq_out = lambda shape: pl.BlockSpec((1,) + shape, lambda b, s: (b, 0, 0))
    return pl.pallas_call(
        _mixer_prompt_kernel,
        grid=grid,
        in_specs=[
            pl.BlockSpec(memory_space=pltpu.SMEM),
            pl.BlockSpec((1, T, D_MODEL), lambda b, s: (b, s, 0)),
            _const_spec(win.shape), _const_spec(wlru.shape), _const_spec(wout.shape),
            _const_spec(cw.shape), _const_spec(v256.shape), _const_spec(scw.shape),
            _const_spec(gmix.shape), _const_spec(ln.shape),
        ],
        out_specs=[
            pl.BlockSpec((1, T, D_MODEL), lambda b, s: (b, s, 0)),
            seq_out((WINDOW, KV_W)), seq_out((WINDOW, KV_W)), seq_out((1, D_LRU)),
            seq_out((LRU_CONV - 1, D_LRU)), seq_out((SC_CONV - 1, D_SC)),
        ],
        out_shape=[
            jax.ShapeDtypeStruct((B, S, D_MODEL), _F32),
            jax.ShapeDtypeStruct((B, WINDOW, KV_W), _F32),
            jax.ShapeDtypeStruct((B, WINDOW, KV_W), _F32),
            jax.ShapeDtypeStruct((B, 1, D_LRU), _F32),
            jax.ShapeDtypeStruct((B, LRU_CONV - 1, D_LRU), _F32),
            jax.ShapeDtypeStruct((B, SC_CONV - 1, D_SC), _F32),
        ],
        scratch_shapes=[
            pltpu.VMEM((T, IN_W), _F32),
            pltpu.VMEM((T // WINDOW, N_QTILES * WINDOW, LANES), _BF),
            pltpu.VMEM((T + WINDOW, LANES), _BF),
            pltpu.VMEM((T + WINDOW, LANES), _BF),
            pltpu.VMEM((T + WINDOW, LANES), _BF),
            pltpu.VMEM((T + SUBLANES, D_LRU), _F32),
            pltpu.VMEM((T + SUBLANES, D_SC), _F32),
            pltpu.VMEM((1, D_LRU), _F32),
            pltpu.VMEM((D_LRU // LANES, T, LANES), _F32),
            pltpu.VMEM((D_LRU // LANES, T, LANES), _F32),
            pltpu.VMEM((D_LRU // LANES, T, LANES), _F32),
            pltpu.VMEM((D_LRU // LANES, T, LANES), _F32),
            pltpu.VMEM((2, N_HEADS, WINDOW, 2 * WINDOW), _F32),
            pltpu.VMEM((T, D_MODEL), _F32),
        ],
        compiler_params=pltpu.CompilerParams(
            dimension_semantics=("arbitrary", "arbitrary"), vmem_limit_bytes=VMEM_LIMIT),
        name="mixer_prompt",
    )(sinks, x, win, wlru, wout, cw, v256, scw, gmix, ln)


def _mixer_sample_kernel(sinks_ref, x_ref, ck_ref, cv_ref, h0_ref, lbuf_ref, sbuf_ref,
                         win_ref, wlru_ref, wout_ref, cw_ref, v256_ref, scw_ref, gmix_ref, ln_ref,
                         y_ref, knew_ref, vnew_ref, hnew_ref, lrubuf_ref, scbuf_ref,
                         kx2, vx2, qs2, os2):
    TS = x_ref.shape[0]
    BT = BT_SAMPLE
    NQ = TS * N_QTILES
    lane = lax.broadcasted_iota(jnp.int32, (1, LANES), 1)
    lo_lane = lane < HEAD_DIM

    x = x_ref[...].reshape(TS * BT, D_MODEL)
    proj = _dot(x.astype(_BF), win_ref[...])

    for bb in range(BT):
        kx2[bb * KPAD:bb * KPAD + WINDOW, :] = ck_ref[bb]
        vx2[bb * KPAD:bb * KPAD + WINDOW, :] = cv_ref[bb]
    for t in range(SUBLANES):
        if t < TS:
            rows = slice(t * BT, (t + 1) * BT)
            kx2[pl.ds(WINDOW + t, BT, stride=KPAD), :] = proj[rows, O_K:O_K + KV_W]
            vx2[pl.ds(WINDOW + t, BT, stride=KPAD), :] = proj[rows, O_V:O_V + KV_W]
            for j in range(N_QTILES):
                qs2[pl.ds(t * N_QTILES + j, BT, stride=NQ), :] = (
                    proj[rows, j * LANES:(j + 1) * LANES] * (HEAD_DIM ** -0.5))
        else:
            kx2[pl.ds(WINDOW + t, BT, stride=KPAD), :] = jnp.zeros((BT, LANES), _F32)
            vx2[pl.ds(WINDOW + t, BT, stride=KPAD), :] = jnp.zeros((BT, LANES), _F32)
    for bb in range(BT):
        knew_ref[bb] = kx2[bb * KPAD + TS:bb * KPAD + TS + WINDOW, :]
        vnew_ref[bb] = vx2[bb * KPAD + TS:bb * KPAD + TS + WINDOW, :]

    kall = kx2[...].reshape(BT, KPAD, LANES)
    qs = qs2[...].reshape(BT, NQ, LANES).astype(_BF)
    k_lo = jnp.where(lo_lane, kall, 0.0).astype(_BF)
    k_hi = jnp.where(lo_lane, 0.0, kall).astype(_BF)
    vv = vx2[...].reshape(BT, KPAD, LANES).astype(_BF)
    s_lo = jnp.einsum('bqd,bkd->bqk', qs, k_lo, preferred_element_type=_F32)
    s_hi = jnp.einsum('bqd,bkd->bqk', qs, k_hi, preferred_element_type=_F32)
    ri = lax.broadcasted_iota(jnp.int32, (NQ, KPAD), 0)
    ci = lax.broadcasted_iota(jnp.int32, (NQ, KPAD), 1)
    tq = lax.shift_right_logical(ri, int(math.log2(N_QTILES)))
    jq = ri & (N_QTILES - 1)
    dist = WINDOW + tq - ci
    valid = (dist >= 0) & (dist < WINDOW)
    distf = dist.astype(_F32)
    rj = lax.broadcasted_iota(jnp.int32, (NQ, 1), 0) & (N_QTILES - 1)
    slope_lo = jnp.zeros((NQ, KPAD), _F32)
    slope_hi = jnp.zeros((NQ, KPAD), _F32)
    sink_lo = jnp.zeros((NQ, 1), _F32)
    sink_hi = jnp.zeros((NQ, 1), _F32)
    slopes = _slopes()
    for j in range(N_QTILES):
        slope_lo = jnp.where(jq == j, slopes[j], slope_lo)
        slope_hi = jnp.where(jq == j, slopes[GQ + j], slope_hi)
        sink_lo = jnp.where(rj == j, sinks_ref[j], sink_lo)
        sink_hi = jnp.where(rj == j, sinks_ref[GQ + j], sink_hi)
    bias_lo = jnp.where(valid, -slope_lo * distf, NEG_INF)
    bias_hi = jnp.where(valid, -slope_hi * distf, NEG_INF)
    p_lo, inv_lo = _sink_softmax(s_lo + bias_lo[None], sink_lo[None])
    p_hi, inv_hi = _sink_softmax(s_hi + bias_hi[None], sink_hi[None])
    o_lo = jnp.einsum('bqk,bkd->bqd', p_lo.astype(_BF), vv, preferred_element_type=_F32)
    o_hi = jnp.einsum('bqk,bkd->bqd', p_hi.astype(_BF), vv, preferred_element_type=_F32)
    o = jnp.where(lo_lane[None], o_lo * inv_lo, o_hi * inv_hi)
    os2[...] = o.reshape(BT * NQ, LANES)

    lbuf = lbuf_ref[...]
    sbuf = sbuf_ref[...]
    xr_hist = [lbuf[:, i * D_LRU:(i + 1) * D_LRU] for i in range(LRU_CONV - 1)]
    z_hist = [sbuf[:, i * D_SC:(i + 1) * D_SC] for i in range(SC_CONV - 1)]
    h = h0_ref[...]
    gmix = gmix_ref[...]
    mixed_rows = []
    for t in range(TS):
        rows = slice(t * BT, (t + 1) * BT)
        o_att = jnp.concatenate(
            [os2[pl.ds(t * N_QTILES + j, BT, stride=NQ), :] for j in range(N_QTILES)], axis=-1)
        xr_hist.append(proj[rows, O_XR:O_XR + D_LRU])
        xc = sum(xr_hist[t + kk] * cw_ref[kk:kk + 1, :] for kk in range(LRU_CONV))
        xcb = xc + v256_ref[0:1, :]
        a, b = _lru_coeffs(xcb, wlru_ref[...], v256_ref[1:2, :], v256_ref[2:3, :], v256_ref[3:4, :])
        h = a * h + b
        y_lru = h * jax.nn.gelu(proj[rows, O_GR:O_GR + D_LRU])
        z_hist.append(proj[rows, O_CG:O_CG + D_SC] * proj[rows, O_U:O_U + D_SC])
        yc = sum(z_hist[t + kk] * scw_ref[kk:kk + 1, :] for kk in range(SC_CONV))
        y_sc = proj[rows, O_BG:O_BG + D_SC] * yc
        mixed_rows.append(jnp.concatenate(
            [_rms_rows(o_att, gmix[:, 0:Q_W]),
             _rms_rows(y_lru, gmix[:, Q_W:Q_W + D_LRU]),
             _rms_rows(y_sc, gmix[:, Q_W + D_LRU:])], axis=-1))
    hnew_ref[...] = h
    lrubuf_ref[...] = jnp.concatenate(xr_hist[TS:], axis=-1)
    scbuf_ref[...] = jnp.concatenate(z_hist[TS:], axis=-1)
    mixed = jnp.concatenate(mixed_rows, axis=0)
    m = _dot(mixed.astype(_BF), wout_ref[...])
    y = _ln_rows(ALPHA * x + m, ln_ref[0:1, :], ln_ref[1:2, :])
    y_ref[...] = y.reshape(TS, BT, D_MODEL)


def _mixer_sample(x, ck, cv, h0, lbuf, sbuf, sinks, win, wlru, wout, cw, v256, scw, gmix, ln):
    TS, B, _ = x.shape
    BT = BT_SAMPLE
    rows2 = lambda w: pl.BlockSpec((BT, w), lambda i: (i, 0))
    cache = pl.BlockSpec((BT, WINDOW, KV_W), lambda i: (i, 0, 0))
    xspec = pl.BlockSpec((TS, BT, D_MODEL), lambda i: (0, i, 0))
    return pl.pallas_call(
        _mixer_sample_kernel,
        grid=(B // BT,),
        in_specs=[
            pl.BlockSpec(memory_space=pltpu.SMEM),
            xspec, cache, cache, rows2(D_LRU), rows2((LRU_CONV - 1) * D_LRU), rows2((SC_CONV - 1) * D_SC),
            _const_spec(win.shape), _const_spec(wlru.shape), _const_spec(wout.shape),
            _const_spec(cw.shape), _const_spec(v256.shape), _const_spec(scw.shape),
            _const_spec(gmix.shape), _const_spec(ln.shape),
        ],
        out_specs=[xspec, cache, cache, rows2(D_LRU), rows2((LRU_CONV - 1) * D_LRU),
                   rows2((SC_CONV - 1) * D_SC)],
        out_shape=[
            jax.ShapeDtypeStruct((TS, B, D_MODEL), _F32),
            jax.ShapeDtypeStruct((B, WINDOW, KV_W), _F32),
            jax.ShapeDtypeStruct((B, WINDOW, KV_W), _F32),
            jax.ShapeDtypeStruct((B, D_LRU), _F32),
            jax.ShapeDtypeStruct((B, (LRU_CONV - 1) * D_LRU), _F32),
            jax.ShapeDtypeStruct((B, (SC_CONV - 1) * D_SC), _F32),
        ],
        scratch_shapes=[
            pltpu.VMEM((BT * KPAD, LANES), _F32),
            pltpu.VMEM((BT * KPAD, LANES), _F32),
            pltpu.VMEM((BT * TS * N_QTILES, LANES), _F32),
            pltpu.VMEM((BT * TS * N_QTILES, LANES), _F32),
        ],
        compiler_params=pltpu.CompilerParams(
            dimension_semantics=("arbitrary",), vmem_limit_bytes=VMEM_LIMIT),
        name="mixer_sample",
    )(sinks, x, ck, cv, h0, lbuf, sbuf, win, wlru, wout, cw, v256, scw, gmix, ln)


def _route(logits):
    lane = lax.broadcasted_iota(jnp.int32, logits.shape, 1)
    big = jnp.int32(ROUTE_W)
    is_grp = lane < N_GROUPS
    glog = jnp.where(is_grp, logits, -jnp.inf)
    gmax = jnp.max(glog, axis=-1, keepdims=True)
    gsel = jnp.min(jnp.where(glog == gmax, lane, big), axis=-1, keepdims=True)
    gw = 1.0 / jnp.sum(jnp.exp(glog - gmax), axis=-1, keepdims=True)
    e_lo = N_GROUPS + gsel * EXPERTS_PER_GROUP
    in_grp = (lane >= e_lo) & (lane < e_lo + EXPERTS_PER_GROUP)
    elog = jnp.where(in_grp, logits, -jnp.inf)
    v1 = jnp.max(elog, axis=-1, keepdims=True)
    i1 = jnp.min(jnp.where(elog == v1, lane, big), axis=-1, keepdims=True)
    elog2 = jnp.where(lane == i1, -jnp.inf, elog)
    v2 = jnp.max(elog2, axis=-1, keepdims=True)
    i2 = jnp.min(jnp.where(elog2 == v2, lane, big), axis=-1, keepdims=True)
    e2 = jnp.exp(v2 - v1)
    w1 = gw / (1.0 + e2)
    w2 = gw * e2 / (1.0 + e2)
    return jnp.where(lane == i1, w1, jnp.where(lane == i2, w2, 0.0))


def _moe_kernel(x_ref, wr_ref, br_ref, wg_ref, wu_ref, wd_ref, ln_ref, y_ref, gates_s, hsc, acc):
    x = x_ref[...]
    xb = x.astype(_BF)
    gates_s[...] = _route(_dot(xb, wr_ref[...]) + br_ref[...])

    def group(g, carry):
        for i in range(EXPERTS_PER_GROUP):
            e = g * EXPERTS_PER_GROUP + i
            hg = _dot(xb, wg_ref[e])
            hu = _dot(xb, wu_ref[e])
            lane = lax.broadcasted_iota(jnp.int32, (1, ROUTE_W), 1)
            gate = jnp.sum(jnp.where(lane == N_GROUPS + e, gates_s[...], 0.0), axis=-1, keepdims=True)
            hsc[:, i * D_EXPERT:(i + 1) * D_EXPERT] = (jax.nn.silu(hg) * hu * gate).astype(_BF)
        contrib = _dot(hsc[...], wd_ref[g])

        @pl.when(g == 0)
        def _():
            acc[...] = contrib

        @pl.when(g > 0)
        def _():
            acc[...] += contrib
        return carry

    lax.fori_loop(0, N_GROUPS, group, 0)
    y_ref[...] = _ln_rows(ALPHA * x + acc[...], ln_ref[0:1, :], ln_ref[1:2, :])


def _moe(x, wr, br, wg, wu, wd, ln, tile):
    N = x.shape[0]
    tok = pl.BlockSpec((tile, D_MODEL), lambda i: (i, 0))
    return pl.pallas_call(
        _moe_kernel,
        grid=(N // tile,),
        in_specs=[tok, _const_spec(wr.shape), _const_spec(br.shape), _const_spec(wg.shape),
                  _const_spec(wu.shape), _const_spec(wd.shape), _const_spec(ln.shape)],
        out_specs=tok,
        out_shape=jax.ShapeDtypeStruct((N, D_MODEL), _F32),
        scratch_shapes=[
            pltpu.VMEM((tile, ROUTE_W), _F32),
            pltpu.VMEM((tile, EXPERTS_PER_GROUP * D_EXPERT), _BF),
            pltpu.VMEM((tile, D_MODEL), _F32),
        ],
        compiler_params=pltpu.CompilerParams(
            dimension_semantics=("arbitrary",), vmem_limit_bytes=VMEM_LIMIT),
        name="moe",
    )(x, wr, br, wg, wu, wd, ln)


DEST_BITS = 6
CODE_ROWS = SUBLANES


def _row_copy(src_ref, s, dst_ref, d, sem):
    return pltpu.make_async_copy(src_ref.at[pl.ds(s, 1)], dst_ref.at[pl.ds(d, 1)], sem)


def _route_kernel(x_ref, wr_ref, br_ref, code_ref, cnt_ref, ltri, cnt_s):
    T = T_DISPATCH
    i = pl.program_id(0)
    lane = lax.broadcasted_iota(jnp.int32, (T, ROUTE_W), 1)

    @pl.when(i == 0)
    def _():
        cnt_s[...] = jnp.zeros((1, ROUTE_W), _F32)
        r = lax.broadcasted_iota(jnp.int32, (T, T), 0)
        c = lax.broadcasted_iota(jnp.int32, (T, T), 1)
        ltri[...] = jnp.where(c < r, 1.0, 0.0).astype(_BF)

    logits = _dot(x_ref[...].astype(_BF), wr_ref[...]) + br_ref[...]
    glog = jnp.where(lane < N_GROUPS, logits, -jnp.inf)
    gmax = jnp.max(glog, axis=-1, keepdims=True)
    gsel = jnp.min(jnp.where(glog == gmax, lane, ROUTE_W), axis=-1, keepdims=True)
    onehot = lane == gsel
    onehot_f = jnp.where(onehot, 1.0, 0.0)
    rank = _dot(ltri[...], onehot_f.astype(_BF))
    cnt = cnt_s[...]
    loc = jnp.sum(jnp.where(onehot, cnt + rank, 0.0), axis=-1, keepdims=True).astype(jnp.int32)
    cnt_new = cnt + jnp.sum(onehot_f, axis=0, keepdims=True)
    cnt_s[...] = cnt_new
    cnt_ref[...] = cnt_new.astype(jnp.int32)

    mask = (1 << DEST_BITS) - 1
    code = jnp.where(lane == 0, loc & mask,
                     jnp.where(lane == 1, (loc >> DEST_BITS) & mask,
                               jnp.where(lane == 2, loc >> (2 * DEST_BITS),
                                         jnp.where(lane == 3, gsel, 0))))
    sel = (lax.broadcasted_iota(jnp.int32, (CODE_ROWS, ROUTE_W), 0)
           == lax.broadcasted_iota(jnp.int32, (CODE_ROWS, ROUTE_W), 1))
    rows = _dot_t(jnp.where(sel, 1.0, 0.0).astype(_BF), code.astype(_F32).astype(_BF))
    code_ref[0] = rows.astype(jnp.int32)


def _route_tokens(x, wr, br):
    N = x.shape[0]
    T = T_DISPATCH
    return pl.pallas_call(
        _route_kernel,
        grid=(N // T,),
        in_specs=[pl.BlockSpec((T, D_MODEL), lambda i: (i, 0)),
                  _const_spec(wr.shape), _const_spec(br.shape)],
        out_specs=[pl.BlockSpec((1, CODE_ROWS, T), lambda i: (i, 0, 0)),
                   pl.BlockSpec((1, ROUTE_W), lambda i: (0, 0))],
        out_shape=[jax.ShapeDtypeStruct((N // T, CODE_ROWS, T), jnp.int32),
                   jax.ShapeDtypeStruct((1, ROUTE_W), jnp.int32)],
        scratch_shapes=[
            pltpu.VMEM((T, T), _BF),
            pltpu.VMEM((1, ROUTE_W), _F32),
        ],
        compiler_params=pltpu.CompilerParams(
            dimension_semantics=("arbitrary",), vmem_limit_bytes=VMEM_LIMIT),
        name="moe_route",
    )(x, wr, br)


def _scatter_kernel(off_ref, code_ref, x_ref, xs_ref, dest_ref, dest_sm, sem_idx, sem_rows):
    T = T_DISPATCH
    code = code_ref[0]
    loc = code[0:1] + (code[1:2] << DEST_BITS) + (code[2:3] << (2 * DEST_BITS))
    grp = code[3:4]
    off = jnp.zeros((1, T), jnp.int32)
    for g in range(N_GROUPS):
        off = jnp.where(grp == g, off_ref[g], off)
    dest_ref[0] = off + loc

    idx_copy = pltpu.make_async_copy(dest_ref.at[0, 0], dest_sm, sem_idx)
    idx_copy.start()
    idx_copy.wait()

    def start(t, c):
        _row_copy(x_ref, t, xs_ref, dest_sm[t], sem_rows).start()
        return c

    lax.fori_loop(0, T, start, 0, unroll=8)

    def wait(t, c):
        _row_copy(x_ref, 0, xs_ref, 0, sem_rows).wait()
        return c

    lax.fori_loop(0, T, wait, 0, unroll=8)


def _scatter(off, code, x):
    N = x.shape[0]
    T = T_DISPATCH
    return pl.pallas_call(
        _scatter_kernel,
        grid=(N // T,),
        in_specs=[pl.BlockSpec(memory_space=pltpu.SMEM),
                  pl.BlockSpec((1, CODE_ROWS, T), lambda i: (i, 0, 0)),
                  pl.BlockSpec((T, D_MODEL), lambda i: (i, 0))],
        out_specs=[pl.BlockSpec(memory_space=pl.ANY),
                   pl.BlockSpec((1, 1, T), lambda i: (i, 0, 0))],
        out_shape=[jax.ShapeDtypeStruct((N, D_MODEL), _F32),
                   jax.ShapeDtypeStruct((N // T, 1, T), jnp.int32)],
        scratch_shapes=[pltpu.SMEM((T,), jnp.int32), pltpu.SemaphoreType.DMA, pltpu.SemaphoreType.DMA],
        compiler_params=pltpu.CompilerParams(
            dimension_semantics=("arbitrary",), vmem_limit_bytes=VMEM_LIMIT),
        name="moe_scatter",
    )(off, code, x)


def _route_in_group(logits, g):
    lane = lax.broadcasted_iota(jnp.int32, logits.shape, 1)
    big = jnp.int32(ROUTE_W)
    glog = jnp.where(lane < N_GROUPS, logits, -jnp.inf)
    gmax = jnp.max(glog, axis=-1, keepdims=True)
    lg = jnp.sum(jnp.where(lane == g, logits, 0.0), axis=-1, keepdims=True)
    gw = jnp.exp(lg - gmax) / jnp.sum(jnp.exp(glog - gmax), axis=-1, keepdims=True)
    e_lo = N_GROUPS + g * EXPERTS_PER_GROUP
    in_grp = (lane >= e_lo) & (lane < e_lo + EXPERTS_PER_GROUP)
    elog = jnp.where(in_grp, logits, -jnp.inf)
    v1 = jnp.max(elog, axis=-1, keepdims=True)
    i1 = jnp.min(jnp.where(elog == v1, lane, big), axis=-1, keepdims=True)
    elog2 = jnp.where(lane == i1, -jnp.inf, elog)
    v2 = jnp.max(elog2, axis=-1, keepdims=True)
    i2 = jnp.min(jnp.where(elog2 == v2, lane, big), axis=-1, keepdims=True)
    e2 = jnp.exp(v2 - v1)
    w1 = gw / (1.0 + e2)
    w2 = gw * e2 / (1.0 + e2)
    return jnp.where(lane == i1, w1, jnp.where(lane == i2, w2, 0.0))


def _moe_grouped_kernel(bidx_ref, bgrp_ref, blo_ref, bhi_ref, bfirst_ref, nstep_ref,
                        x_ref, wr_ref, br_ref, wg_ref, wu_ref, wd_ref, ln_ref, y_ref, hsc):
    i = pl.program_id(0)

    @pl.when(i < nstep_ref[0])
    def _():
        g = bgrp_ref[i]
        x = x_ref[...]
        xb = x.astype(_BF)
        gates = _route_in_group(_dot(xb, wr_ref[...]) + br_ref[...], g)
        lane = lax.broadcasted_iota(jnp.int32, (1, ROUTE_W), 1)
        for e in range(EXPERTS_PER_GROUP):
            hg = _dot(xb, wg_ref[e])
            hu = _dot(xb, wu_ref[e])
            gate = jnp.sum(jnp.where(lane == N_GROUPS + g * EXPERTS_PER_GROUP + e, gates, 0.0),
                           axis=-1, keepdims=True)
            hsc[:, e * D_EXPERT:(e + 1) * D_EXPERT] = (jax.nn.silu(hg) * hu * gate).astype(_BF)
        y = _dot(hsc[...], wd_ref[0])
        y_new = _ln_rows(ALPHA * x + y, ln_ref[0:1, :], ln_ref[1:2, :])

        @pl.when(bfirst_ref[i] == 1)
        def _():
            y_ref[...] = y_new

        @pl.when(bfirst_ref[i] == 0)
        def _():
            row = lax.broadcasted_iota(jnp.int32, (BM, 1), 0)
            mine = (row >= blo_ref[i]) & (row < bhi_ref[i])
            y_ref[...] = jnp.where(mine, y_new, y_ref[...])


def _moe_grouped(tables, xs, wr, br, wg, wu, wd, ln):
    nsteps = tables[0].shape[0]
    tok = pl.BlockSpec((BM, D_MODEL), lambda i, bidx, *_: (bidx[i], 0))
    const = lambda shape: pl.BlockSpec(shape, lambda i, *_: (0,) * len(shape), pipeline_mode=pl.Buffered(1))
    grp3 = lambda shape: pl.BlockSpec(shape, lambda i, bidx, bgrp, *_: (bgrp[i], 0, 0))
    return pl.pallas_call(
        _moe_grouped_kernel,
        grid_spec=pltpu.PrefetchScalarGridSpec(
            num_scalar_prefetch=len(tables),
            grid=(nsteps,),
            in_specs=[tok, const(wr.shape), const(br.shape),
                      grp3((EXPERTS_PER_GROUP, D_MODEL, D_EXPERT)),
                      grp3((EXPERTS_PER_GROUP, D_MODEL, D_EXPERT)),
                      grp3((1, EXPERTS_PER_GROUP * D_EXPERT, D_MODEL)),
                      const(ln.shape)],
            out_specs=tok,
            scratch_shapes=[pltpu.VMEM((BM, EXPERTS_PER_GROUP * D_EXPERT), _BF)],
        ),
        out_shape=jax.ShapeDtypeStruct(xs.shape, _F32),
        compiler_params=pltpu.CompilerParams(
            dimension_semantics=("arbitrary",), vmem_limit_bytes=VMEM_LIMIT),
        name="moe_grouped",
    )(*tables, xs, wr, br, wg, wu, wd, ln)


def _combine_kernel(dest_ref, ys_ref, out_ref, dest_sm, sem_idx, sem_rows):
    T = T_DISPATCH
    idx_copy = pltpu.make_async_copy(dest_ref.at[0, 0], dest_sm, sem_idx)
    idx_copy.start()
    idx_copy.wait()

    def start(t, c):
        _row_copy(ys_ref, dest_sm[t], out_ref, t, sem_rows).start()
        return c

    lax.fori_loop(0, T, start, 0, unroll=8)

    def wait(t, c):
        _row_copy(ys_ref, 0, out_ref, 0, sem_rows).wait()
        return c

    lax.fori_loop(0, T, wait, 0, unroll=8)


def _combine(dest, ys):
    nt, _, T = dest.shape
    return pl.pallas_call(
        _combine_kernel,
        grid=(nt,),
        in_specs=[pl.BlockSpec((1, 1, T), lambda i: (i, 0, 0)),
                  pl.BlockSpec(memory_space=pl.ANY)],
        out_specs=pl.BlockSpec((T, D_MODEL), lambda i: (i, 0)),
        out_shape=jax.ShapeDtypeStruct((nt * T, D_MODEL), _F32),
        scratch_shapes=[pltpu.SMEM((T,), jnp.int32), pltpu.SemaphoreType.DMA, pltpu.SemaphoreType.DMA],
        compiler_params=pltpu.CompilerParams(
            dimension_semantics=("arbitrary",), vmem_limit_bytes=VMEM_LIMIT),
        name="moe_combine",
    )(dest, ys)


def _step_tables(cnt, nsteps):
    n = cnt[0, :N_GROUPS]
    off = jnp.cumsum(n) - n
    b_lo = off // BM
    nb = jnp.where(n > 0, (off + n - 1) // BM - b_lo + 1, 0)
    ends = jnp.cumsum(nb)
    total = ends[-1]
    step = jnp.minimum(jnp.arange(nsteps, dtype=jnp.int32), total - 1)
    grp = jnp.sum(step[:, None] >= ends[None, :], axis=1).astype(jnp.int32)
    bidx = b_lo[grp] + step - (ends - nb)[grp]
    lo = jnp.clip(off[grp] - bidx * BM, 0, BM)
    hi = jnp.clip(off[grp] + n[grp] - bidx * BM, 0, BM)
    first = jnp.concatenate([jnp.ones((1,), jnp.int32), (bidx[1:] != bidx[:-1]).astype(jnp.int32)])
    tables = tuple(a.astype(jnp.int32) for a in (bidx, grp, lo, hi, first, total[None]))
    return off.astype(jnp.int32), tables


def _moe_sorted(x, wr, br, wg, wu, wd, ln):
    N = x.shape[0]
    code, cnt = _route_tokens(x, wr, br)
    off, tables = _step_tables(cnt, N // BM + N_GROUPS - 1)
    xs, dest = _scatter(off, code, x)
    ys = _moe_grouped(tables, xs, wr, br, wg, wu, wd, ln)
    return _combine(dest, ys)


def _q_perm():
    cols = []
    for j in range(N_QTILES):
        for h in (j, GQ + j):
            cols.extend(range(h * HEAD_DIM, (h + 1) * HEAD_DIM))
    return jnp.asarray(cols, jnp.int32)


def _block_diag(w):
    eye = jnp.eye(LRU_BLOCKS, dtype=w.dtype)
    return jnp.einsum('nij,nm->nimj', w, eye).reshape(D_LRU, D_LRU)


def kernel(x_prompt, x_sample, cache_k, cache_v, state_lru_h, state_lru_conv, state_sc_conv, w_in, attn_sinks, lru_conv_w, lru_conv_b, lru_w_a, lru_b_a, lru_w_x, lru_b_x, lru_lambda, sc_conv_w, g_mix, w_out, ln1_g, ln1_b, w_grp, b_grp, w_route, b_route, w_gate, w_up, w_down, ln2_g, ln2_b):
    B, S, _ = x_prompt.shape
    DB, TS, _ = x_sample.shape
    assert S % T_PROMPT == 0 and (B * S) % T_DISPATCH == 0 and (B * S) % BM == 0 and DB % BT_SAMPLE == 0
    assert B * S <= (1 << (3 * DEST_BITS))
    assert TS <= SUBLANES and cache_k.shape[2] == WINDOW
    perm = _q_perm()

    win = jnp.concatenate([jnp.take(w_in[:, :, :Q_W], perm, axis=2), w_in[:, :, Q_W:]], axis=2).astype(_BF)
    wout = jnp.concatenate([jnp.take(w_out[:, :Q_W], perm, axis=1), w_out[:, Q_W:]], axis=1).astype(_BF)
    gmix = jnp.concatenate([jnp.take(g_mix[:, :Q_W], perm, axis=1), g_mix[:, Q_W:]], axis=1)[:, None, :]
    wlru = jnp.concatenate([jax.vmap(_block_diag)(lru_w_a), jax.vmap(_block_diag)(lru_w_x)], axis=2).astype(_BF)
    v256 = jnp.stack([lru_conv_b, lru_b_a, lru_b_x, lru_lambda], axis=1)
    ln1 = jnp.stack([ln1_g, ln1_b], axis=1)
    ln2 = jnp.stack([ln2_g, ln2_b], axis=1)
    pad = ROUTE_W - N_GROUPS - N_EXPERTS
    wr = jnp.pad(jnp.concatenate([w_grp, w_route], axis=2), ((0, 0), (0, 0), (0, pad))).astype(_BF)
    br = jnp.pad(jnp.concatenate([b_grp, b_route], axis=1), ((0, 0), (0, pad)))[:, None, :]
    wg = w_gate.astype(_BF)
    wu = w_up.astype(_BF)
    wd = w_down.reshape(DEPTH, N_GROUPS, EXPERTS_PER_GROUP * D_EXPERT, D_MODEL).astype(_BF)

    yp = x_prompt
    ys = jnp.swapaxes(x_sample, 0, 1)
    outs_p = [[] for _ in range(5)]
    outs_s = [[] for _ in range(5)]
    for l in range(DEPTH):
        mix_w = (attn_sinks[l], win[l], wlru[l], wout[l], lru_conv_w[l], v256[l], sc_conv_w[l], gmix[l], ln1[l])
        moe_w = (wr[l], br[l], wg[l], wu[l], wd[l], ln2[l])
        yp, kp, vp, hp, lbp, sbp = _mixer_prompt(yp, *mix_w)
        yp = _moe_sorted(yp.reshape(B * S, D_MODEL), *moe_w).reshape(B, S, D_MODEL)
        ys, ks, vs, hs, lbs, sbs = _mixer_sample(
            ys, cache_k[l].reshape(DB, WINDOW, KV_W), cache_v[l].reshape(DB, WINDOW, KV_W),
            state_lru_h[l], state_lru_conv[l].reshape(DB, -1), state_sc_conv[l].reshape(DB, -1), *mix_w)
        ys = _moe(ys.reshape(TS * DB, D_MODEL), *moe_w, TS * DB).reshape(TS, DB, D_MODEL)
        for lst, arr in zip(outs_p, (kp.reshape(B, WINDOW, N_KV_HEADS, HEAD_DIM),
                                     vp.reshape(B, WINDOW, N_KV_HEADS, HEAD_DIM),
                                     hp.reshape(B, D_LRU), lbp, sbp)):
            lst.append(arr)
        for lst, arr in zip(outs_s, (ks.reshape(DB, WINDOW, N_KV_HEADS, HEAD_DIM),
                                     vs.reshape(DB, WINDOW, N_KV_HEADS, HEAD_DIM),
                                     hs, lbs.reshape(DB, LRU_CONV - 1, D_LRU),
                                     sbs.reshape(DB, SC_CONV - 1, D_SC))):
            lst.append(arr)
    return (yp, jnp.swapaxes(ys, 0, 1),
            *[jnp.stack(o) for o in outs_p], *[jnp.stack(o) for o in outs_s])
```

```python
import functools
import math

import jax
import jax.numpy as jnp
from jax import lax
from jax.experimental import pallas as pl
from jax.experimental.pallas import tpu as pltpu

D_MODEL = 1024
DEPTH = 2
HEAD_DIM = 64
N_HEADS = 8
N_KV_HEADS = 2
GQ = N_HEADS // N_KV_HEADS
WINDOW = 128
ALIBI_MAX = 8.0
D_LRU = 256
LRU_BLOCKS = 4
LRU_BLOCK_W = D_LRU // LRU_BLOCKS
LRU_CONV = 4
LRU_C = 8.0
D_SC = 256
SC_CONV = 3
Q_W = N_HEADS * HEAD_DIM
KV_W = N_KV_HEADS * HEAD_DIM
IN_W = Q_W + 2 * KV_W + 2 * D_LRU + 3 * D_SC
N_GROUPS = 4
EXPERTS_PER_GROUP = 4
N_EXPERTS = N_GROUPS * EXPERTS_PER_GROUP
D_EXPERT = 256
ALPHA = (2 * DEPTH) ** 0.25
LN_EPS = 1e-5
RMS_EPS = 1e-6
NEG_INF = -1e30

O_K = Q_W
O_V = O_K + KV_W
O_XR = O_V + KV_W
O_GR = O_XR + D_LRU
O_U = O_GR + D_LRU
O_BG = O_U + D_SC
O_CG = O_BG + D_SC

LANES = 128
SUBLANES = 8
N_QTILES = Q_W // LANES
ROUTE_W = LANES
VMEM_LIMIT = 56 * 1024 * 1024

T_PROMPT = 512
T_DISPATCH = 512
T_PERM = 2048
TOK_ROWS = D_MODEL // LANES
BM = 512
BT_SAMPLE = 32
KPAD = WINDOW + SUBLANES

_BF = jnp.bfloat16
_F32 = jnp.float32


def _slopes():
    return [2.0 ** (-ALIBI_MAX * h / N_HEADS) for h in range(1, N_HEADS + 1)]


def _dot(a, b):
    return jnp.dot(a, b, preferred_element_type=_F32)


def _dot_t(a, b):
    return lax.dot_general(a, b, (((1,), (1,)), ((), ())), preferred_element_type=_F32)


def _rms_rows(y, g):
    return y * lax.rsqrt(jnp.mean(y * y, axis=-1, keepdims=True) + RMS_EPS) * g


def _ln_rows(x, g, b):
    mu = jnp.mean(x, axis=-1, keepdims=True)
    xc = x - mu
    var = jnp.mean(xc * xc, axis=-1, keepdims=True)
    return xc * lax.rsqrt(var + LN_EPS) * g + b


def _lru_coeffs(xcb, wlru, b_a, b_x, lam):
    gates = _dot(xcb.astype(_BF), wlru)
    r = jax.nn.sigmoid(gates[:, :D_LRU] + b_a)
    i = jax.nn.sigmoid(gates[:, D_LRU:] + b_x)
    log_a = (-LRU_C * jax.nn.softplus(-lam)) * r
    a = jnp.exp(log_a)
    b = jnp.sqrt(jnp.tanh(-log_a) * (1.0 + a * a)) * (i * xcb)
    return a, b


def _sink_softmax(s, sink):
    m = jnp.maximum(jnp.max(s, axis=-1, keepdims=True), sink)
    p = jnp.exp(s - m)
    denom = jnp.sum(p, axis=-1, keepdims=True) + jnp.exp(sink - m)
    return p, 1.0 / denom


def _mixer_prompt_kernel(sinks_ref, x_ref, win_ref, wlru_ref, wout_ref, cw_ref, v256_ref, scw_ref,
                         gmix_ref, ln_ref,
                         y_ref, knew_ref, vnew_ref, hnew_ref, lrubuf_ref, scbuf_ref,
                         proj, qsc, klo, khi, vext, xrext, zext, hst, a_s, b_s, hl_s, pc_s, bias_s, mixed):
    T = T_PROMPT
    nblk = T // WINDOW
    s_idx = pl.program_id(1)
    lane = lax.broadcasted_iota(jnp.int32, (1, LANES), 1)
    lo_lane = lane < HEAD_DIM

    @pl.when(s_idx == 0)
    def _():
        klo[0:WINDOW, :] = jnp.zeros((WINDOW, LANES), _BF)
        khi[0:WINDOW, :] = jnp.zeros((WINDOW, LANES), _BF)
        vext[0:WINDOW, :] = jnp.zeros((WINDOW, LANES), _BF)
        xrext[0:SUBLANES, :] = jnp.zeros((SUBLANES, D_LRU), _F32)
        zext[0:SUBLANES, :] = jnp.zeros((SUBLANES, D_SC), _F32)
        hst[...] = jnp.zeros((1, D_LRU), _F32)
        qi = lax.broadcasted_iota(jnp.int32, (WINDOW, 2 * WINDOW), 0)
        kc = lax.broadcasted_iota(jnp.int32, (WINDOW, 2 * WINDOW), 1)
        dist = qi + WINDOW - kc
        valid = (dist >= 0) & (dist < WINDOW)
        valid_first = valid & (kc >= WINDOW)
        distf = dist.astype(_F32)
        for h, slope in enumerate(_slopes()):
            bias = -slope * distf
            bias_s[0, h] = jnp.where(valid, bias, NEG_INF)
            bias_s[1, h] = jnp.where(valid_first, bias, NEG_INF)

    proj[...] = _dot(x_ref[0].astype(_BF), win_ref[...])

    for n in range(nblk):
        rows = slice(n * WINDOW, (n + 1) * WINDOW)
        for j in range(N_QTILES):
            qsc[n, j * WINDOW:(j + 1) * WINDOW, :] = (
                proj[rows, j * LANES:(j + 1) * LANES] * (HEAD_DIM ** -0.5)).astype(_BF)
    k = proj[:, O_K:O_K + KV_W]
    v = proj[:, O_V:O_V + KV_W]
    klo[WINDOW:, :] = jnp.where(lo_lane, k, 0.0).astype(_BF)
    khi[WINDOW:, :] = jnp.where(lo_lane, 0.0, k).astype(_BF)
    vext[WINDOW:, :] = v.astype(_BF)
    knew_ref[0] = k[T - WINDOW:, :]
    vnew_ref[0] = v[T - WINDOW:, :]

    def attn_block(n, carry):
        r0 = pl.multiple_of(n * WINDOW, WINDOW)
        first = jnp.logical_and(s_idx == 0, n == 0).astype(jnp.int32)
        qs = qsc[n]
        kl = klo[pl.ds(r0, 2 * WINDOW), :]
        kh = khi[pl.ds(r0, 2 * WINDOW), :]
        vv = vext[pl.ds(r0, 2 * WINDOW), :]
        s_lo = _dot_t(qs, kl)
        s_hi = _dot_t(qs, kh)
        p_lo, p_hi, inv_lo, inv_hi = [], [], [], []
        for j in range(N_QTILES):
            rj = slice(j * WINDOW, (j + 1) * WINDOW)
            p, inv = _sink_softmax(s_lo[rj] + bias_s[first, j], sinks_ref[j])
            p_lo.append(p.astype(_BF))
            inv_lo.append(inv)
            p, inv = _sink_softmax(s_hi[rj] + bias_s[first, GQ + j], sinks_ref[GQ + j])
            p_hi.append(p.astype(_BF))
            inv_hi.append(inv)
        o_lo = _dot(jnp.concatenate(p_lo, axis=0), vv)
        o_hi = _dot(jnp.concatenate(p_hi, axis=0), vv)
        for j in range(N_QTILES):
            rj = slice(j * WINDOW, (j + 1) * WINDOW)
            mixed[pl.ds(r0, WINDOW), j * LANES:(j + 1) * LANES] = jnp.where(
                lo_lane, o_lo[rj] * inv_lo[j], o_hi[rj] * inv_hi[j])
        return carry

    lax.fori_loop(0, nblk, attn_block, 0)
    klo[0:WINDOW, :] = klo[T:T + WINDOW, :]
    khi[0:WINDOW, :] = khi[T:T + WINDOW, :]
    vext[0:WINDOW, :] = vext[T:T + WINDOW, :]

    xr = proj[:, O_XR:O_XR + D_LRU]
    xrext[SUBLANES:, :] = xr
    xc = xr * cw_ref[LRU_CONV - 1:LRU_CONV, :]
    for kk in range(1, LRU_CONV):
        xc = xc + xrext[pl.ds(SUBLANES - kk, T), :] * cw_ref[LRU_CONV - 1 - kk:LRU_CONV - kk, :]
    xcb = xc + v256_ref[0:1, :]
    a, b = _lru_coeffs(xcb, wlru_ref[...], v256_ref[1:2, :], v256_ref[2:3, :], v256_ref[3:4, :])
    NH = D_LRU // LANES
    C = T // SUBLANES
    PITCH = C + SUBLANES
    for hh in range(NH):
        for c in range(SUBLANES):
            a_s[hh, c * PITCH:c * PITCH + C, :] = a[c * C:(c + 1) * C, hh * LANES:(hh + 1) * LANES]
            b_s[hh, c * PITCH:c * PITCH + C, :] = b[c * C:(c + 1) * C, hh * LANES:(hh + 1) * LANES]
    lrubuf_ref[0] = xrext[pl.ds(T + SUBLANES - (LRU_CONV - 1), LRU_CONV - 1), :]
    xrext[0:SUBLANES, :] = xrext[T:T + SUBLANES, :]

    def scan_step(s, carry):
        out = []
        for hh in range(NH):
            h, p = carry[hh]
            at = a_s[hh, pl.ds(s, SUBLANES, stride=PITCH), :]
            bt = b_s[hh, pl.ds(s, SUBLANES, stride=PITCH), :]
            h = at * h + bt
            p = at * p
            hl_s[hh, pl.ds(s, SUBLANES, stride=PITCH), :] = h
            pc_s[hh, pl.ds(s, SUBLANES, stride=PITCH), :] = p
            out.append((h, p))
        return tuple(out)

    init = tuple((jnp.zeros((SUBLANES, LANES), _F32), jnp.ones((SUBLANES, LANES), _F32))
                 for _ in range(NH))
    ends = lax.fori_loop(0, C, scan_step, init)
    h_end = jnp.concatenate([e[0] for e in ends], axis=-1)
    p_end = jnp.concatenate([e[1] for e in ends], axis=-1)
    h_in = hst[...]
    gmix_lru = gmix_ref[:, Q_W:Q_W + D_LRU]
    for c in range(SUBLANES):
        rows = slice(c * C, (c + 1) * C)
        prow = slice(c * PITCH, c * PITCH + C)
        hl = jnp.concatenate([hl_s[hh, prow, :] for hh in range(NH)], axis=-1)
        pc = jnp.concatenate([pc_s[hh, prow, :] for hh in range(NH)], axis=-1)
        hs = hl + pc * h_in
        y_lru = hs * jax.nn.gelu(proj[rows, O_GR:O_GR + D_LRU])
        mixed[rows, Q_W:Q_W + D_LRU] = _rms_rows(y_lru, gmix_lru)
        h_in = p_end[c:c + 1, :] * h_in + h_end[c:c + 1, :]
    hst[...] = h_in
    hnew_ref[0] = h_in

    z = proj[:, O_CG:O_CG + D_SC] * proj[:, O_U:O_U + D_SC]
    zext[SUBLANES:, :] = z
    yc = z * scw_ref[SC_CONV - 1:SC_CONV, :]
    for kk in range(1, SC_CONV):
        yc = yc + zext[pl.ds(SUBLANES - kk, T), :] * scw_ref[SC_CONV - 1 - kk:SC_CONV - kk, :]
    y_sc = proj[:, O_BG:O_BG + D_SC] * yc
    mixed[:, Q_W + D_LRU:] = _rms_rows(y_sc, gmix_ref[:, Q_W + D_LRU:])
    scbuf_ref[0] = zext[pl.ds(T + SUBLANES - (SC_CONV - 1), SC_CONV - 1), :]
    zext[0:SUBLANES, :] = zext[T:T + SUBLANES, :]

    mixed[:, 0:Q_W] = _rms_rows(mixed[:, 0:Q_W], gmix_ref[:, 0:Q_W])
    m = _dot(mixed[...].astype(_BF), wout_ref[...])
    y_ref[0] = _ln_rows(ALPHA * x_ref[0] + m, ln_ref[0:1, :], ln_ref[1:2, :])


def _const_spec(shape):
    nd = len(shape)
    return pl.BlockSpec(shape, lambda *_: (0,) * nd, pipeline_mode=pl.Buffered(1))


def _mixer_prompt(x, sinks, win, wlru, wout, cw, v256, scw, gmix, ln):
    B, S, _ = x.shape
    T = T_PROMPT
    grid = (B, S // T)
    seq_out = lambda shape: pl.BlockSpec((1,) + shape, lambda b, s: (b, 0, 0))
    return pl.pallas_call(
        _mixer_prompt_kernel,
        grid=grid,
        in_specs=[
            pl.BlockSpec(memory_space=pltpu.SMEM),
            pl.BlockSpec((1, T, D_MODEL), lambda b, s: (b, s, 0)),
            _const_spec(win.shape), _const_spec(wlru.shape), _const_spec(wout.shape),
            _const_spec(cw.shape), _const_spec(v256.shape), _const_spec(scw.shape),
            _const_spec(gmix.shape), _const_spec(ln.shape),
        ],
        out_specs=[
            pl.BlockSpec((1, T, D_MODEL), lambda b, s: (b, s, 0)),
            seq_out((WINDOW, KV_W)), seq_out((WINDOW, KV_W)), seq_out((1, D_LRU)),
            seq_out((LRU_CONV - 1, D_LRU)), seq_out((SC_CONV - 1, D_SC)),
        ],
        out_shape=[
            jax.ShapeDtypeStruct((B, S, D_MODEL), _F32),
            jax.ShapeDtypeStruct((B, WINDOW, KV_W), _F32),
            jax.ShapeDtypeStruct((B, WINDOW, KV_W), _F32),
            jax.ShapeDtypeStruct((B, 1, D_LRU), _F32),
            jax.ShapeDtypeStruct((B, LRU_CONV - 1, D_LRU), _F32),
            jax.ShapeDtypeStruct((B, SC_CONV - 1, D_SC), _F32),
        ],
        scratch_shapes=[
            pltpu.VMEM((T, IN_W), _F32),
            pltpu.VMEM((T // WINDOW, N_QTILES * WINDOW, LANES), _BF),
            pltpu.VMEM((T + WINDOW, LANES), _BF),
            pltpu.VMEM((T + WINDOW, LANES), _BF),
            pltpu.VMEM((T + WINDOW, LANES), _BF),
            pltpu.VMEM((T + SUBLANES, D_LRU), _F32),
            pltpu.VMEM((T + SUBLANES, D_SC), _F32),
            pltpu.VMEM((1, D_LRU), _F32),
            pltpu.VMEM((D_LRU // LANES, T + SUBLANES * SUBLANES, LANES), _F32),
            pltpu.VMEM((D_LRU // LANES, T + SUBLANES * SUBLANES, LANES), _F32),
            pltpu.VMEM((D_LRU // LANES, T + SUBLANES * SUBLANES, LANES), _F32),
            pltpu.VMEM((D_LRU // LANES, T + SUBLANES * SUBLANES, LANES), _F32),
            pltpu.VMEM((2, N_HEADS, WINDOW, 2 * WINDOW), _F32),
            pltpu.VMEM((T, D_MODEL), _F32),
        ],
        compiler_params=pltpu.CompilerParams(
            dimension_semantics=("arbitrary", "arbitrary"), vmem_limit_bytes=VMEM_LIMIT),
        name="mixer_prompt",
    )(sinks, x, win, wlru, wout, cw, v256, scw, gmix, ln)


def _mixer_sample_kernel(sinks_ref, x_ref, ck_ref, cv_ref, h0_ref, lbuf_ref, sbuf_ref,
                         win_ref, wlru_ref, wout_ref, cw_ref, v256_ref, scw_ref, gmix_ref, ln_ref,
                         y_ref, knew_ref, vnew_ref, hnew_ref, lrubuf_ref, scbuf_ref,
                         kx2, vx2, qs2, os2):
    TS = x_ref.shape[0]
    BT = BT_SAMPLE
    NQ = TS * N_QTILES
    lane = lax.broadcasted_iota(jnp.int32, (1, LANES), 1)
    lo_lane = lane < HEAD_DIM

    x = x_ref[...].reshape(TS * BT, D_MODEL)
    proj = _dot(x.astype(_BF), win_ref[...])

    for bb in range(BT):
        kx2[bb * KPAD:bb * KPAD + WINDOW, :] = ck_ref[bb]
        vx2[bb * KPAD:bb * KPAD + WINDOW, :] = cv_ref[bb]
    for t in range(SUBLANES):
        if t < TS:
            rows = slice(t * BT, (t + 1) * BT)
            kx2[pl.ds(WINDOW + t, BT, stride=KPAD), :] = proj[rows, O_K:O_K + KV_W]
            vx2[pl.ds(WINDOW + t, BT, stride=KPAD), :] = proj[rows, O_V:O_V + KV_W]
            for j in range(N_QTILES):
                qs2[pl.ds(t * N_QTILES + j, BT, stride=NQ), :] = (
                    proj[rows, j * LANES:(j + 1) * LANES] * (HEAD_DIM ** -0.5))
        else:
            kx2[pl.ds(WINDOW + t, BT, stride=KPAD), :] = jnp.zeros((BT, LANES), _F32)
            vx2[pl.ds(WINDOW + t, BT, stride=KPAD), :] = jnp.zeros((BT, LANES), _F32)
    for bb in range(BT):
        knew_ref[bb] = kx2[bb * KPAD + TS:bb * KPAD + TS + WINDOW, :]
        vnew_ref[bb] = vx2[bb * KPAD + TS:bb * KPAD + TS + WINDOW, :]

    kall = kx2[...].reshape(BT, KPAD, LANES)
    qs = qs2[...].reshape(BT, NQ, LANES).astype(_BF)
    k_lo = jnp.where(lo_lane, kall, 0.0).astype(_BF)
    k_hi = jnp.where(lo_lane, 0.0, kall).astype(_BF)
    vv = vx2[...].reshape(BT, KPAD, LANES).astype(_BF)
    s_lo = jnp.einsum('bqd,bkd->bqk', qs, k_lo, preferred_element_type=_F32)
    s_hi = jnp.einsum('bqd,bkd->bqk', qs, k_hi, preferred_element_type=_F32)
    ri = lax.broadcasted_iota(jnp.int32, (NQ, KPAD), 0)
    ci = lax.broadcasted_iota(jnp.int32, (NQ, KPAD), 1)
    tq = lax.shift_right_logical(ri, int(math.log2(N_QTILES)))
    jq = ri & (N_QTILES - 1)
    dist = WINDOW + tq - ci
    valid = (dist >= 0) & (dist < WINDOW)
    distf = dist.astype(_F32)
    rj = lax.broadcasted_iota(jnp.int32, (NQ, 1), 0) & (N_QTILES - 1)
    slope_lo = jnp.zeros((NQ, KPAD), _F32)
    slope_hi = jnp.zeros((NQ, KPAD), _F32)
    sink_lo = jnp.zeros((NQ, 1), _F32)
    sink_hi = jnp.zeros((NQ, 1), _F32)
    slopes = _slopes()
    for j in range(N_QTILES):
        slope_lo = jnp.where(jq == j, slopes[j], slope_lo)
        slope_hi = jnp.where(jq == j, slopes[GQ + j], slope_hi)
        sink_lo = jnp.where(rj == j, sinks_ref[j], sink_lo)
        sink_hi = jnp.where(rj == j, sinks_ref[GQ + j], sink_hi)
    bias_lo = jnp.where(valid, -slope_lo * distf, NEG_INF)
    bias_hi = jnp.where(valid, -slope_hi * distf, NEG_INF)
    p_lo, inv_lo = _sink_softmax(s_lo + bias_lo[None], sink_lo[None])
    p_hi, inv_hi = _sink_softmax(s_hi + bias_hi[None], sink_hi[None])
    o_lo = jnp.einsum('bqk,bkd->bqd', p_lo.astype(_BF), vv, preferred_element_type=_F32)
    o_hi = jnp.einsum('bqk,bkd->bqd', p_hi.astype(_BF), vv, preferred_element_type=_F32)
    o = jnp.where(lo_lane[None], o_lo * inv_lo, o_hi * inv_hi)
    os2[...] = o.reshape(BT * NQ, LANES)

    lbuf = lbuf_ref[...]
    sbuf = sbuf_ref[...]
    xr_hist = [lbuf[:, i * D_LRU:(i + 1) * D_LRU] for i in range(LRU_CONV - 1)]
    z_hist = [sbuf[:, i * D_SC:(i + 1) * D_SC] for i in range(SC_CONV - 1)]
    h = h0_ref[...]
    gmix = gmix_ref[...]
    mixed_rows = []
    for t in range(TS):
        rows = slice(t * BT, (t + 1) * BT)
        o_att = jnp.concatenate(
            [os2[pl.ds(t * N_QTILES + j, BT, stride=NQ), :] for j in range(N_QTILES)], axis=-1)
        xr_hist.append(proj[rows, O_XR:O_XR + D_LRU])
        xc = sum(xr_hist[t + kk] * cw_ref[kk:kk + 1, :] for kk in range(LRU_CONV))
        xcb = xc + v256_ref[0:1, :]
        a, b = _lru_coeffs(xcb, wlru_ref[...], v256_ref[1:2, :], v256_ref[2:3, :], v256_ref[3:4, :])
        h = a * h + b
        y_lru = h * jax.nn.gelu(proj[rows, O_GR:O_GR + D_LRU])
        z_hist.append(proj[rows, O_CG:O_CG + D_SC] * proj[rows, O_U:O_U + D_SC])
        yc = sum(z_hist[t + kk] * scw_ref[kk:kk + 1, :] for kk in range(SC_CONV))
        y_sc = proj[rows, O_BG:O_BG + D_SC] * yc
        mixed_rows.append(jnp.concatenate(
            [_rms_rows(o_att, gmix[:, 0:Q_W]),
             _rms_rows(y_lru, gmix[:, Q_W:Q_W + D_LRU]),
             _rms_rows(y_sc, gmix[:, Q_W + D_LRU:])], axis=-1))
    hnew_ref[...] = h
    lrubuf_ref[...] = jnp.concatenate(xr_hist[TS:], axis=-1)
    scbuf_ref[...] = jnp.concatenate(z_hist[TS:], axis=-1)
    mixed = jnp.concatenate(mixed_rows, axis=0)
    m = _dot(mixed.astype(_BF), wout_ref[...])
    y = _ln_rows(ALPHA * x + m, ln_ref[0:1, :], ln_ref[1:2, :])
    y_ref[...] = y.reshape(TS, BT, D_MODEL)


def _mixer_sample(x, ck, cv, h0, lbuf, sbuf, sinks, win, wlru, wout, cw, v256, scw, gmix, ln):
    TS, B, _ = x.shape
    BT = BT_SAMPLE
    rows2 = lambda w: pl.BlockSpec((BT, w), lambda i: (i, 0))
    cache = pl.BlockSpec((BT, WINDOW, KV_W), lambda i: (i, 0, 0))
    xspec = pl.BlockSpec((TS, BT, D_MODEL), lambda i: (0, i, 0))
    return pl.pallas_call(
        _mixer_sample_kernel,
        grid=(B // BT,),
        in_specs=[
            pl.BlockSpec(memory_space=pltpu.SMEM),
            xspec, cache, cache, rows2(D_LRU), rows2((LRU_CONV - 1) * D_LRU), rows2((SC_CONV - 1) * D_SC),
            _const_spec(win.shape), _const_spec(wlru.shape), _const_spec(wout.shape),
            _const_spec(cw.shape), _const_spec(v256.shape), _const_spec(scw.shape),
            _const_spec(gmix.shape), _const_spec(ln.shape),
        ],
        out_specs=[xspec, cache, cache, rows2(D_LRU), rows2((LRU_CONV - 1) * D_LRU),
                   rows2((SC_CONV - 1) * D_SC)],
        out_shape=[
            jax.ShapeDtypeStruct((TS, B, D_MODEL), _F32),
            jax.ShapeDtypeStruct((B, WINDOW, KV_W), _F32),
            jax.ShapeDtypeStruct((B, WINDOW, KV_W), _F32),
            jax.ShapeDtypeStruct((B, D_LRU), _F32),
            jax.ShapeDtypeStruct((B, (LRU_CONV - 1) * D_LRU), _F32),
            jax.ShapeDtypeStruct((B, (SC_CONV - 1) * D_SC), _F32),
        ],
        scratch_shapes=[
            pltpu.VMEM((BT * KPAD, LANES), _F32),
            pltpu.VMEM((BT * KPAD, LANES), _F32),
            pltpu.VMEM((BT * TS * N_QTILES, LANES), _F32),
            pltpu.VMEM((BT * TS * N_QTILES, LANES), _F32),
        ],
        compiler_params=pltpu.CompilerParams(
            dimension_semantics=("arbitrary",), vmem_limit_bytes=VMEM_LIMIT),
        name="mixer_sample",
    )(sinks, x, ck, cv, h0, lbuf, sbuf, win, wlru, wout, cw, v256, scw, gmix, ln)


def _route(logits):
    lane = lax.broadcasted_iota(jnp.int32, logits.shape, 1)
    big = jnp.int32(ROUTE_W)
    is_grp = lane < N_GROUPS
    glog = jnp.where(is_grp, logits, -jnp.inf)
    gmax = jnp.max(glog, axis=-1, keepdims=True)
    gsel = jnp.min(jnp.where(glog == gmax, lane, big), axis=-1, keepdims=True)
    gw = 1.0 / jnp.sum(jnp.exp(glog - gmax), axis=-1, keepdims=True)
    e_lo = N_GROUPS + gsel * EXPERTS_PER_GROUP
    in_grp = (lane >= e_lo) & (lane < e_lo + EXPERTS_PER_GROUP)
    elog = jnp.where(in_grp, logits, -jnp.inf)
    v1 = jnp.max(elog, axis=-1, keepdims=True)
    i1 = jnp.min(jnp.where(elog == v1, lane, big), axis=-1, keepdims=True)
    elog2 = jnp.where(lane == i1, -jnp.inf, elog)
    v2 = jnp.max(elog2, axis=-1, keepdims=True)
    i2 = jnp.min(jnp.where(elog2 == v2, lane, big), axis=-1, keepdims=True)
    e2 = jnp.exp(v2 - v1)
    w1 = gw / (1.0 + e2)
    w2 = gw * e2 / (1.0 + e2)
    return jnp.where(lane == i1, w1, jnp.where(lane == i2, w2, 0.0))


def _moe_kernel(x_ref, wr_ref, br_ref, wg_ref, wu_ref, wd_ref, ln_ref, y_ref, gates_s, hsc, acc):
    x = x_ref[...]
    xb = x.astype(_BF)
    gates_s[...] = _route(_dot(xb, wr_ref[...]) + br_ref[...])

    def group(g, carry):
        for i in range(EXPERTS_PER_GROUP):
            e = g * EXPERTS_PER_GROUP + i
            hg = _dot(xb, wg_ref[e])
            hu = _dot(xb, wu_ref[e])
            lane = lax.broadcasted_iota(jnp.int32, (1, ROUTE_W), 1)
            gate = jnp.sum(jnp.where(lane == N_GROUPS + e, gates_s[...], 0.0), axis=-1, keepdims=True)
            hsc[:, i * D_EXPERT:(i + 1) * D_EXPERT] = (jax.nn.silu(hg) * hu * gate).astype(_BF)
        contrib = _dot(hsc[...], wd_ref[g])

        @pl.when(g == 0)
        def _():
            acc[...] = contrib

        @pl.when(g > 0)
        def _():
            acc[...] += contrib
        return carry

    lax.fori_loop(0, N_GROUPS, group, 0)
    y_ref[...] = _ln_rows(ALPHA * x + acc[...], ln_ref[0:1, :], ln_ref[1:2, :])


def _moe(x, wr, br, wg, wu, wd, ln, tile):
    N = x.shape[0]
    tok = pl.BlockSpec((tile, D_MODEL), lambda i: (i, 0))
    return pl.pallas_call(
        _moe_kernel,
        grid=(N // tile,),
        in_specs=[tok, _const_spec(wr.shape), _const_spec(br.shape), _const_spec(wg.shape),
                  _const_spec(wu.shape), _const_spec(wd.shape), _const_spec(ln.shape)],
        out_specs=tok,
        out_shape=jax.ShapeDtypeStruct((N, D_MODEL), _F32),
        scratch_shapes=[
            pltpu.VMEM((tile, ROUTE_W), _F32),
            pltpu.VMEM((tile, EXPERTS_PER_GROUP * D_EXPERT), _BF),
            pltpu.VMEM((tile, D_MODEL), _F32),
        ],
        compiler_params=pltpu.CompilerParams(
            dimension_semantics=("arbitrary",), vmem_limit_bytes=VMEM_LIMIT),
        name="moe",
    )(x, wr, br, wg, wu, wd, ln)


DEST_BITS = 6
CODE_ROWS = SUBLANES


def _route_kernel(x_ref, wr_ref, br_ref, code_ref, cnt_ref, ltri, cnt_s):
    T = T_DISPATCH
    i = pl.program_id(0)
    lane = lax.broadcasted_iota(jnp.int32, (T, ROUTE_W), 1)

    @pl.when(i == 0)
    def _():
        cnt_s[...] = jnp.zeros((1, ROUTE_W), _F32)
        r = lax.broadcasted_iota(jnp.int32, (T, T), 0)
        c = lax.broadcasted_iota(jnp.int32, (T, T), 1)
        ltri[...] = jnp.where(c < r, 1.0, 0.0).astype(_BF)

    logits = _dot(x_ref[...].astype(_BF), wr_ref[...]) + br_ref[...]
    glog = jnp.where(lane < N_GROUPS, logits, -jnp.inf)
    gmax = jnp.max(glog, axis=-1, keepdims=True)
    gsel = jnp.min(jnp.where(glog == gmax, lane, ROUTE_W), axis=-1, keepdims=True)
    onehot = lane == gsel
    onehot_f = jnp.where(onehot, 1.0, 0.0)
    rank = _dot(ltri[...], onehot_f.astype(_BF))
    cnt = cnt_s[...]
    loc = jnp.sum(jnp.where(onehot, cnt + rank, 0.0), axis=-1, keepdims=True).astype(jnp.int32)
    cnt_new = cnt + jnp.sum(onehot_f, axis=0, keepdims=True)
    cnt_s[...] = cnt_new
    cnt_ref[...] = cnt_new.astype(jnp.int32)

    mask = (1 << DEST_BITS) - 1
    code = jnp.where(lane == 0, loc & mask,
                     jnp.where(lane == 1, (loc >> DEST_BITS) & mask,
                               jnp.where(lane == 2, loc >> (2 * DEST_BITS),
                                         jnp.where(lane == 3, gsel, 0))))
    sel = (lax.broadcasted_iota(jnp.int32, (CODE_ROWS, ROUTE_W), 0)
           == lax.broadcasted_iota(jnp.int32, (CODE_ROWS, ROUTE_W), 1))
    rows = _dot_t(jnp.where(sel, 1.0, 0.0).astype(_BF), code.astype(_F32).astype(_BF))
    code_ref[0] = rows.astype(jnp.int32)


def _route_tokens(x, wr, br):
    N = x.shape[0]
    T = T_DISPATCH
    return pl.pallas_call(
        _route_kernel,
        grid=(N // T,),
        in_specs=[pl.BlockSpec((T, D_MODEL), lambda i: (i, 0)),
                  _const_spec(wr.shape), _const_spec(br.shape)],
        out_specs=[pl.BlockSpec((1, CODE_ROWS, T), lambda i: (i, 0, 0)),
                   pl.BlockSpec((1, ROUTE_W), lambda i: (0, 0))],
        out_shape=[jax.ShapeDtypeStruct((N // T, CODE_ROWS, T), jnp.int32),
                   jax.ShapeDtypeStruct((1, ROUTE_W), jnp.int32)],
        scratch_shapes=[
            pltpu.VMEM((T, T), _BF),
            pltpu.VMEM((1, ROUTE_W), _F32),
        ],
        compiler_params=pltpu.CompilerParams(
            dimension_semantics=("arbitrary",), vmem_limit_bytes=VMEM_LIMIT),
        name="moe_route",
    )(x, wr, br)


def _to_token_rows(x, rows_ref):
    R = x.shape[0]
    for c in range(TOK_ROWS):
        rows_ref[pl.ds(c, R, stride=TOK_ROWS), :] = x[:, c * LANES:(c + 1) * LANES]


def _from_token_rows(rows_ref, R):
    return jnp.concatenate(
        [rows_ref[pl.ds(c, R, stride=TOK_ROWS), :] for c in range(TOK_ROWS)], axis=-1)


def _token_copy(src_ref, s, dst_ref, d, sem):
    return pltpu.make_async_copy(
        src_ref.at[pl.ds(pl.multiple_of(s * TOK_ROWS, TOK_ROWS), TOK_ROWS)],
        dst_ref.at[pl.ds(pl.multiple_of(d * TOK_ROWS, TOK_ROWS), TOK_ROWS)], sem)


def _load_dest(dest_ref, dest_sm, sem_idx):
    copies = [pltpu.make_async_copy(dest_ref.at[j, 0], dest_sm.at[pl.ds(j * T_DISPATCH, T_DISPATCH)], sem_idx)
              for j in range(T_PERM // T_DISPATCH)]
    for cp in copies:
        cp.start()
    for cp in copies:
        cp.wait()


def _scatter_kernel(off_ref, code_ref, x_ref, xs_ref, dest_ref, xrow, dest_sm, sem_idx, sem_rows):
    code = code_ref[...]
    loc = code[:, 0:1] + (code[:, 1:2] << DEST_BITS) + (code[:, 2:3] << (2 * DEST_BITS))
    grp = code[:, 3:4]
    off = jnp.zeros(loc.shape, jnp.int32)
    for g in range(N_GROUPS):
        off = jnp.where(grp == g, off_ref[g], off)
    dest_ref[...] = off + loc
    _to_token_rows(x_ref[...], xrow)
    _load_dest(dest_ref, dest_sm, sem_idx)

    def start(t, c):
        _token_copy(xrow, t, xs_ref, dest_sm[t], sem_rows).start()
        return c

    lax.fori_loop(0, T_PERM, start, 0, unroll=8)

    def wait(t, c):
        _token_copy(xrow, 0, xs_ref, 0, sem_rows).wait()
        return c

    lax.fori_loop(0, T_PERM, wait, 0, unroll=8)


def _scatter(off, code, x):
    N = x.shape[0]
    T = T_PERM
    nt = T // T_DISPATCH
    return pl.pallas_call(
        _scatter_kernel,
        grid=(N // T,),
        in_specs=[pl.BlockSpec(memory_space=pltpu.SMEM),
                  pl.BlockSpec((nt, CODE_ROWS, T_DISPATCH), lambda i: (i, 0, 0)),
                  pl.BlockSpec((T, D_MODEL), lambda i: (i, 0))],
        out_specs=[pl.BlockSpec(memory_space=pl.ANY),
                   pl.BlockSpec((nt, 1, T_DISPATCH), lambda i: (i, 0, 0))],
        out_shape=[jax.ShapeDtypeStruct((N * TOK_ROWS, LANES), _F32),
                   jax.ShapeDtypeStruct((N // T_DISPATCH, 1, T_DISPATCH), jnp.int32)],
        scratch_shapes=[pltpu.VMEM((T * TOK_ROWS, LANES), _F32),
                        pltpu.SMEM((T,), jnp.int32), pltpu.SemaphoreType.DMA, pltpu.SemaphoreType.DMA],
        compiler_params=pltpu.CompilerParams(
            dimension_semantics=("arbitrary",), vmem_limit_bytes=VMEM_LIMIT),
        name="moe_scatter",
    )(off, code, x)


def _route_in_group(logits, g):
    lane = lax.broadcasted_iota(jnp.int32, logits.shape, 1)
    big = jnp.int32(ROUTE_W)
    glog = jnp.where(lane < N_GROUPS, logits, -jnp.inf)
    gmax = jnp.max(glog, axis=-1, keepdims=True)
    lg = jnp.sum(jnp.where(lane == g, logits, 0.0), axis=-1, keepdims=True)
    gw = jnp.exp(lg - gmax) / jnp.sum(jnp.exp(glog - gmax), axis=-1, keepdims=True)
    e_lo = N_GROUPS + g * EXPERTS_PER_GROUP
    in_grp = (lane >= e_lo) & (lane < e_lo + EXPERTS_PER_GROUP)
    elog = jnp.where(in_grp, logits, -jnp.inf)
    v1 = jnp.max(elog, axis=-1, keepdims=True)
    i1 = jnp.min(jnp.where(elog == v1, lane, big), axis=-1, keepdims=True)
    elog2 = jnp.where(lane == i1, -jnp.inf, elog)
    v2 = jnp.max(elog2, axis=-1, keepdims=True)
    i2 = jnp.min(jnp.where(elog2 == v2, lane, big), axis=-1, keepdims=True)
    e2 = jnp.exp(v2 - v1)
    w1 = gw / (1.0 + e2)
    w2 = gw * e2 / (1.0 + e2)
    return jnp.where(lane == i1, w1, jnp.where(lane == i2, w2, 0.0))


def _moe_grouped_kernel(bidx_ref, bgrp_ref, blo_ref, bhi_ref, bfirst_ref, nstep_ref,
                        x_ref, wr_ref, br_ref, wg_ref, wu_ref, wd_ref, ln_ref, y_ref, hsc):
    i = pl.program_id(0)

    @pl.when(i < nstep_ref[0])
    def _():
        g = bgrp_ref[i]
        x = _from_token_rows(x_ref, BM)
        xb = x.astype(_BF)
        gates = _route_in_group(_dot(xb, wr_ref[...]) + br_ref[...], g)
        lane = lax.broadcasted_iota(jnp.int32, (1, ROUTE_W), 1)
        for e in range(EXPERTS_PER_GROUP):
            hg = _dot(xb, wg_ref[e])
            hu = _dot(xb, wu_ref[e])
            gate = jnp.sum(jnp.where(lane == N_GROUPS + g * EXPERTS_PER_GROUP + e, gates, 0.0),
                           axis=-1, keepdims=True)
            hsc[:, e * D_EXPERT:(e + 1) * D_EXPERT] = (jax.nn.silu(hg) * hu * gate).astype(_BF)
        y = _dot(hsc[...], wd_ref[0])
        y_new = _ln_rows(ALPHA * x + y, ln_ref[0:1, :], ln_ref[1:2, :])

        @pl.when(bfirst_ref[i] == 1)
        def _():
            _to_token_rows(y_new, y_ref)

        @pl.when(bfirst_ref[i] == 0)
        def _():
            row = lax.broadcasted_iota(jnp.int32, (BM, 1), 0)
            mine = (row >= blo_ref[i]) & (row < bhi_ref[i])
            _to_token_rows(jnp.where(mine, y_new, _from_token_rows(y_ref, BM)), y_ref)


def _moe_grouped(tables, xs, wr, br, wg, wu, wd, ln):
    nsteps = tables[0].shape[0]
    tok = pl.BlockSpec((BM * TOK_ROWS, LANES), lambda i, bidx, *_: (bidx[i], 0))
    const = lambda shape: pl.BlockSpec(shape, lambda i, *_: (0,) * len(shape), pipeline_mode=pl.Buffered(1))
    grp3 = lambda shape: pl.BlockSpec(shape, lambda i, bidx, bgrp, *_: (bgrp[i], 0, 0))
    return pl.pallas_call(
        _moe_grouped_kernel,
        grid_spec=pltpu.PrefetchScalarGridSpec(
            num_scalar_prefetch=len(tables),
            grid=(nsteps,),
            in_specs=[tok, const(wr.shape), const(br.shape),
                      grp3((EXPERTS_PER_GROUP, D_MODEL, D_EXPERT)),
                      grp3((EXPERTS_PER_GROUP, D_MODEL, D_EXPERT)),
                      grp3((1, EXPERTS_PER_GROUP * D_EXPERT, D_MODEL)),
                      const(ln.shape)],
            out_specs=tok,
            scratch_shapes=[pltpu.VMEM((BM, EXPERTS_PER_GROUP * D_EXPERT), _BF)],
        ),
        out_shape=jax.ShapeDtypeStruct(xs.shape, _F32),
        compiler_params=pltpu.CompilerParams(
            dimension_semantics=("arbitrary",), vmem_limit_bytes=VMEM_LIMIT),
        name="moe_grouped",
    )(*tables, xs, wr, br, wg, wu, wd, ln)


def _combine_kernel(dest_ref, ys_ref, out_ref, yrow, dest_sm, sem_idx, sem_rows):
    _load_dest(dest_ref, dest_sm, sem_idx)

    def start(t, c):
        _token_copy(ys_ref, dest_sm[t], yrow, t, sem_rows).start()
        return c

    lax.fori_loop(0, T_PERM, start, 0, unroll=8)

    def wait(t, c):
        _token_copy(ys_ref, 0, yrow, 0, sem_rows).wait()
        return c

    lax.fori_loop(0, T_PERM, wait, 0, unroll=8)
    out_ref[...] = _from_token_rows(yrow, T_PERM)


def _combine(dest, ys):
    N = dest.shape[0] * T_DISPATCH
    T = T_PERM
    nt = T // T_DISPATCH
    return pl.pallas_call(
        _combine_kernel,
        grid=(N // T,),
        in_specs=[pl.BlockSpec((nt, 1, T_DISPATCH), lambda i: (i, 0, 0)),
                  pl.BlockSpec(memory_space=pl.ANY)],
        out_specs=pl.BlockSpec((T, D_MODEL), lambda i: (i, 0)),
        out_shape=jax.ShapeDtypeStruct((N, D_MODEL), _F32),
        scratch_shapes=[pltpu.VMEM((T * TOK_ROWS, LANES), _F32),
                        pltpu.SMEM((T,), jnp.int32), pltpu.SemaphoreType.DMA, pltpu.SemaphoreType.DMA],
        compiler_params=pltpu.CompilerParams(
            dimension_semantics=("arbitrary",), vmem_limit_bytes=VMEM_LIMIT),
        name="moe_combine",
    )(dest, ys)


def _step_tables(cnt, nsteps):
    n = cnt[0, :N_GROUPS]
    off = jnp.cumsum(n) - n
    b_lo = off // BM
    nb = jnp.where(n > 0, (off + n - 1) // BM - b_lo + 1, 0)
    ends = jnp.cumsum(nb)
    total = ends[-1]
    step = jnp.minimum(jnp.arange(nsteps, dtype=jnp.int32), total - 1)
    grp = jnp.sum(step[:, None] >= ends[None, :], axis=1).astype(jnp.int32)
    bidx = b_lo[grp] + step - (ends - nb)[grp]
    lo = jnp.clip(off[grp] - bidx * BM, 0, BM)
    hi = jnp.clip(off[grp] + n[grp] - bidx * BM, 0, BM)
    first = jnp.concatenate([jnp.ones((1,), jnp.int32), (bidx[1:] != bidx[:-1]).astype(jnp.int32)])
    tables = tuple(a.astype(jnp.int32) for a in (bidx, grp, lo, hi, first, total[None]))
    return off.astype(jnp.int32), tables


def _moe_sorted(x, wr, br, wg, wu, wd, ln):
    N = x.shape[0]
    code, cnt = _route_tokens(x, wr, br)
    off, tables = _step_tables(cnt, N // BM + N_GROUPS - 1)
    xs, dest = _scatter(off, code, x)
    ys = _moe_grouped(tables, xs, wr, br, wg, wu, wd, ln)
    return _combine(dest, ys)


def _q_perm(w, axis):
    shape = w.shape
    w = w.reshape(shape[:axis] + (N_KV_HEADS, GQ, HEAD_DIM) + shape[axis + 1:])
    return jnp.swapaxes(w, axis, axis + 1).reshape(shape)


def _block_diag(w):
    eye = jnp.eye(LRU_BLOCKS, dtype=w.dtype)
    return jnp.einsum('nij,nm->nimj', w, eye).reshape(D_LRU, D_LRU)


def kernel(x_prompt, x_sample, cache_k, cache_v, state_lru_h, state_lru_conv, state_sc_conv, w_in, attn_sinks, lru_conv_w, lru_conv_b, lru_w_a, lru_b_a, lru_w_x, lru_b_x, lru_lambda, sc_conv_w, g_mix, w_out, ln1_g, ln1_b, w_grp, b_grp, w_route, b_route, w_gate, w_up, w_down, ln2_g, ln2_b):
    B, S, _ = x_prompt.shape
    DB, TS, _ = x_sample.shape
    assert S % T_PROMPT == 0 and (B * S) % T_PERM == 0 and (B * S) % BM == 0 and DB % BT_SAMPLE == 0
    assert T_PERM % T_DISPATCH == 0
    assert B * S <= (1 << (3 * DEST_BITS))
    assert TS <= SUBLANES and cache_k.shape[2] == WINDOW

    win = jnp.concatenate([_q_perm(w_in[:, :, :Q_W], 2), w_in[:, :, Q_W:]], axis=2).astype(_BF)
    wout = jnp.concatenate([_q_perm(w_out[:, :Q_W], 1), w_out[:, Q_W:]], axis=1).astype(_BF)
    gmix = jnp.concatenate([_q_perm(g_mix[:, :Q_W], 1), g_mix[:, Q_W:]], axis=1)[:, None, :]
    wlru = jnp.concatenate([jax.vmap(_block_diag)(lru_w_a), jax.vmap(_block_diag)(lru_w_x)], axis=2).astype(_BF)
    v256 = jnp.stack([lru_conv_b, lru_b_a, lru_b_x, lru_lambda], axis=1)
    ln1 = jnp.stack([ln1_g, ln1_b], axis=1)
    ln2 = jnp.stack([ln2_g, ln2_b], axis=1)
    pad = ROUTE_W - N_GROUPS - N_EXPERTS
    wr = jnp.pad(jnp.concatenate([w_grp, w_route], axis=2), ((0, 0), (0, 0), (0, pad))).astype(_BF)
    br = jnp.pad(jnp.concatenate([b_grp, b_route], axis=1), ((0, 0), (0, pad)))[:, None, :]
    wg = w_gate.astype(_BF)
    wu = w_up.astype(_BF)
    wd = w_down.reshape(DEPTH, N_GROUPS, EXPERTS_PER_GROUP * D_EXPERT, D_MODEL).astype(_BF)

    yp = x_prompt
    ys = jnp.swapaxes(x_sample, 0, 1)
    outs_p = [[] for _ in range(5)]
    outs_s = [[] for _ in range(5)]
    for l in range(DEPTH):
        mix_w = (attn_sinks[l], win[l], wlru[l], wout[l], lru_conv_w[l], v256[l], sc_conv_w[l], gmix[l], ln1[l])
        moe_w = (wr[l], br[l], wg[l], wu[l], wd[l], ln2[l])
        yp, kp, vp, hp, lbp, sbp = _mixer_prompt(yp, *mix_w)
        yp = _moe_sorted(yp.reshape(B * S, D_MODEL), *moe_w).reshape(B, S, D_MODEL)
        ys, ks, vs, hs, lbs, sbs = _mixer_sample(
            ys, cache_k[l].reshape(DB, WINDOW, KV_W), cache_v[l].reshape(DB, WINDOW, KV_W),
            state_lru_h[l], state_lru_conv[l].reshape(DB, -1), state_sc_conv[l].reshape(DB, -1), *mix_w)
        ys = _moe(ys.reshape(TS * DB, D_MODEL), *moe_w, TS * DB).reshape(TS, DB, D_MODEL)
        for lst, arr in zip(outs_p, (kp.reshape(B, WINDOW, N_KV_HEADS, HEAD_DIM),
                                     vp.reshape(B, WINDOW, N_KV_HEADS, HEAD_DIM),
                                     hp.reshape(B, D_LRU), lbp, sbp)):
            lst.append(arr)
        for lst, arr in zip(outs_s, (ks.reshape(DB, WINDOW, N_KV_HEADS, HEAD_DIM),
                                     vs.reshape(DB, WINDOW, N_KV_HEADS, HEAD_DIM),
                                     hs, lbs.reshape(DB, LRU_CONV - 1, D_LRU),
                                     sbs.reshape(DB, SC_CONV - 1, D_SC))):
            lst.append(arr)
    return (yp, jnp.swapaxes(ys, 0, 1),
            *[jnp.stack(o) for o in outs_p], *[jnp.stack(o) for o in outs_s])
```

```python
import functools
import math

import jax
import jax.numpy as jnp
from jax import lax
from jax.experimental import pallas as pl
from jax.experimental.pallas import tpu as pltpu

D_MODEL = 1024
DEPTH = 2
HEAD_DIM = 64
N_HEADS = 8
N_KV_HEADS = 2
GQ = N_HEADS // N_KV_HEADS
WINDOW = 128
ALIBI_MAX = 8.0
D_LRU = 256
LRU_BLOCKS = 4
LRU_BLOCK_W = D_LRU // LRU_BLOCKS
LRU_CONV = 4
LRU_C = 8.0
D_SC = 256
SC_CONV = 3
Q_W = N_HEADS * HEAD_DIM
KV_W = N_KV_HEADS * HEAD_DIM
IN_W = Q_W + 2 * KV_W + 2 * D_LRU + 3 * D_SC
N_GROUPS = 4
EXPERTS_PER_GROUP = 4
N_EXPERTS = N_GROUPS * EXPERTS_PER_GROUP
D_EXPERT = 256
ALPHA = (2 * DEPTH) ** 0.25
LN_EPS = 1e-5
RMS_EPS = 1e-6
NEG_INF = -1e30

O_K = Q_W
O_V = O_K + KV_W
O_XR = O_V + KV_W
O_GR = O_XR + D_LRU
O_U = O_GR + D_LRU
O_BG = O_U + D_SC
O_CG = O_BG + D_SC

LANES = 128
SUBLANES = 8
N_QTILES = Q_W // LANES
ROUTE_W = LANES
VMEM_LIMIT = 56 * 1024 * 1024

T_PROMPT = 512
T_DISPATCH = 512
T_PERM = 2048
TOK_ROWS = D_MODEL // LANES
DMA_UNROLL = 8
BM = 512
BT_SAMPLE = 32
KPAD = WINDOW + SUBLANES

_BF = jnp.bfloat16
_F32 = jnp.float32


def _slopes():
    return [2.0 ** (-ALIBI_MAX * h / N_HEADS) for h in range(1, N_HEADS + 1)]


def _dot(a, b):
    return jnp.dot(a, b, preferred_element_type=_F32)


def _dot_t(a, b):
    return lax.dot_general(a, b, (((1,), (1,)), ((), ())), preferred_element_type=_F32)


def _rms_rows(y, g):
    return y * lax.rsqrt(jnp.mean(y * y, axis=-1, keepdims=True) + RMS_EPS) * g


def _ln_rows(x, g, b):
    mu = jnp.mean(x, axis=-1, keepdims=True)
    xc = x - mu
    var = jnp.mean(xc * xc, axis=-1, keepdims=True)
    return xc * lax.rsqrt(var + LN_EPS) * g + b


def _lru_coeffs(xcb, wlru, b_a, b_x, lam):
    gates = _dot(xcb.astype(_BF), wlru)
    r = jax.nn.sigmoid(gates[:, :D_LRU] + b_a)
    i = jax.nn.sigmoid(gates[:, D_LRU:] + b_x)
    log_a = (-LRU_C * jax.nn.softplus(-lam)) * r
    a = jnp.exp(log_a)
    b = jnp.sqrt(jnp.tanh(-log_a) * (1.0 + a * a)) * (i * xcb)
    return a, b


def _sink_softmax(s, sink):
    m = jnp.maximum(jnp.max(s, axis=-1, keepdims=True), sink)
    p = jnp.exp(s - m)
    denom = jnp.sum(p, axis=-1, keepdims=True) + jnp.exp(sink - m)
    return p, 1.0 / denom


def _mixer_prompt_kernel(sinks_ref, x_ref, win_ref, wlru_ref, wout_ref, cw_ref, v256_ref, scw_ref,
                         gmix_ref, ln_ref,
                         y_ref, knew_ref, vnew_ref, hnew_ref, lrubuf_ref, scbuf_ref,
                         proj, qsc, klo, khi, vext, xrext, zext, hst, a_s, b_s, hl_s, pc_s, bias_s, mixed):
    T = T_PROMPT
    nblk = T // WINDOW
    s_idx = pl.program_id(1)
    lane = lax.broadcasted_iota(jnp.int32, (1, LANES), 1)
    lo_lane = lane < HEAD_DIM

    @pl.when(s_idx == 0)
    def _():
        klo[0:WINDOW, :] = jnp.zeros((WINDOW, LANES), _BF)
        khi[0:WINDOW, :] = jnp.zeros((WINDOW, LANES), _BF)
        vext[0:WINDOW, :] = jnp.zeros((WINDOW, LANES), _BF)
        xrext[0:SUBLANES, :] = jnp.zeros((SUBLANES, D_LRU), _F32)
        zext[0:SUBLANES, :] = jnp.zeros((SUBLANES, D_SC), _F32)
        hst[...] = jnp.zeros((1, D_LRU), _F32)
        qi = lax.broadcasted_iota(jnp.int32, (WINDOW, 2 * WINDOW), 0)
        kc = lax.broadcasted_iota(jnp.int32, (WINDOW, 2 * WINDOW), 1)
        dist = qi + WINDOW - kc
        valid = (dist >= 0) & (dist < WINDOW)
        valid_first = valid & (kc >= WINDOW)
        distf = dist.astype(_F32)
        for h, slope in enumerate(_slopes()):
            bias = -slope * distf
            bias_s[0, h] = jnp.where(valid, bias, NEG_INF)
            bias_s[1, h] = jnp.where(valid_first, bias, NEG_INF)

    proj[...] = _dot(x_ref[0].astype(_BF), win_ref[...])

    for n in range(nblk):
        rows = slice(n * WINDOW, (n + 1) * WINDOW)
        for j in range(N_QTILES):
            qsc[n, j * WINDOW:(j + 1) * WINDOW, :] = (
                proj[rows, j * LANES:(j + 1) * LANES] * (HEAD_DIM ** -0.5)).astype(_BF)
    k = proj[:, O_K:O_K + KV_W]
    v = proj[:, O_V:O_V + KV_W]
    klo[WINDOW:, :] = jnp.where(lo_lane, k, 0.0).astype(_BF)
    khi[WINDOW:, :] = jnp.where(lo_lane, 0.0, k).astype(_BF)
    vext[WINDOW:, :] = v.astype(_BF)
    knew_ref[0] = k[T - WINDOW:, :]
    vnew_ref[0] = v[T - WINDOW:, :]

    def attn_block(n, carry):
        r0 = pl.multiple_of(n * WINDOW, WINDOW)
        first = jnp.logical_and(s_idx == 0, n == 0).astype(jnp.int32)
        qs = qsc[n]
        kl = klo[pl.ds(r0, 2 * WINDOW), :]
        kh = khi[pl.ds(r0, 2 * WINDOW), :]
        vv = vext[pl.ds(r0, 2 * WINDOW), :]
        s_lo = _dot_t(qs, kl)
        s_hi = _dot_t(qs, kh)
        p_lo, p_hi, inv_lo, inv_hi = [], [], [], []
        for j in range(N_QTILES):
            rj = slice(j * WINDOW, (j + 1) * WINDOW)
            p, inv = _sink_softmax(s_lo[rj] + bias_s[first, j], sinks_ref[j])
            p_lo.append(p.astype(_BF))
            inv_lo.append(inv)
            p, inv = _sink_softmax(s_hi[rj] + bias_s[first, GQ + j], sinks_ref[GQ + j])
            p_hi.append(p.astype(_BF))
            inv_hi.append(inv)
        o_lo = _dot(jnp.concatenate(p_lo, axis=0), vv)
        o_hi = _dot(jnp.concatenate(p_hi, axis=0), vv)
        for j in range(N_QTILES):
            rj = slice(j * WINDOW, (j + 1) * WINDOW)
            mixed[pl.ds(r0, WINDOW), j * LANES:(j + 1) * LANES] = jnp.where(
                lo_lane, o_lo[rj] * inv_lo[j], o_hi[rj] * inv_hi[j])
        return carry

    lax.fori_loop(0, nblk, attn_block, 0)
    klo[0:WINDOW, :] = klo[T:T + WINDOW, :]
    khi[0:WINDOW, :] = khi[T:T + WINDOW, :]
    vext[0:WINDOW, :] = vext[T:T + WINDOW, :]

    xr = proj[:, O_XR:O_XR + D_LRU]
    xrext[SUBLANES:, :] = xr
    xc = xr * cw_ref[LRU_CONV - 1:LRU_CONV, :]
    for kk in range(1, LRU_CONV):
        xc = xc + xrext[pl.ds(SUBLANES - kk, T), :] * cw_ref[LRU_CONV - 1 - kk:LRU_CONV - kk, :]
    xcb = xc + v256_ref[0:1, :]
    a, b = _lru_coeffs(xcb, wlru_ref[...], v256_ref[1:2, :], v256_ref[2:3, :], v256_ref[3:4, :])
    NH = D_LRU // LANES
    C = T // SUBLANES
    PITCH = C + SUBLANES
    for hh in range(NH):
        for c in range(SUBLANES):
            a_s[hh, c * PITCH:c * PITCH + C, :] = a[c * C:(c + 1) * C, hh * LANES:(hh + 1) * LANES]
            b_s[hh, c * PITCH:c * PITCH + C, :] = b[c * C:(c + 1) * C, hh * LANES:(hh + 1) * LANES]
    lrubuf_ref[0] = xrext[pl.ds(T + SUBLANES - (LRU_CONV - 1), LRU_CONV - 1), :]
    xrext[0:SUBLANES, :] = xrext[T:T + SUBLANES, :]

    def scan_step(s, carry):
        out = []
        for hh in range(NH):
            h, p = carry[hh]
            at = a_s[hh, pl.ds(s, SUBLANES, stride=PITCH), :]
            bt = b_s[hh, pl.ds(s, SUBLANES, stride=PITCH), :]
            h = at * h + bt
            p = at * p
            hl_s[hh, pl.ds(s, SUBLANES, stride=PITCH), :] = h
            pc_s[hh, pl.ds(s, SUBLANES, stride=PITCH), :] = p
            out.append((h, p))
        return tuple(out)

    init = tuple((jnp.zeros((SUBLANES, LANES), _F32), jnp.ones((SUBLANES, LANES), _F32))
                 for _ in range(NH))
    ends = lax.fori_loop(0, C, scan_step, init)
    h_end = jnp.concatenate([e[0] for e in ends], axis=-1)
    p_end = jnp.concatenate([e[1] for e in ends], axis=-1)
    h_in = hst[...]
    gmix_lru = gmix_ref[:, Q_W:Q_W + D_LRU]
    for c in range(SUBLANES):
        rows = slice(c * C, (c + 1) * C)
        prow = slice(c * PITCH, c * PITCH + C)
        hl = jnp.concatenate([hl_s[hh, prow, :] for hh in range(NH)], axis=-1)
        pc = jnp.concatenate([pc_s[hh, prow, :] for hh in range(NH)], axis=-1)
        hs = hl + pc * h_in
        y_lru = hs * jax.nn.gelu(proj[rows, O_GR:O_GR + D_LRU])
        mixed[rows, Q_W:Q_W + D_LRU] = _rms_rows(y_lru, gmix_lru)
        h_in = p_end[c:c + 1, :] * h_in + h_end[c:c + 1, :]
    hst[...] = h_in
    hnew_ref[0] = h_in

    z = proj[:, O_CG:O_CG + D_SC] * proj[:, O_U:O_U + D_SC]
    zext[SUBLANES:, :] = z
    yc = z * scw_ref[SC_CONV - 1:SC_CONV, :]
    for kk in range(1, SC_CONV):
        yc = yc + zext[pl.ds(SUBLANES - kk, T), :] * scw_ref[SC_CONV - 1 - kk:SC_CONV - kk, :]
    y_sc = proj[:, O_BG:O_BG + D_SC] * yc
    mixed[:, Q_W + D_LRU:] = _rms_rows(y_sc, gmix_ref[:, Q_W + D_LRU:])
    scbuf_ref[0] = zext[pl.ds(T + SUBLANES - (SC_CONV - 1), SC_CONV - 1), :]
    zext[0:SUBLANES, :] = zext[T:T + SUBLANES, :]

    mixed[:, 0:Q_W] = _rms_rows(mixed[:, 0:Q_W], gmix_ref[:, 0:Q_W])
    m = _dot(mixed[...].astype(_BF), wout_ref[...])
    y_ref[0] = _ln_rows(ALPHA * x_ref[0] + m, ln_ref[0:1, :], ln_ref[1:2, :])


def _const_spec(shape):
    nd = len(shape)
    return pl.BlockSpec(shape, lambda *_: (0,) * nd, pipeline_mode=pl.Buffered(1))


def _mixer_prompt(x, sinks, win, wlru, wout, cw, v256, scw, gmix, ln):
    B, S, _ = x.shape
    T = T_PROMPT
    grid = (B, S // T)
    seq_out = lambda shape: pl.BlockSpec((1,) + shape, lambda b, s: (b, 0, 0))
    return pl.pallas_call(
        _mixer_prompt_kernel,
        grid=grid,
        in_specs=[
            pl.BlockSpec(memory_space=pltpu.SMEM),
            pl.BlockSpec((1, T, D_MODEL), lambda b, s: (b, s, 0)),
            _const_spec(win.shape), _const_spec(wlru.shape), _const_spec(wout.shape),
            _const_spec(cw.shape), _const_spec(v256.shape), _const_spec(scw.shape),
            _const_spec(gmix.shape), _const_spec(ln.shape),
        ],
        out_specs=[
            pl.BlockSpec((1, T, D_MODEL), lambda b, s: (b, s, 0)),
            seq_out((WINDOW, KV_W)), seq_out((WINDOW, KV_W)), seq_out((1, D_LRU)),
            seq_out((LRU_CONV - 1, D_LRU)), seq_out((SC_CONV - 1, D_SC)),
        ],
        out_shape=[
            jax.ShapeDtypeStruct((B, S, D_MODEL), _F32),
            jax.ShapeDtypeStruct((B, WINDOW, KV_W), _F32),
            jax.ShapeDtypeStruct((B, WINDOW, KV_W), _F32),
            jax.ShapeDtypeStruct((B, 1, D_LRU), _F32),
            jax.ShapeDtypeStruct((B, LRU_CONV - 1, D_LRU), _F32),
            jax.ShapeDtypeStruct((B, SC_CONV - 1, D_SC), _F32),
        ],
        scratch_shapes=[
            pltpu.VMEM((T, IN_W), _F32),
            pltpu.VMEM((T // WINDOW, N_QTILES * WINDOW, LANES), _BF),
            pltpu.VMEM((T + WINDOW, LANES), _BF),
            pltpu.VMEM((T + WINDOW, LANES), _BF),
            pltpu.VMEM((T + WINDOW, LANES), _BF),
            pltpu.VMEM((T + SUBLANES, D_LRU), _F32),
            pltpu.VMEM((T + SUBLANES, D_SC), _F32),
            pltpu.VMEM((1, D_LRU), _F32),
            pltpu.VMEM((D_LRU // LANES, T + SUBLANES * SUBLANES, LANES), _F32),
            pltpu.VMEM((D_LRU // LANES, T + SUBLANES * SUBLANES, LANES), _F32),
            pltpu.VMEM((D_LRU // LANES, T + SUBLANES * SUBLANES, LANES), _F32),
            pltpu.VMEM((D_LRU // LANES, T + SUBLANES * SUBLANES, LANES), _F32),
            pltpu.VMEM((2, N_HEADS, WINDOW, 2 * WINDOW), _F32),
            pltpu.VMEM((T, D_MODEL), _F32),
        ],
        compiler_params=pltpu.CompilerParams(
            dimension_semantics=("arbitrary", "arbitrary"), vmem_limit_bytes=VMEM_LIMIT),
        name="mixer_prompt",
    )(sinks, x, win, wlru, wout, cw, v256, scw, gmix, ln)


def _mixer_sample_kernel(sinks_ref, x_ref, ck_ref, cv_ref, h0_ref, lbuf_ref, sbuf_ref,
                         win_ref, wlru_ref, wout_ref, cw_ref, v256_ref, scw_ref, gmix_ref, ln_ref,
                         y_ref, knew_ref, vnew_ref, hnew_ref, lrubuf_ref, scbuf_ref,
                         kx2, vx2, qs2, os2):
    TS = x_ref.shape[0]
    BT = BT_SAMPLE
    NQ = TS * N_QTILES
    lane = lax.broadcasted_iota(jnp.int32, (1, LANES), 1)
    lo_lane = lane < HEAD_DIM

    x = x_ref[...].reshape(TS * BT, D_MODEL)
    proj = _dot(x.astype(_BF), win_ref[...])

    for bb in range(BT):
        kx2[bb * KPAD:bb * KPAD + WINDOW, :] = ck_ref[bb]
        vx2[bb * KPAD:bb * KPAD + WINDOW, :] = cv_ref[bb]
    for t in range(SUBLANES):
        if t < TS:
            rows = slice(t * BT, (t + 1) * BT)
            kx2[pl.ds(WINDOW + t, BT, stride=KPAD), :] = proj[rows, O_K:O_K + KV_W]
            vx2[pl.ds(WINDOW + t, BT, stride=KPAD), :] = proj[rows, O_V:O_V + KV_W]
            for j in range(N_QTILES):
                qs2[pl.ds(t * N_QTILES + j, BT, stride=NQ), :] = (
                    proj[rows, j * LANES:(j + 1) * LANES] * (HEAD_DIM ** -0.5))
        else:
            kx2[pl.ds(WINDOW + t, BT, stride=KPAD), :] = jnp.zeros((BT, LANES), _F32)
            vx2[pl.ds(WINDOW + t, BT, stride=KPAD), :] = jnp.zeros((BT, LANES), _F32)
    for bb in range(BT):
        knew_ref[bb] = kx2[bb * KPAD + TS:bb * KPAD + TS + WINDOW, :]
        vnew_ref[bb] = vx2[bb * KPAD + TS:bb * KPAD + TS + WINDOW, :]

    kall = kx2[...].reshape(BT, KPAD, LANES)
    qs = qs2[...].reshape(BT, NQ, LANES).astype(_BF)
    k_lo = jnp.where(lo_lane, kall, 0.0).astype(_BF)
    k_hi = jnp.where(lo_lane, 0.0, kall).astype(_BF)
    vv = vx2[...].reshape(BT, KPAD, LANES).astype(_BF)
    s_lo = jnp.einsum('bqd,bkd->bqk', qs, k_lo, preferred_element_type=_F32)
    s_hi = jnp.einsum('bqd,bkd->bqk', qs, k_hi, preferred_element_type=_F32)
    ri = lax.broadcasted_iota(jnp.int32, (NQ, KPAD), 0)
    ci = lax.broadcasted_iota(jnp.int32, (NQ, KPAD), 1)
    tq = lax.shift_right_logical(ri, int(math.log2(N_QTILES)))
    jq = ri & (N_QTILES - 1)
    dist = WINDOW + tq - ci
    valid = (dist >= 0) & (dist < WINDOW)
    distf = dist.astype(_F32)
    rj = lax.broadcasted_iota(jnp.int32, (NQ, 1), 0) & (N_QTILES - 1)
    slope_lo = jnp.zeros((NQ, KPAD), _F32)
    slope_hi = jnp.zeros((NQ, KPAD), _F32)
    sink_lo = jnp.zeros((NQ, 1), _F32)
    sink_hi = jnp.zeros((NQ, 1), _F32)
    slopes = _slopes()
    for j in range(N_QTILES):
        slope_lo = jnp.where(jq == j, slopes[j], slope_lo)
        slope_hi = jnp.where(jq == j, slopes[GQ + j], slope_hi)
        sink_lo = jnp.where(rj == j, sinks_ref[j], sink_lo)
        sink_hi = jnp.where(rj == j, sinks_ref[GQ + j], sink_hi)
    bias_lo = jnp.where(valid, -slope_lo * distf, NEG_INF)
    bias_hi = jnp.where(valid, -slope_hi * distf, NEG_INF)
    p_lo, inv_lo = _sink_softmax(s_lo + bias_lo[None], sink_lo[None])
    p_hi, inv_hi = _sink_softmax(s_hi + bias_hi[None], sink_hi[None])
    o_lo = jnp.einsum('bqk,bkd->bqd', p_lo.astype(_BF), vv, preferred_element_type=_F32)
    o_hi = jnp.einsum('bqk,bkd->bqd', p_hi.astype(_BF), vv, preferred_element_type=_F32)
    o = jnp.where(lo_lane[None], o_lo * inv_lo, o_hi * inv_hi)
    os2[...] = o.reshape(BT * NQ, LANES)

    lbuf = lbuf_ref[...]
    sbuf = sbuf_ref[...]
    xr_hist = [lbuf[:, i * D_LRU:(i + 1) * D_LRU] for i in range(LRU_CONV - 1)]
    z_hist = [sbuf[:, i * D_SC:(i + 1) * D_SC] for i in range(SC_CONV - 1)]
    h = h0_ref[...]
    gmix = gmix_ref[...]
    mixed_rows = []
    for t in range(TS):
        rows = slice(t * BT, (t + 1) * BT)
        o_att = jnp.concatenate(
            [os2[pl.ds(t * N_QTILES + j, BT, stride=NQ), :] for j in range(N_QTILES)], axis=-1)
        xr_hist.append(proj[rows, O_XR:O_XR + D_LRU])
        xc = sum(xr_hist[t + kk] * cw_ref[kk:kk + 1, :] for kk in range(LRU_CONV))
        xcb = xc + v256_ref[0:1, :]
        a, b = _lru_coeffs(xcb, wlru_ref[...], v256_ref[1:2, :], v256_ref[2:3, :], v256_ref[3:4, :])
        h = a * h + b
        y_lru = h * jax.nn.gelu(proj[rows, O_GR:O_GR + D_LRU])
        z_hist.append(proj[rows, O_CG:O_CG + D_SC] * proj[rows, O_U:O_U + D_SC])
        yc = sum(z_hist[t + kk] * scw_ref[kk:kk + 1, :] for kk in range(SC_CONV))
        y_sc = proj[rows, O_BG:O_BG + D_SC] * yc
        mixed_rows.append(jnp.concatenate(
            [_rms_rows(o_att, gmix[:, 0:Q_W]),
             _rms_rows(y_lru, gmix[:, Q_W:Q_W + D_LRU]),
             _rms_rows(y_sc, gmix[:, Q_W + D_LRU:])], axis=-1))
    hnew_ref[...] = h
    lrubuf_ref[...] = jnp.concatenate(xr_hist[TS:], axis=-1)
    scbuf_ref[...] = jnp.concatenate(z_hist[TS:], axis=-1)
    mixed = jnp.concatenate(mixed_rows, axis=0)
    m = _dot(mixed.astype(_BF), wout_ref[...])
    y = _ln_rows(ALPHA * x + m, ln_ref[0:1, :], ln_ref[1:2, :])
    y_ref[...] = y.reshape(TS, BT, D_MODEL)


def _mixer_sample(x, ck, cv, h0, lbuf, sbuf, sinks, win, wlru, wout, cw, v256, scw, gmix, ln):
    TS, B, _ = x.shape
    BT = BT_SAMPLE
    rows2 = lambda w: pl.BlockSpec((BT, w), lambda i: (i, 0))
    cache = pl.BlockSpec((BT, WINDOW, KV_W), lambda i: (i, 0, 0))
    xspec = pl.BlockSpec((TS, BT, D_MODEL), lambda i: (0, i, 0))
    return pl.pallas_call(
        _mixer_sample_kernel,
        grid=(B // BT,),
        in_specs=[
            pl.BlockSpec(memory_space=pltpu.SMEM),
            xspec, cache, cache, rows2(D_LRU), rows2((LRU_CONV - 1) * D_LRU), rows2((SC_CONV - 1) * D_SC),
            _const_spec(win.shape), _const_spec(wlru.shape), _const_spec(wout.shape),
            _const_spec(cw.shape), _const_spec(v256.shape), _const_spec(scw.shape),
            _const_spec(gmix.shape), _const_spec(ln.shape),
        ],
        out_specs=[xspec, cache, cache, rows2(D_LRU), rows2((LRU_CONV - 1) * D_LRU),
                   rows2((SC_CONV - 1) * D_SC)],
        out_shape=[
            jax.ShapeDtypeStruct((TS, B, D_MODEL), _F32),
            jax.ShapeDtypeStruct((B, WINDOW, KV_W), _F32),
            jax.ShapeDtypeStruct((B, WINDOW, KV_W), _F32),
            jax.ShapeDtypeStruct((B, D_LRU), _F32),
            jax.ShapeDtypeStruct((B, (LRU_CONV - 1) * D_LRU), _F32),
            jax.ShapeDtypeStruct((B, (SC_CONV - 1) * D_SC), _F32),
        ],
        scratch_shapes=[
            pltpu.VMEM((BT * KPAD, LANES), _F32),
            pltpu.VMEM((BT * KPAD, LANES), _F32),
            pltpu.VMEM((BT * TS * N_QTILES, LANES), _F32),
            pltpu.VMEM((BT * TS * N_QTILES, LANES), _F32),
        ],
        compiler_params=pltpu.CompilerParams(
            dimension_semantics=("arbitrary",), vmem_limit_bytes=VMEM_LIMIT),
        name="mixer_sample",
    )(sinks, x, ck, cv, h0, lbuf, sbuf, win, wlru, wout, cw, v256, scw, gmix, ln)


def _route(logits):
    lane = lax.broadcasted_iota(jnp.int32, logits.shape, 1)
    big = jnp.int32(ROUTE_W)
    is_grp = lane < N_GROUPS
    glog = jnp.where(is_grp, logits, -jnp.inf)
    gmax = jnp.max(glog, axis=-1, keepdims=True)
    gsel = jnp.min(jnp.where(glog == gmax, lane, big), axis=-1, keepdims=True)
    gw = 1.0 / jnp.sum(jnp.exp(glog - gmax), axis=-1, keepdims=True)
    e_lo = N_GROUPS + gsel * EXPERTS_PER_GROUP
    in_grp = (lane >= e_lo) & (lane < e_lo + EXPERTS_PER_GROUP)
    elog = jnp.where(in_grp, logits, -jnp.inf)
    v1 = jnp.max(elog, axis=-1, keepdims=True)
    i1 = jnp.min(jnp.where(elog == v1, lane, big), axis=-1, keepdims=True)
    elog2 = jnp.where(lane == i1, -jnp.inf, elog)
    v2 = jnp.max(elog2, axis=-1, keepdims=True)
    i2 = jnp.min(jnp.where(elog2 == v2, lane, big), axis=-1, keepdims=True)
    e2 = jnp.exp(v2 - v1)
    w1 = gw / (1.0 + e2)
    w2 = gw * e2 / (1.0 + e2)
    return jnp.where(lane == i1, w1, jnp.where(lane == i2, w2, 0.0))


def _moe_kernel(x_ref, wr_ref, br_ref, wg_ref, wu_ref, wd_ref, ln_ref, y_ref, gates_s, hsc, acc):
    x = x_ref[...]
    xb = x.astype(_BF)
    gates_s[...] = _route(_dot(xb, wr_ref[...]) + br_ref[...])

    def group(g, carry):
        for i in range(EXPERTS_PER_GROUP):
            e = g * EXPERTS_PER_GROUP + i
            hg = _dot(xb, wg_ref[e])
            hu = _dot(xb, wu_ref[e])
            lane = lax.broadcasted_iota(jnp.int32, (1, ROUTE_W), 1)
            gate = jnp.sum(jnp.where(lane == N_GROUPS + e, gates_s[...], 0.0), axis=-1, keepdims=True)
            hsc[:, i * D_EXPERT:(i + 1) * D_EXPERT] = (jax.nn.silu(hg) * hu * gate).astype(_BF)
        contrib = _dot(hsc[...], wd_ref[g])

        @pl.when(g == 0)
        def _():
            acc[...] = contrib

        @pl.when(g > 0)
        def _():
            acc[...] += contrib
        return carry

    lax.fori_loop(0, N_GROUPS, group, 0)
    y_ref[...] = _ln_rows(ALPHA * x + acc[...], ln_ref[0:1, :], ln_ref[1:2, :])


def _moe(x, wr, br, wg, wu, wd, ln, tile):
    N = x.shape[0]
    tok = pl.BlockSpec((tile, D_MODEL), lambda i: (i, 0))
    return pl.pallas_call(
        _moe_kernel,
        grid=(N // tile,),
        in_specs=[tok, _const_spec(wr.shape), _const_spec(br.shape), _const_spec(wg.shape),
                  _const_spec(wu.shape), _const_spec(wd.shape), _const_spec(ln.shape)],
        out_specs=tok,
        out_shape=jax.ShapeDtypeStruct((N, D_MODEL), _F32),
        scratch_shapes=[
            pltpu.VMEM((tile, ROUTE_W), _F32),
            pltpu.VMEM((tile, EXPERTS_PER_GROUP * D_EXPERT), _BF),
            pltpu.VMEM((tile, D_MODEL), _F32),
        ],
        compiler_params=pltpu.CompilerParams(
            dimension_semantics=("arbitrary",), vmem_limit_bytes=VMEM_LIMIT),
        name="moe",
    )(x, wr, br, wg, wu, wd, ln)


CODE_ROWS = SUBLANES


def _route_kernel(x_ref, wr_ref, br_ref, code_ref, cnt_ref, utri, cnt_s):
    T = T_DISPATCH
    i = pl.program_id(0)

    @pl.when(i == 0)
    def _():
        cnt_s[...] = jnp.zeros((SUBLANES, T), _F32)
        r = lax.broadcasted_iota(jnp.int32, (T, T), 0)
        c = lax.broadcasted_iota(jnp.int32, (T, T), 1)
        utri[...] = jnp.where(r < c, 1.0, 0.0).astype(_BF)

    logits = _dot(x_ref[...].astype(_BF), wr_ref[...]) + br_ref[...]
    lt = logits.T[0:SUBLANES]
    g = [lt[k:k + 1] for k in range(N_GROUPS)]
    gmax = functools.reduce(jnp.maximum, g)
    gsel = jnp.full((1, T), N_GROUPS - 1, jnp.int32)
    for k in range(N_GROUPS - 2, -1, -1):
        gsel = jnp.where(g[k] == gmax, k, gsel)
    row = lax.broadcasted_iota(jnp.int32, (SUBLANES, T), 0)
    onehot = row == gsel
    onehot_f = jnp.where(onehot, 1.0, 0.0)
    rank = _dot(onehot_f.astype(_BF), utri[...])
    cnt = cnt_s[...]
    loc = jnp.sum(jnp.where(onehot, cnt + rank, 0.0), axis=0, keepdims=True).astype(jnp.int32)
    cnt_new = cnt + jnp.sum(onehot_f, axis=1, keepdims=True)
    cnt_s[...] = cnt_new
    cnt_ref[...] = cnt_new[:, 0:LANES].astype(jnp.int32)
    code_ref[0] = jnp.where(row == 0, loc, jnp.where(row == 1, gsel, 0))


def _route_tokens(x, wr, br):
    N = x.shape[0]
    T = T_DISPATCH
    return pl.pallas_call(
        _route_kernel,
        grid=(N // T,),
        in_specs=[pl.BlockSpec((T, D_MODEL), lambda i: (i, 0)),
                  _const_spec(wr.shape), _const_spec(br.shape)],
        out_specs=[pl.BlockSpec((1, CODE_ROWS, T), lambda i: (i, 0, 0)),
                   pl.BlockSpec((SUBLANES, LANES), lambda i: (0, 0))],
        out_shape=[jax.ShapeDtypeStruct((N // T, CODE_ROWS, T), jnp.int32),
                   jax.ShapeDtypeStruct((SUBLANES, LANES), jnp.int32)],
        scratch_shapes=[
            pltpu.VMEM((T, T), _BF),
            pltpu.VMEM((SUBLANES, T), _F32),
        ],
        compiler_params=pltpu.CompilerParams(
            dimension_semantics=("arbitrary",), vmem_limit_bytes=VMEM_LIMIT),
        name="moe_route",
    )(x, wr, br)


def _to_token_rows(x, rows_ref):
    R = x.shape[0]
    for c in range(TOK_ROWS):
        rows_ref[pl.ds(c, R, stride=TOK_ROWS), :] = x[:, c * LANES:(c + 1) * LANES]


def _from_token_rows(rows_ref, R):
    return jnp.concatenate(
        [rows_ref[pl.ds(c, R, stride=TOK_ROWS), :] for c in range(TOK_ROWS)], axis=-1)


def _token_copy(src_ref, s, dst_ref, d, sem):
    return pltpu.make_async_copy(
        src_ref.at[pl.ds(pl.multiple_of(s * TOK_ROWS, TOK_ROWS), TOK_ROWS)],
        dst_ref.at[pl.ds(pl.multiple_of(d * TOK_ROWS, TOK_ROWS), TOK_ROWS)], sem)


def _load_dest(dest_ref, dest_sm, sem_idx):
    copies = [pltpu.make_async_copy(dest_ref.at[j, 0], dest_sm.at[pl.ds(j * T_DISPATCH, T_DISPATCH)], sem_idx)
              for j in range(T_PERM // T_DISPATCH)]
    for cp in copies:
        cp.start()
    for cp in copies:
        cp.wait()


def _scatter_kernel(off_ref, code_ref, x_ref, xs_ref, dest_ref, xrow, dest_sm, sem_idx, sem_rows):
    code = code_ref[...]
    loc = code[:, 0:1]
    grp = code[:, 1:2]
    off = jnp.zeros(loc.shape, jnp.int32)
    for g in range(N_GROUPS):
        off = jnp.where(grp == g, off_ref[g], off)
    dest_ref[...] = off + loc
    _to_token_rows(x_ref[...], xrow)
    _load_dest(dest_ref, dest_sm, sem_idx)

    def start(i, c):
        for k in range(DMA_UNROLL):
            t = i * DMA_UNROLL + k
            _token_copy(xrow, t, xs_ref, dest_sm[t], sem_rows).start(priority=k % 2)
        return c

    lax.fori_loop(0, T_PERM // DMA_UNROLL, start, 0)

    def wait(t, c):
        _token_copy(xrow, 0, xs_ref, 0, sem_rows).wait()
        return c

    lax.fori_loop(0, T_PERM, wait, 0, unroll=DMA_UNROLL)


def _scatter(off, code, x):
    N = x.shape[0]
    T = T_PERM
    nt = T // T_DISPATCH
    return pl.pallas_call(
        _scatter_kernel,
        grid=(N // T,),
        in_specs=[pl.BlockSpec(memory_space=pltpu.SMEM),
                  pl.BlockSpec((nt, CODE_ROWS, T_DISPATCH), lambda i: (i, 0, 0)),
                  pl.BlockSpec((T, D_MODEL), lambda i: (i, 0))],
        out_specs=[pl.BlockSpec(memory_space=pl.ANY),
                   pl.BlockSpec((nt, 1, T_DISPATCH), lambda i: (i, 0, 0))],
        out_shape=[jax.ShapeDtypeStruct((N * TOK_ROWS, LANES), _F32),
                   jax.ShapeDtypeStruct((N // T_DISPATCH, 1, T_DISPATCH), jnp.int32)],
        scratch_shapes=[pltpu.VMEM((T * TOK_ROWS, LANES), _F32),
                        pltpu.SMEM((T,), jnp.int32), pltpu.SemaphoreType.DMA, pltpu.SemaphoreType.DMA],
        compiler_params=pltpu.CompilerParams(
            dimension_semantics=("arbitrary",), vmem_limit_bytes=VMEM_LIMIT),
        name="moe_scatter",
    )(off, code, x)


def _route_in_group(logits, g):
    lane = lax.broadcasted_iota(jnp.int32, logits.shape, 1)
    big = jnp.int32(ROUTE_W)
    glog = jnp.where(lane < N_GROUPS, logits, -jnp.inf)
    gmax = jnp.max(glog, axis=-1, keepdims=True)
    lg = jnp.sum(jnp.where(lane == g, logits, 0.0), axis=-1, keepdims=True)
    gw = jnp.exp(lg - gmax) / jnp.sum(jnp.exp(glog - gmax), axis=-1, keepdims=True)
    e_lo = N_GROUPS + g * EXPERTS_PER_GROUP
    in_grp = (lane >= e_lo) & (lane < e_lo + EXPERTS_PER_GROUP)
    elog = jnp.where(in_grp, logits, -jnp.inf)
    v1 = jnp.max(elog, axis=-1, keepdims=True)
    i1 = jnp.min(jnp.where(elog == v1, lane, big), axis=-1, keepdims=True)
    elog2 = jnp.where(lane == i1, -jnp.inf, elog)
    v2 = jnp.max(elog2, axis=-1, keepdims=True)
    i2 = jnp.min(jnp.where(elog2 == v2, lane, big), axis=-1, keepdims=True)
    e2 = jnp.exp(v2 - v1)
    w1 = gw / (1.0 + e2)
    w2 = gw * e2 / (1.0 + e2)
    return jnp.where(lane == i1, w1, jnp.where(lane == i2, w2, 0.0))


def _moe_grouped_kernel(bidx_ref, bgrp_ref, blo_ref, bhi_ref, bfirst_ref, nstep_ref,
                        x_ref, wr_ref, br_ref, wg_ref, wu_ref, wd_ref, ln_ref, y_ref, hsc):
    i = pl.program_id(0)

    @pl.when(i < nstep_ref[0])
    def _():
        g = bgrp_ref[i]
        x = _from_token_rows(x_ref, BM)
        xb = x.astype(_BF)
        gates = _route_in_group(_dot(xb, wr_ref[...]) + br_ref[...], g)
        lane = lax.broadcasted_iota(jnp.int32, (1, ROUTE_W), 1)
        for e in range(EXPERTS_PER_GROUP):
            hg = _dot(xb, wg_ref[e])
            hu = _dot(xb, wu_ref[e])
            gate = jnp.sum(jnp.where(lane == N_GROUPS + g * EXPERTS_PER_GROUP + e, gates, 0.0),
                           axis=-1, keepdims=True)
            hsc[:, e * D_EXPERT:(e + 1) * D_EXPERT] = (jax.nn.silu(hg) * hu * gate).astype(_BF)
        y = _dot(hsc[...], wd_ref[0])
        y_new = _ln_rows(ALPHA * x + y, ln_ref[0:1, :], ln_ref[1:2, :])

        @pl.when(bfirst_ref[i] == 1)
        def _():
            _to_token_rows(y_new, y_ref)

        @pl.when(bfirst_ref[i] == 0)
        def _():
            row = lax.broadcasted_iota(jnp.int32, (BM, 1), 0)
            mine = (row >= blo_ref[i]) & (row < bhi_ref[i])
            _to_token_rows(jnp.where(mine, y_new, _from_token_rows(y_ref, BM)), y_ref)


def _moe_grouped(tables, xs, wr, br, wg, wu, wd, ln):
    nsteps = tables[0].shape[0]
    tok = pl.BlockSpec((BM * TOK_ROWS, LANES), lambda i, bidx, *_: (bidx[i], 0))
    const = lambda shape: pl.BlockSpec(shape, lambda i, *_: (0,) * len(shape), pipeline_mode=pl.Buffered(1))
    grp3 = lambda shape: pl.BlockSpec(shape, lambda i, bidx, bgrp, *_: (bgrp[i], 0, 0))
    return pl.pallas_call(
        _moe_grouped_kernel,
        grid_spec=pltpu.PrefetchScalarGridSpec(
            num_scalar_prefetch=len(tables),
            grid=(nsteps,),
            in_specs=[tok, const(wr.shape), const(br.shape),
                      grp3((EXPERTS_PER_GROUP, D_MODEL, D_EXPERT)),
                      grp3((EXPERTS_PER_GROUP, D_MODEL, D_EXPERT)),
                      grp3((1, EXPERTS_PER_GROUP * D_EXPERT, D_MODEL)),
                      const(ln.shape)],
            out_specs=tok,
            scratch_shapes=[pltpu.VMEM((BM, EXPERTS_PER_GROUP * D_EXPERT), _BF)],
        ),
        out_shape=jax.ShapeDtypeStruct(xs.shape, _F32),
        compiler_params=pltpu.CompilerParams(
            dimension_semantics=("arbitrary",), vmem_limit_bytes=VMEM_LIMIT),
        name="moe_grouped",
    )(*tables, xs, wr, br, wg, wu, wd, ln)


def _combine_kernel(dest_ref, ys_ref, out_ref, yrow, dest_sm, sem_idx, sem_rows):
    _load_dest(dest_ref, dest_sm, sem_idx)

    def start(i, c):
        for k in range(DMA_UNROLL):
            t = i * DMA_UNROLL + k
            _token_copy(ys_ref, dest_sm[t], yrow, t, sem_rows).start(priority=k % 2)
        return c

    lax.fori_loop(0, T_PERM // DMA_UNROLL, start, 0)

    def wait(t, c):
        _token_copy(ys_ref, 0, yrow, 0, sem_rows).wait()
        return c

    lax.fori_loop(0, T_PERM, wait, 0, unroll=DMA_UNROLL)
    out_ref[...] = _from_token_rows(yrow, T_PERM)


def _combine(dest, ys):
    N = dest.shape[0] * T_DISPATCH
    T = T_PERM
    nt = T // T_DISPATCH
    return pl.pallas_call(
        _combine_kernel,
        grid=(N // T,),
        in_specs=[pl.BlockSpec((nt, 1, T_DISPATCH), lambda i: (i, 0, 0)),
                  pl.BlockSpec(memory_space=pl.ANY)],
        out_specs=pl.BlockSpec((T, D_MODEL), lambda i: (i, 0)),
        out_shape=jax.ShapeDtypeStruct((N, D_MODEL), _F32),
        scratch_shapes=[pltpu.VMEM((T * TOK_ROWS, LANES), _F32),
                        pltpu.SMEM((T,), jnp.int32), pltpu.SemaphoreType.DMA, pltpu.SemaphoreType.DMA],
        compiler_params=pltpu.CompilerParams(
            dimension_semantics=("arbitrary",), vmem_limit_bytes=VMEM_LIMIT),
        name="moe_combine",
    )(dest, ys)


def _step_tables(cnt, nsteps):
    n = cnt[:N_GROUPS, 0]
    off = jnp.cumsum(n) - n
    b_lo = off // BM
    nb = jnp.where(n > 0, (off + n - 1) // BM - b_lo + 1, 0)
    ends = jnp.cumsum(nb)
    total = ends[-1]
    step = jnp.minimum(jnp.arange(nsteps, dtype=jnp.int32), total - 1)
    grp = jnp.sum(step[:, None] >= ends[None, :], axis=1).astype(jnp.int32)
    bidx = b_lo[grp] + step - (ends - nb)[grp]
    lo = jnp.clip(off[grp] - bidx * BM, 0, BM)
    hi = jnp.clip(off[grp] + n[grp] - bidx * BM, 0, BM)
    first = jnp.concatenate([jnp.ones((1,), jnp.int32), (bidx[1:] != bidx[:-1]).astype(jnp.int32)])
    tables = tuple(a.astype(jnp.int32) for a in (bidx, grp, lo, hi, first, total[None]))
    return off.astype(jnp.int32), tables


def _moe_sorted(x, wr, br, wg, wu, wd, ln):
    N = x.shape[0]
    code, cnt = _route_tokens(x, wr, br)
    off, tables = _step_tables(cnt, N // BM + N_GROUPS - 1)
    xs, dest = _scatter(off, code, x)
    ys = _moe_grouped(tables, xs, wr, br, wg, wu, wd, ln)
    return _combine(dest, ys)


def _q_perm(w, axis):
    shape = w.shape
    w = w.reshape(shape[:axis] + (N_KV_HEADS, GQ, HEAD_DIM) + shape[axis + 1:])
    return jnp.swapaxes(w, axis, axis + 1).reshape(shape)


def _block_diag(w):
    eye = jnp.eye(LRU_BLOCKS, dtype=w.dtype)
    return jnp.einsum('nij,nm->nimj', w, eye).reshape(D_LRU, D_LRU)


def kernel(x_prompt, x_sample, cache_k, cache_v, state_lru_h, state_lru_conv, state_sc_conv, w_in, attn_sinks, lru_conv_w, lru_conv_b, lru_w_a, lru_b_a, lru_w_x, lru_b_x, lru_lambda, sc_conv_w, g_mix, w_out, ln1_g, ln1_b, w_grp, b_grp, w_route, b_route, w_gate, w_up, w_down, ln2_g, ln2_b):
    B, S, _ = x_prompt.shape
    DB, TS, _ = x_sample.shape
    assert S % T_PROMPT == 0 and (B * S) % T_PERM == 0 and (B * S) % BM == 0 and DB % BT_SAMPLE == 0
    assert T_PERM % T_DISPATCH == 0
    assert B * S < (1 << 24)
    assert TS <= SUBLANES and cache_k.shape[2] == WINDOW

    win = jnp.concatenate([_q_perm(w_in[:, :, :Q_W], 2), w_in[:, :, Q_W:]], axis=2).astype(_BF)
    wout = jnp.concatenate([_q_perm(w_out[:, :Q_W], 1), w_out[:, Q_W:]], axis=1).astype(_BF)
    gmix = jnp.concatenate([_q_perm(g_mix[:, :Q_W], 1), g_mix[:, Q_W:]], axis=1)[:, None, :]
    wlru = jnp.concatenate([jax.vmap(_block_diag)(lru_w_a), jax.vmap(_block_diag)(lru_w_x)], axis=2).astype(_BF)
    v256 = jnp.stack([lru_conv_b, lru_b_a, lru_b_x, lru_lambda], axis=1)
    ln1 = jnp.stack([ln1_g, ln1_b], axis=1)
    ln2 = jnp.stack([ln2_g, ln2_b], axis=1)
    pad = ROUTE_W - N_GROUPS - N_EXPERTS
    wr = jnp.pad(jnp.concatenate([w_grp, w_route], axis=2), ((0, 0), (0, 0), (0, pad))).astype(_BF)
    br = jnp.pad(jnp.concatenate([b_grp, b_route], axis=1), ((0, 0), (0, pad)))[:, None, :]
    wg = w_gate.astype(_BF)
    wu = w_up.astype(_BF)
    wd = w_down.reshape(DEPTH, N_GROUPS, EXPERTS_PER_GROUP * D_EXPERT, D_MODEL).astype(_BF)

    yp = x_prompt
    ys = jnp.swapaxes(x_sample, 0, 1)
    outs_p = [[] for _ in range(5)]
    outs_s = [[] for _ in range(5)]
    for l in range(DEPTH):
        mix_w = (attn_sinks[l], win[l], wlru[l], wout[l], lru_conv_w[l], v256[l], sc_conv_w[l], gmix[l], ln1[l])
        moe_w = (wr[l], br[l], wg[l], wu[l], wd[l], ln2[l])
        yp, kp, vp, hp, lbp, sbp = _mixer_prompt(yp, *mix_w)
        yp = _moe_sorted(yp.reshape(B * S, D_MODEL), *moe_w).reshape(B, S, D_MODEL)
        ys, ks, vs, hs, lbs, sbs = _mixer_sample(
            ys, cache_k[l].reshape(DB, WINDOW, KV_W), cache_v[l].reshape(DB, WINDOW, KV_W),
            state_lru_h[l], state_lru_conv[l].reshape(DB, -1), state_sc_conv[l].reshape(DB, -1), *mix_w)
        ys = _moe(ys.reshape(TS * DB, D_MODEL), *moe_w, TS * DB).reshape(TS, DB, D_MODEL)
        for lst, arr in zip(outs_p, (kp.reshape(B, WINDOW, N_KV_HEADS, HEAD_DIM),
                                     vp.reshape(B, WINDOW, N_KV_HEADS, HEAD_DIM),
                                     hp.reshape(B, D_LRU), lbp, sbp)):
            lst.append(arr)
        for lst, arr in zip(outs_s, (ks.reshape(DB, WINDOW, N_KV_HEADS, HEAD_DIM),
                                     vs.reshape(DB, WINDOW, N_KV_HEADS, HEAD_DIM),
                                     hs, lbs.reshape(DB, LRU_CONV - 1, D_LRU),
                                     sbs.reshape(DB, SC_CONV - 1, D_SC))):
            lst.append(arr)
    return (yp, jnp.swapaxes(ys, 0, 1),
            *[jnp.stack(o) for o in outs_p], *[jnp.stack(o) for o in outs_s])
```

```python
import functools
import math

import jax
import jax.numpy as jnp
from jax import lax
from jax.experimental import pallas as pl
from jax.experimental.pallas import tpu as pltpu

D_MODEL = 1024
DEPTH = 2
HEAD_DIM = 64
N_HEADS = 8
N_KV_HEADS = 2
GQ = N_HEADS // N_KV_HEADS
WINDOW = 128
ALIBI_MAX = 8.0
D_LRU = 256
LRU_BLOCKS = 4
LRU_BLOCK_W = D_LRU // LRU_BLOCKS
LRU_CONV = 4
LRU_C = 8.0
D_SC = 256
SC_CONV = 3
Q_W = N_HEADS * HEAD_DIM
KV_W = N_KV_HEADS * HEAD_DIM
IN_W = Q_W + 2 * KV_W + 2 * D_LRU + 3 * D_SC
N_GROUPS = 4
EXPERTS_PER_GROUP = 4
N_EXPERTS = N_GROUPS * EXPERTS_PER_GROUP
D_EXPERT = 256
ALPHA = (2 * DEPTH) ** 0.25
LN_EPS = 1e-5
RMS_EPS = 1e-6
NEG_INF = -1e30

O_K = Q_W
O_V = O_K + KV_W
O_XR = O_V + KV_W
O_GR = O_XR + D_LRU
O_U = O_GR + D_LRU
O_BG = O_U + D_SC
O_CG = O_BG + D_SC

LANES = 128
SUBLANES = 8
N_QTILES = Q_W // LANES
ROUTE_W = LANES
VMEM_LIMIT = 56 * 1024 * 1024

T_PROMPT = 512
T_DISPATCH = 512
T_PERM = 2048
TOK_ROWS = D_MODEL // LANES
DMA_UNROLL = 8
BM = 512
BT_SAMPLE = 32
KPAD = WINDOW + SUBLANES

_BF = jnp.bfloat16
_F32 = jnp.float32


def _slopes():
    return [2.0 ** (-ALIBI_MAX * h / N_HEADS) for h in range(1, N_HEADS + 1)]


def _dot(a, b):
    return jnp.dot(a, b, preferred_element_type=_F32)


def _dot_t(a, b):
    return lax.dot_general(a, b, (((1,), (1,)), ((), ())), preferred_element_type=_F32)


def _rms_rows(y, g):
    return y * lax.rsqrt(jnp.mean(y * y, axis=-1, keepdims=True) + RMS_EPS) * g


def _ln_rows(x, g, b):
    mu = jnp.mean(x, axis=-1, keepdims=True)
    xc = x - mu
    var = jnp.mean(xc * xc, axis=-1, keepdims=True)
    return xc * lax.rsqrt(var + LN_EPS) * g + b


def _lru_coeffs(xcb, wlru, b_a, b_x, lam):
    gates = _dot(xcb.astype(_BF), wlru)
    r = jax.nn.sigmoid(gates[:, :D_LRU] + b_a)
    i = jax.nn.sigmoid(gates[:, D_LRU:] + b_x)
    log_a = (-LRU_C * jax.nn.softplus(-lam)) * r
    a = jnp.exp(log_a)
    b = jnp.sqrt(jnp.tanh(-log_a) * (1.0 + a * a)) * (i * xcb)
    return a, b


def _sink_softmax(s, sink):
    m = jnp.maximum(jnp.max(s, axis=-1, keepdims=True), sink)
    p = jnp.exp(s - m)
    denom = jnp.sum(p, axis=-1, keepdims=True) + jnp.exp(sink - m)
    return p, 1.0 / denom


def _mixer_prompt_kernel(sinks_ref, x_ref, win_ref, wlru_ref, wout_ref, cw_ref, v256_ref, scw_ref,
                         gmix_ref, ln_ref,
                         y_ref, knew_ref, vnew_ref, hnew_ref, lrubuf_ref, scbuf_ref,
                         proj, qsc, klo, khi, vext, xrext, zext, hst, a_s, b_s, hl_s, pc_s, bias_s, mixed):
    T = T_PROMPT
    nblk = T // WINDOW
    s_idx = pl.program_id(1)
    lane = lax.broadcasted_iota(jnp.int32, (1, LANES), 1)
    lo_lane = lane < HEAD_DIM

    @pl.when(s_idx == 0)
    def _():
        klo[0:WINDOW, :] = jnp.zeros((WINDOW, LANES), _BF)
        khi[0:WINDOW, :] = jnp.zeros((WINDOW, LANES), _BF)
        vext[0:WINDOW, :] = jnp.zeros((WINDOW, LANES), _BF)
        xrext[0:SUBLANES, :] = jnp.zeros((SUBLANES, D_LRU), _F32)
        zext[0:SUBLANES, :] = jnp.zeros((SUBLANES, D_SC), _F32)
        hst[...] = jnp.zeros((1, D_LRU), _F32)
        qi = lax.broadcasted_iota(jnp.int32, (WINDOW, 2 * WINDOW), 0)
        kc = lax.broadcasted_iota(jnp.int32, (WINDOW, 2 * WINDOW), 1)
        dist = qi + WINDOW - kc
        valid = (dist >= 0) & (dist < WINDOW)
        valid_first = valid & (kc >= WINDOW)
        distf = dist.astype(_F32)
        for h, slope in enumerate(_slopes()):
            bias = -slope * distf
            bias_s[0, h] = jnp.where(valid, bias, NEG_INF)
            bias_s[1, h] = jnp.where(valid_first, bias, NEG_INF)

    proj[...] = _dot(x_ref[0].astype(_BF), win_ref[...])

    for n in range(nblk):
        rows = slice(n * WINDOW, (n + 1) * WINDOW)
        for j in range(N_QTILES):
            qsc[n, j * WINDOW:(j + 1) * WINDOW, :] = (
                proj[rows, j * LANES:(j + 1) * LANES] * (HEAD_DIM ** -0.5)).astype(_BF)
    k = proj[:, O_K:O_K + KV_W]
    v = proj[:, O_V:O_V + KV_W]
    klo[WINDOW:, :] = jnp.where(lo_lane, k, 0.0).astype(_BF)
    khi[WINDOW:, :] = jnp.where(lo_lane, 0.0, k).astype(_BF)
    vext[WINDOW:, :] = v.astype(_BF)
    knew_ref[0] = k[T - WINDOW:, :]
    vnew_ref[0] = v[T - WINDOW:, :]

    def attn_block(n, carry):
        r0 = pl.multiple_of(n * WINDOW, WINDOW)
        first = jnp.logical_and(s_idx == 0, n == 0).astype(jnp.int32)
        qs = qsc[n]
        kl = klo[pl.ds(r0, 2 * WINDOW), :]
        kh = khi[pl.ds(r0, 2 * WINDOW), :]
        vv = vext[pl.ds(r0, 2 * WINDOW), :]
        s_lo = _dot_t(qs, kl)
        s_hi = _dot_t(qs, kh)
        p_lo, p_hi, inv_lo, inv_hi = [], [], [], []
        for j in range(N_QTILES):
            rj = slice(j * WINDOW, (j + 1) * WINDOW)
            p, inv = _sink_softmax(s_lo[rj] + bias_s[first, j], sinks_ref[j])
            p_lo.append(p.astype(_BF))
            inv_lo.append(inv)
            p, inv = _sink_softmax(s_hi[rj] + bias_s[first, GQ + j], sinks_ref[GQ + j])
            p_hi.append(p.astype(_BF))
            inv_hi.append(inv)
        o_lo = _dot(jnp.concatenate(p_lo, axis=0), vv)
        o_hi = _dot(jnp.concatenate(p_hi, axis=0), vv)
        for j in range(N_QTILES):
            rj = slice(j * WINDOW, (j + 1) * WINDOW)
            mixed[pl.ds(r0, WINDOW), j * LANES:(j + 1) * LANES] = jnp.where(
                lo_lane, o_lo[rj] * inv_lo[j], o_hi[rj] * inv_hi[j])
        return carry

    lax.fori_loop(0, nblk, attn_block, 0)
    klo[0:WINDOW, :] = klo[T:T + WINDOW, :]
    khi[0:WINDOW, :] = khi[T:T + WINDOW, :]
    vext[0:WINDOW, :] = vext[T:T + WINDOW, :]

    xr = proj[:, O_XR:O_XR + D_LRU]
    xrext[SUBLANES:, :] = xr
    xc = xr * cw_ref[LRU_CONV - 1:LRU_CONV, :]
    for kk in range(1, LRU_CONV):
        xc = xc + xrext[pl.ds(SUBLANES - kk, T), :] * cw_ref[LRU_CONV - 1 - kk:LRU_CONV - kk, :]
    xcb = xc + v256_ref[0:1, :]
    a, b = _lru_coeffs(xcb, wlru_ref[...], v256_ref[1:2, :], v256_ref[2:3, :], v256_ref[3:4, :])
    NH = D_LRU // LANES
    C = T // SUBLANES
    PITCH = C + SUBLANES
    for hh in range(NH):
        for c in range(SUBLANES):
            a_s[hh, c * PITCH:c * PITCH + C, :] = a[c * C:(c + 1) * C, hh * LANES:(hh + 1) * LANES]
            b_s[hh, c * PITCH:c * PITCH + C, :] = b[c * C:(c + 1) * C, hh * LANES:(hh + 1) * LANES]
    lrubuf_ref[0] = xrext[pl.ds(T + SUBLANES - (LRU_CONV - 1), LRU_CONV - 1), :]
    xrext[0:SUBLANES, :] = xrext[T:T + SUBLANES, :]

    def scan_step(s, carry):
        out = []
        for hh in range(NH):
            h, p = carry[hh]
            at = a_s[hh, pl.ds(s, SUBLANES, stride=PITCH), :]
            bt = b_s[hh, pl.ds(s, SUBLANES, stride=PITCH), :]
            h = at * h + bt
            p = at * p
            hl_s[hh, pl.ds(s, SUBLANES, stride=PITCH), :] = h
            pc_s[hh, pl.ds(s, SUBLANES, stride=PITCH), :] = p
            out.append((h, p))
        return tuple(out)

    init = tuple((jnp.zeros((SUBLANES, LANES), _F32), jnp.ones((SUBLANES, LANES), _F32))
                 for _ in range(NH))
    ends = lax.fori_loop(0, C, scan_step, init)
    h_end = jnp.concatenate([e[0] for e in ends], axis=-1)
    p_end = jnp.concatenate([e[1] for e in ends], axis=-1)
    h_in = hst[...]
    gmix_lru = gmix_ref[:, Q_W:Q_W + D_LRU]
    for c in range(SUBLANES):
        rows = slice(c * C, (c + 1) * C)
        prow = slice(c * PITCH, c * PITCH + C)
        hl = jnp.concatenate([hl_s[hh, prow, :] for hh in range(NH)], axis=-1)
        pc = jnp.concatenate([pc_s[hh, prow, :] for hh in range(NH)], axis=-1)
        hs = hl + pc * h_in
        y_lru = hs * jax.nn.gelu(proj[rows, O_GR:O_GR + D_LRU])
        mixed[rows, Q_W:Q_W + D_LRU] = _rms_rows(y_lru, gmix_lru)
        h_in = p_end[c:c + 1, :] * h_in + h_end[c:c + 1, :]
    hst[...] = h_in
    hnew_ref[0] = h_in

    z = proj[:, O_CG:O_CG + D_SC] * proj[:, O_U:O_U + D_SC]
    zext[SUBLANES:, :] = z
    yc = z * scw_ref[SC_CONV - 1:SC_CONV, :]
    for kk in range(1, SC_CONV):
        yc = yc + zext[pl.ds(SUBLANES - kk, T), :] * scw_ref[SC_CONV - 1 - kk:SC_CONV - kk, :]
    y_sc = proj[:, O_BG:O_BG + D_SC] * yc
    mixed[:, Q_W + D_LRU:] = _rms_rows(y_sc, gmix_ref[:, Q_W + D_LRU:])
    scbuf_ref[0] = zext[pl.ds(T + SUBLANES - (SC_CONV - 1), SC_CONV - 1), :]
    zext[0:SUBLANES, :] = zext[T:T + SUBLANES, :]

    mixed[:, 0:Q_W] = _rms_rows(mixed[:, 0:Q_W], gmix_ref[:, 0:Q_W])
    m = _dot(mixed[...].astype(_BF), wout_ref[...])
    y_ref[0] = _ln_rows(ALPHA * x_ref[0] + m, ln_ref[0:1, :], ln_ref[1:2, :])


def _layer_spec(stacked, l):
    nd = stacked.ndim - 1
    return pl.BlockSpec((None,) + stacked.shape[1:], lambda *_: (l,) + (0,) * nd,
                        pipeline_mode=pl.Buffered(1))


def _mixer_prompt(l, x, sinks, *params):
    B, S, _ = x.shape
    T = T_PROMPT
    grid = (B, S // T)
    seq_out = lambda shape: pl.BlockSpec((1,) + shape, lambda b, s: (b, 0, 0))
    return pl.pallas_call(
        _mixer_prompt_kernel,
        grid=grid,
        in_specs=[
            pl.BlockSpec(memory_space=pltpu.SMEM),
            pl.BlockSpec((1, T, D_MODEL), lambda b, s: (b, s, 0)),
            *[_layer_spec(p, l) for p in params],
        ],
        out_specs=[
            pl.BlockSpec((1, T, D_MODEL), lambda b, s: (b, s, 0)),
            seq_out((WINDOW, KV_W)), seq_out((WINDOW, KV_W)), seq_out((1, D_LRU)),
            seq_out((LRU_CONV - 1, D_LRU)), seq_out((SC_CONV - 1, D_SC)),
        ],
        out_shape=[
            jax.ShapeDtypeStruct((B, S, D_MODEL), _F32),
            jax.ShapeDtypeStruct((B, WINDOW, KV_W), _F32),
            jax.ShapeDtypeStruct((B, WINDOW, KV_W), _F32),
            jax.ShapeDtypeStruct((B, 1, D_LRU), _F32),
            jax.ShapeDtypeStruct((B, LRU_CONV - 1, D_LRU), _F32),
            jax.ShapeDtypeStruct((B, SC_CONV - 1, D_SC), _F32),
        ],
        scratch_shapes=[
            pltpu.VMEM((T, IN_W), _F32),
            pltpu.VMEM((T // WINDOW, N_QTILES * WINDOW, LANES), _BF),
            pltpu.VMEM((T + WINDOW, LANES), _BF),
            pltpu.VMEM((T + WINDOW, LANES), _BF),
            pltpu.VMEM((T + WINDOW, LANES), _BF),
            pltpu.VMEM((T + SUBLANES, D_LRU), _F32),
            pltpu.VMEM((T + SUBLANES, D_SC), _F32),
            pltpu.VMEM((1, D_LRU), _F32),
            pltpu.VMEM((D_LRU // LANES, T + SUBLANES * SUBLANES, LANES), _F32),
            pltpu.VMEM((D_LRU // LANES, T + SUBLANES * SUBLANES, LANES), _F32),
            pltpu.VMEM((D_LRU // LANES, T + SUBLANES * SUBLANES, LANES), _F32),
            pltpu.VMEM((D_LRU // LANES, T + SUBLANES * SUBLANES, LANES), _F32),
            pltpu.VMEM((2, N_HEADS, WINDOW, 2 * WINDOW), _F32),
            pltpu.VMEM((T, D_MODEL), _F32),
        ],
        compiler_params=pltpu.CompilerParams(
            dimension_semantics=("arbitrary", "arbitrary"), vmem_limit_bytes=VMEM_LIMIT),
        name="mixer_prompt",
    )(sinks, x, *params)


def _mixer_sample_kernel(sinks_ref, x_ref, ck_ref, cv_ref, h0_ref, lbuf_ref, sbuf_ref,
                         win_ref, wlru_ref, wout_ref, cw_ref, v256_ref, scw_ref, gmix_ref, ln_ref,
                         y_ref, knew_ref, vnew_ref, hnew_ref, lrubuf_ref, scbuf_ref,
                         kx2, vx2, qs2, os2):
    TS = x_ref.shape[0]
    BT = BT_SAMPLE
    NQ = TS * N_QTILES
    lane = lax.broadcasted_iota(jnp.int32, (1, LANES), 1)
    lo_lane = lane < HEAD_DIM

    x = x_ref[...].reshape(TS * BT, D_MODEL)
    proj = _dot(x.astype(_BF), win_ref[...])

    for bb in range(BT):
        kx2[bb * KPAD:bb * KPAD + WINDOW, :] = ck_ref[bb]
        vx2[bb * KPAD:bb * KPAD + WINDOW, :] = cv_ref[bb]
    for t in range(SUBLANES):
        if t < TS:
            rows = slice(t * BT, (t + 1) * BT)
            kx2[pl.ds(WINDOW + t, BT, stride=KPAD), :] = proj[rows, O_K:O_K + KV_W]
            vx2[pl.ds(WINDOW + t, BT, stride=KPAD), :] = proj[rows, O_V:O_V + KV_W]
            for j in range(N_QTILES):
                qs2[pl.ds(t * N_QTILES + j, BT, stride=NQ), :] = (
                    proj[rows, j * LANES:(j + 1) * LANES] * (HEAD_DIM ** -0.5))
        else:
            kx2[pl.ds(WINDOW + t, BT, stride=KPAD), :] = jnp.zeros((BT, LANES), _F32)
            vx2[pl.ds(WINDOW + t, BT, stride=KPAD), :] = jnp.zeros((BT, LANES), _F32)
    for bb in range(BT):
        knew_ref[bb] = kx2[bb * KPAD + TS:bb * KPAD + TS + WINDOW, :]
        vnew_ref[bb] = vx2[bb * KPAD + TS:bb * KPAD + TS + WINDOW, :]

    kall = kx2[...].reshape(BT, KPAD, LANES)
    qs = qs2[...].reshape(BT, NQ, LANES).astype(_BF)
    k_lo = jnp.where(lo_lane, kall, 0.0).astype(_BF)
    k_hi = jnp.where(lo_lane, 0.0, kall).astype(_BF)
    vv = vx2[...].reshape(BT, KPAD, LANES).astype(_BF)
    s_lo = jnp.einsum('bqd,bkd->bqk', qs, k_lo, preferred_element_type=_F32)
    s_hi = jnp.einsum('bqd,bkd->bqk', qs, k_hi, preferred_element_type=_F32)
    ri = lax.broadcasted_iota(jnp.int32, (NQ, KPAD), 0)
    ci = lax.broadcasted_iota(jnp.int32, (NQ, KPAD), 1)
    tq = lax.shift_right_logical(ri, int(math.log2(N_QTILES)))
    jq = ri & (N_QTILES - 1)
    dist = WINDOW + tq - ci
    valid = (dist >= 0) & (dist < WINDOW)
    distf = dist.astype(_F32)
    rj = lax.broadcasted_iota(jnp.int32, (NQ, 1), 0) & (N_QTILES - 1)
    slope_lo = jnp.zeros((NQ, KPAD), _F32)
    slope_hi = jnp.zeros((NQ, KPAD), _F32)
    sink_lo = jnp.zeros((NQ, 1), _F32)
    sink_hi = jnp.zeros((NQ, 1), _F32)
    slopes = _slopes()
    for j in range(N_QTILES):
        slope_lo = jnp.where(jq == j, slopes[j], slope_lo)
        slope_hi = jnp.where(jq == j, slopes[GQ + j], slope_hi)
        sink_lo = jnp.where(rj == j, sinks_ref[j], sink_lo)
        sink_hi = jnp.where(rj == j, sinks_ref[GQ + j], sink_hi)
    bias_lo = jnp.where(valid, -slope_lo * distf, NEG_INF)
    bias_hi = jnp.where(valid, -slope_hi * distf, NEG_INF)
    p_lo, inv_lo = _sink_softmax(s_lo + bias_lo[None], sink_lo[None])
    p_hi, inv_hi = _sink_softmax(s_hi + bias_hi[None], sink_hi[None])
    o_lo = jnp.einsum('bqk,bkd->bqd', p_lo.astype(_BF), vv, preferred_element_type=_F32)
    o_hi = jnp.einsum('bqk,bkd->bqd', p_hi.astype(_BF), vv, preferred_element_type=_F32)
    o = jnp.where(lo_lane[None], o_lo * inv_lo, o_hi * inv_hi)
    os2[...] = o.reshape(BT * NQ, LANES)

    lbuf = lbuf_ref[...]
    sbuf = sbuf_ref[...]
    xr_hist = [lbuf[:, i * D_LRU:(i + 1) * D_LRU] for i in range(LRU_CONV - 1)]
    z_hist = [sbuf[:, i * D_SC:(i + 1) * D_SC] for i in range(SC_CONV - 1)]
    h = h0_ref[...]
    gmix = gmix_ref[...]
    mixed_rows = []
    for t in range(TS):
        rows = slice(t * BT, (t + 1) * BT)
        o_att = jnp.concatenate(
            [os2[pl.ds(t * N_QTILES + j, BT, stride=NQ), :] for j in range(N_QTILES)], axis=-1)
        xr_hist.append(proj[rows, O_XR:O_XR + D_LRU])
        xc = sum(xr_hist[t + kk] * cw_ref[kk:kk + 1, :] for kk in range(LRU_CONV))
        xcb = xc + v256_ref[0:1, :]
        a, b = _lru_coeffs(xcb, wlru_ref[...], v256_ref[1:2, :], v256_ref[2:3, :], v256_ref[3:4, :])
        h = a * h + b
        y_lru = h * jax.nn.gelu(proj[rows, O_GR:O_GR + D_LRU])
        z_hist.append(proj[rows, O_CG:O_CG + D_SC] * proj[rows, O_U:O_U + D_SC])
        yc = sum(z_hist[t + kk] * scw_ref[kk:kk + 1, :] for kk in range(SC_CONV))
        y_sc = proj[rows, O_BG:O_BG + D_SC] * yc
        mixed_rows.append(jnp.concatenate(
            [_rms_rows(o_att, gmix[:, 0:Q_W]),
             _rms_rows(y_lru, gmix[:, Q_W:Q_W + D_LRU]),
             _rms_rows(y_sc, gmix[:, Q_W + D_LRU:])], axis=-1))
    hnew_ref[...] = h
    lrubuf_ref[...] = jnp.concatenate(xr_hist[TS:], axis=-1)
    scbuf_ref[...] = jnp.concatenate(z_hist[TS:], axis=-1)
    mixed = jnp.concatenate(mixed_rows, axis=0)
    m = _dot(mixed.astype(_BF), wout_ref[...])
    y = _ln_rows(ALPHA * x + m, ln_ref[0:1, :], ln_ref[1:2, :])
    y_ref[...] = y.reshape(TS, BT, D_MODEL)


def _mixer_sample(l, x, ck, cv, h0, lbuf, sbuf, sinks, *params):
    TS, B, _ = x.shape
    BT = BT_SAMPLE
    rows2 = lambda w: pl.BlockSpec((BT, w), lambda i: (i, 0))
    rows2_l = lambda w: pl.BlockSpec((None, BT, w), lambda i: (l, i, 0))
    cache = pl.BlockSpec((BT, WINDOW, KV_W), lambda i: (i, 0, 0))
    cache_l = pl.BlockSpec((None, BT, WINDOW, KV_W), lambda i: (l, i, 0, 0))
    xspec = pl.BlockSpec((TS, BT, D_MODEL), lambda i: (0, i, 0))
    return pl.pallas_call(
        _mixer_sample_kernel,
        grid=(B // BT,),
        in_specs=[
            pl.BlockSpec(memory_space=pltpu.SMEM),
            xspec, cache_l, cache_l, rows2_l(D_LRU), rows2_l((LRU_CONV - 1) * D_LRU),
            rows2_l((SC_CONV - 1) * D_SC),
            *[_layer_spec(p, l) for p in params],
        ],
        out_specs=[xspec, cache, cache, rows2(D_LRU), rows2((LRU_CONV - 1) * D_LRU),
                   rows2((SC_CONV - 1) * D_SC)],
        out_shape=[
            jax.ShapeDtypeStruct((TS, B, D_MODEL), _F32),
            jax.ShapeDtypeStruct((B, WINDOW, KV_W), _F32),
            jax.ShapeDtypeStruct((B, WINDOW, KV_W), _F32),
            jax.ShapeDtypeStruct((B, D_LRU), _F32),
            jax.ShapeDtypeStruct((B, (LRU_CONV - 1) * D_LRU), _F32),
            jax.ShapeDtypeStruct((B, (SC_CONV - 1) * D_SC), _F32),
        ],
        scratch_shapes=[
            pltpu.VMEM((BT * KPAD, LANES), _F32),
            pltpu.VMEM((BT * KPAD, LANES), _F32),
            pltpu.VMEM((BT * TS * N_QTILES, LANES), _F32),
            pltpu.VMEM((BT * TS * N_QTILES, LANES), _F32),
        ],
        compiler_params=pltpu.CompilerParams(
            dimension_semantics=("arbitrary",), vmem_limit_bytes=VMEM_LIMIT),
        name="mixer_sample",
    )(sinks, x, ck, cv, h0, lbuf, sbuf, *params)


def _route(logits):
    lane = lax.broadcasted_iota(jnp.int32, logits.shape, 1)
    big = jnp.int32(ROUTE_W)
    is_grp = lane < N_GROUPS
    glog = jnp.where(is_grp, logits, -jnp.inf)
    gmax = jnp.max(glog, axis=-1, keepdims=True)
    gsel = jnp.min(jnp.where(glog == gmax, lane, big), axis=-1, keepdims=True)
    gw = 1.0 / jnp.sum(jnp.exp(glog - gmax), axis=-1, keepdims=True)
    e_lo = N_GROUPS + gsel * EXPERTS_PER_GROUP
    in_grp = (lane >= e_lo) & (lane < e_lo + EXPERTS_PER_GROUP)
    elog = jnp.where(in_grp, logits, -jnp.inf)
    v1 = jnp.max(elog, axis=-1, keepdims=True)
    i1 = jnp.min(jnp.where(elog == v1, lane, big), axis=-1, keepdims=True)
    elog2 = jnp.where(lane == i1, -jnp.inf, elog)
    v2 = jnp.max(elog2, axis=-1, keepdims=True)
    i2 = jnp.min(jnp.where(elog2 == v2, lane, big), axis=-1, keepdims=True)
    e2 = jnp.exp(v2 - v1)
    w1 = gw / (1.0 + e2)
    w2 = gw * e2 / (1.0 + e2)
    return jnp.where(lane == i1, w1, jnp.where(lane == i2, w2, 0.0))


def _group_experts(xb, gates, g, wg, wu, wd, hsc):
    lane = lax.broadcasted_iota(jnp.int32, (1, ROUTE_W), 1)
    for e in range(EXPERTS_PER_GROUP):
        hg = _dot(xb, wg[e])
        hu = _dot(xb, wu[e])
        gate = jnp.sum(jnp.where(lane == N_GROUPS + g * EXPERTS_PER_GROUP + e, gates, 0.0),
                       axis=-1, keepdims=True)
        hsc[:, e * D_EXPERT:(e + 1) * D_EXPERT] = (jax.nn.silu(hg) * hu * gate).astype(_BF)
    return _dot(hsc[...], wd)


def _moe_kernel(x_ref, wr_ref, br_ref, wg_ref, wu_ref, wd_ref, ln_ref, y_ref, xb_s, gates_s, hsc, acc):
    g = pl.program_id(0)

    @pl.when(g == 0)
    def _():
        xb = x_ref[...].astype(_BF)
        xb_s[...] = xb
        gates_s[...] = _route(_dot(xb, wr_ref[...]) + br_ref[...])
        acc[...] = jnp.zeros(acc.shape, _F32)

    wd = wd_ref[...].astype(_BF).reshape(EXPERTS_PER_GROUP * D_EXPERT, D_MODEL)
    acc[...] += _group_experts(xb_s[...], gates_s[...], g, wg_ref[...].astype(_BF),
                               wu_ref[...].astype(_BF), wd, hsc)

    @pl.when(g == N_GROUPS - 1)
    def _():
        y_ref[...] = _ln_rows(ALPHA * x_ref[...] + acc[...], ln_ref[0:1, :], ln_ref[1:2, :])


def _group_weight_spec(w, index_map):
    return pl.BlockSpec((None, EXPERTS_PER_GROUP) + w.shape[2:], index_map)


def _moe(l, x, wr, br, wg, wu, wd, ln):
    N = x.shape[0]
    tok = pl.BlockSpec((N, D_MODEL), lambda g: (0, 0))
    gw = lambda w: _group_weight_spec(w, lambda g: (l, g, 0, 0))
    return pl.pallas_call(
        _moe_kernel,
        grid=(N_GROUPS,),
        in_specs=[tok, _layer_spec(wr, l), _layer_spec(br, l), gw(wg), gw(wu), gw(wd), _layer_spec(ln, l)],
        out_specs=tok,
        out_shape=jax.ShapeDtypeStruct((N, D_MODEL), _F32),
        scratch_shapes=[
            pltpu.VMEM((N, D_MODEL), _BF),
            pltpu.VMEM((N, ROUTE_W), _F32),
            pltpu.VMEM((N, EXPERTS_PER_GROUP * D_EXPERT), _BF),
            pltpu.VMEM((N, D_MODEL), _F32),
        ],
        compiler_params=pltpu.CompilerParams(
            dimension_semantics=("arbitrary",), vmem_limit_bytes=VMEM_LIMIT),
        name="moe",
    )(x, wr, br, wg, wu, wd, ln)


CODE_ROWS = SUBLANES


def _route_kernel(x_ref, wr_ref, br_ref, code_ref, cnt_ref, utri, cnt_s):
    T = T_DISPATCH
    i = pl.program_id(0)

    @pl.when(i == 0)
    def _():
        cnt_s[...] = jnp.zeros((SUBLANES, T), _F32)
        r = lax.broadcasted_iota(jnp.int32, (T, T), 0)
        c = lax.broadcasted_iota(jnp.int32, (T, T), 1)
        utri[...] = jnp.where(r < c, 1.0, 0.0).astype(_BF)

    logits = _dot(x_ref[...].astype(_BF), wr_ref[...]) + br_ref[...]
    lt = logits.T[0:SUBLANES]
    g = [lt[k:k + 1] for k in range(N_GROUPS)]
    gmax = functools.reduce(jnp.maximum, g)
    gsel = jnp.full((1, T), N_GROUPS - 1, jnp.int32)
    for k in range(N_GROUPS - 2, -1, -1):
        gsel = jnp.where(g[k] == gmax, k, gsel)
    row = lax.broadcasted_iota(jnp.int32, (SUBLANES, T), 0)
    onehot = row == gsel
    onehot_f = jnp.where(onehot, 1.0, 0.0)
    rank = _dot(onehot_f.astype(_BF), utri[...])
    cnt = cnt_s[...]
    loc = jnp.sum(jnp.where(onehot, cnt + rank, 0.0), axis=0, keepdims=True).astype(jnp.int32)
    cnt_new = cnt + jnp.sum(onehot_f, axis=1, keepdims=True)
    cnt_s[...] = cnt_new
    cnt_ref[...] = cnt_new[:, 0:LANES].astype(jnp.int32)
    code_ref[0] = jnp.where(row == 0, loc, jnp.where(row == 1, gsel, 0))


def _route_tokens(l, x, wr, br):
    N = x.shape[0]
    T = T_DISPATCH
    return pl.pallas_call(
        _route_kernel,
        grid=(N // T,),
        in_specs=[pl.BlockSpec((T, D_MODEL), lambda i: (i, 0)),
                  _layer_spec(wr, l), _layer_spec(br, l)],
        out_specs=[pl.BlockSpec((1, CODE_ROWS, T), lambda i: (i, 0, 0)),
                   pl.BlockSpec((SUBLANES, LANES), lambda i: (0, 0))],
        out_shape=[jax.ShapeDtypeStruct((N // T, CODE_ROWS, T), jnp.int32),
                   jax.ShapeDtypeStruct((SUBLANES, LANES), jnp.int32)],
        scratch_shapes=[
            pltpu.VMEM((T, T), _BF),
            pltpu.VMEM((SUBLANES, T), _F32),
        ],
        compiler_params=pltpu.CompilerParams(
            dimension_semantics=("arbitrary",), vmem_limit_bytes=VMEM_LIMIT),
        name="moe_route",
    )(x, wr, br)


def _to_token_rows(x, rows_ref):
    R = x.shape[0]
    for c in range(TOK_ROWS):
        rows_ref[pl.ds(c, R, stride=TOK_ROWS), :] = x[:, c * LANES:(c + 1) * LANES]


def _from_token_rows(rows_ref, R):
    return jnp.concatenate(
        [rows_ref[pl.ds(c, R, stride=TOK_ROWS), :] for c in range(TOK_ROWS)], axis=-1)


def _token_copy(src_ref, s, dst_ref, d, sem):
    return pltpu.make_async_copy(
        src_ref.at[pl.ds(pl.multiple_of(s * TOK_ROWS, TOK_ROWS), TOK_ROWS)],
        dst_ref.at[pl.ds(pl.multiple_of(d * TOK_ROWS, TOK_ROWS), TOK_ROWS)], sem)


def _load_dest(dest_ref, dest_sm, sem_idx):
    copies = [pltpu.make_async_copy(dest_ref.at[j, 0], dest_sm.at[pl.ds(j * T_DISPATCH, T_DISPATCH)], sem_idx)
              for j in range(T_PERM // T_DISPATCH)]
    for cp in copies:
        cp.start()
    for cp in copies:
        cp.wait()


def _scatter_kernel(off_ref, code_ref, x_ref, xs_ref, dest_ref, xrow, dest_sm, sem_idx, sem_rows):
    code = code_ref[...]
    loc = code[:, 0:1]
    grp = code[:, 1:2]
    off = jnp.zeros(loc.shape, jnp.int32)
    for g in range(N_GROUPS):
        off = jnp.where(grp == g, off_ref[g], off)
    dest_ref[...] = off + loc
    _to_token_rows(x_ref[...], xrow)
    _load_dest(dest_ref, dest_sm, sem_idx)

    def start(i, c):
        for k in range(DMA_UNROLL):
            t = i * DMA_UNROLL + k
            _token_copy(xrow, t, xs_ref, dest_sm[t], sem_rows).start(priority=k % 2)
        return c

    lax.fori_loop(0, T_PERM // DMA_UNROLL, start, 0)

    def wait(t, c):
        _token_copy(xrow, 0, xs_ref, 0, sem_rows).wait()
        return c

    lax.fori_loop(0, T_PERM, wait, 0, unroll=DMA_UNROLL)


def _scatter(off, code, x):
    N = x.shape[0]
    T = T_PERM
    nt = T // T_DISPATCH
    return pl.pallas_call(
        _scatter_kernel,
        grid=(N // T,),
        in_specs=[pl.BlockSpec(memory_space=pltpu.SMEM),
                  pl.BlockSpec((nt, CODE_ROWS, T_DISPATCH), lambda i: (i, 0, 0)),
                  pl.BlockSpec((T, D_MODEL), lambda i: (i, 0))],
        out_specs=[pl.BlockSpec(memory_space=pl.ANY),
                   pl.BlockSpec((nt, 1, T_DISPATCH), lambda i: (i, 0, 0))],
        out_shape=[jax.ShapeDtypeStruct((N * TOK_ROWS, LANES), _F32),
                   jax.ShapeDtypeStruct((N // T_DISPATCH, 1, T_DISPATCH), jnp.int32)],
        scratch_shapes=[pltpu.VMEM((T * TOK_ROWS, LANES), _F32),
                        pltpu.SMEM((T,), jnp.int32), pltpu.SemaphoreType.DMA, pltpu.SemaphoreType.DMA],
        compiler_params=pltpu.CompilerParams(
            dimension_semantics=("arbitrary",), vmem_limit_bytes=VMEM_LIMIT),
        name="moe_scatter",
    )(off, code, x)


def _route_in_group(logits, g):
    lane = lax.broadcasted_iota(jnp.int32, logits.shape, 1)
    big = jnp.int32(ROUTE_W)
    glog = jnp.where(lane < N_GROUPS, logits, -jnp.inf)
    gmax = jnp.max(glog, axis=-1, keepdims=True)
    lg = jnp.sum(jnp.where(lane == g, logits, 0.0), axis=-1, keepdims=True)
    gw = jnp.exp(lg - gmax) / jnp.sum(jnp.exp(glog - gmax), axis=-1, keepdims=True)
    e_lo = N_GROUPS + g * EXPERTS_PER_GROUP
    in_grp = (lane >= e_lo) & (lane < e_lo + EXPERTS_PER_GROUP)
    elog = jnp.where(in_grp, logits, -jnp.inf)
    v1 = jnp.max(elog, axis=-1, keepdims=True)
    i1 = jnp.min(jnp.where(elog == v1, lane, big), axis=-1, keepdims=True)
    elog2 = jnp.where(lane == i1, -jnp.inf, elog)
    v2 = jnp.max(elog2, axis=-1, keepdims=True)
    i2 = jnp.min(jnp.where(elog2 == v2, lane, big), axis=-1, keepdims=True)
    e2 = jnp.exp(v2 - v1)
    w1 = gw / (1.0 + e2)
    w2 = gw * e2 / (1.0 + e2)
    return jnp.where(lane == i1, w1, jnp.where(lane == i2, w2, 0.0))


def _moe_grouped_kernel(bidx_ref, bgrp_ref, blo_ref, bhi_ref, bfirst_ref, gfirst_ref, nstep_ref,
                        x_ref, wr_ref, br_ref, wg_ref, wu_ref, wd_ref, ln_ref, y_ref,
                        wg_b, wu_b, wd_b, hsc):
    i = pl.program_id(0)

    @pl.when(gfirst_ref[i] == 1)
    def _():
        wg_b[...] = wg_ref[...].astype(_BF)
        wu_b[...] = wu_ref[...].astype(_BF)
        wd_b[...] = wd_ref[...].astype(_BF).reshape(wd_b.shape)

    @pl.when(i < nstep_ref[0])
    def _():
        g = bgrp_ref[i]
        x = _from_token_rows(x_ref, BM)
        xb = x.astype(_BF)
        gates = _route_in_group(_dot(xb, wr_ref[...]) + br_ref[...], g)
        y = _group_experts(xb, gates, g, wg_b, wu_b, wd_b[...], hsc)
        y_new = _ln_rows(ALPHA * x + y, ln_ref[0:1, :], ln_ref[1:2, :])

        @pl.when(bfirst_ref[i] == 1)
        def _():
            _to_token_rows(y_new, y_ref)

        @pl.when(bfirst_ref[i] == 0)
        def _():
            row = lax.broadcasted_iota(jnp.int32, (BM, 1), 0)
            mine = (row >= blo_ref[i]) & (row < bhi_ref[i])
            _to_token_rows(jnp.where(mine, y_new, _from_token_rows(y_ref, BM)), y_ref)


def _moe_grouped(l, tables, xs, wr, br, wg, wu, wd, ln):
    nsteps = tables[0].shape[0]
    tok = pl.BlockSpec((BM * TOK_ROWS, LANES), lambda i, bidx, *_: (bidx[i], 0))
    gw = lambda w: _group_weight_spec(w, lambda i, bidx, bgrp, *_: (l, bgrp[i], 0, 0))
    return pl.pallas_call(
        _moe_grouped_kernel,
        grid_spec=pltpu.PrefetchScalarGridSpec(
            num_scalar_prefetch=len(tables),
            grid=(nsteps,),
            in_specs=[tok, _layer_spec(wr, l), _layer_spec(br, l), gw(wg), gw(wu), gw(wd),
                      _layer_spec(ln, l)],
            out_specs=tok,
            scratch_shapes=[
                pltpu.VMEM((EXPERTS_PER_GROUP, D_MODEL, D_EXPERT), _BF),
                pltpu.VMEM((EXPERTS_PER_GROUP, D_MODEL, D_EXPERT), _BF),
                pltpu.VMEM((EXPERTS_PER_GROUP * D_EXPERT, D_MODEL), _BF),
                pltpu.VMEM((BM, EXPERTS_PER_GROUP * D_EXPERT), _BF),
            ],
        ),
        out_shape=jax.ShapeDtypeStruct(xs.shape, _F32),
        compiler_params=pltpu.CompilerParams(
            dimension_semantics=("arbitrary",), vmem_limit_bytes=VMEM_LIMIT),
        name="moe_grouped",
    )(*tables, xs, wr, br, wg, wu, wd, ln)


def _combine_kernel(dest_ref, ys_ref, out_ref, yrow, dest_sm, sem_idx, sem_rows):
    _load_dest(dest_ref, dest_sm, sem_idx)

    def start(i, c):
        for k in range(DMA_UNROLL):
            t = i * DMA_UNROLL + k
            _token_copy(ys_ref, dest_sm[t], yrow, t, sem_rows).start(priority=k % 2)
        return c

    lax.fori_loop(0, T_PERM // DMA_UNROLL, start, 0)

    def wait(t, c):
        _token_copy(ys_ref, 0, yrow, 0, sem_rows).wait()
        return c

    lax.fori_loop(0, T_PERM, wait, 0, unroll=DMA_UNROLL)
    out_ref[...] = _from_token_rows(yrow, T_PERM)


def _combine(dest, ys):
    N = dest.shape[0] * T_DISPATCH
    T = T_PERM
    nt = T // T_DISPATCH
    return pl.pallas_call(
        _combine_kernel,
        grid=(N // T,),
        in_specs=[pl.BlockSpec((nt, 1, T_DISPATCH), lambda i: (i, 0, 0)),
                  pl.BlockSpec(memory_space=pl.ANY)],
        out_specs=pl.BlockSpec((T, D_MODEL), lambda i: (i, 0)),
        out_shape=jax.ShapeDtypeStruct((N, D_MODEL), _F32),
        scratch_shapes=[pltpu.VMEM((T * TOK_ROWS, LANES), _F32),
                        pltpu.SMEM((T,), jnp.int32), pltpu.SemaphoreType.DMA, pltpu.SemaphoreType.DMA],
        compiler_params=pltpu.CompilerParams(
            dimension_semantics=("arbitrary",), vmem_limit_bytes=VMEM_LIMIT),
        name="moe_combine",
    )(dest, ys)


def _step_tables(cnt, nsteps):
    n = cnt[:N_GROUPS, 0]
    off = jnp.cumsum(n) - n
    b_lo = off // BM
    nb = jnp.where(n > 0, (off + n - 1) // BM - b_lo + 1, 0)
    ends = jnp.cumsum(nb)
    total = ends[-1]
    step = jnp.minimum(jnp.arange(nsteps, dtype=jnp.int32), total - 1)
    grp = jnp.sum(step[:, None] >= ends[None, :], axis=1).astype(jnp.int32)
    bidx = b_lo[grp] + step - (ends - nb)[grp]
    lo = jnp.clip(off[grp] - bidx * BM, 0, BM)
    hi = jnp.clip(off[grp] + n[grp] - bidx * BM, 0, BM)
    one = jnp.ones((1,), jnp.int32)
    first = jnp.concatenate([one, (bidx[1:] != bidx[:-1]).astype(jnp.int32)])
    gfirst = jnp.concatenate([one, (grp[1:] != grp[:-1]).astype(jnp.int32)])
    tables = tuple(a.astype(jnp.int32) for a in (bidx, grp, lo, hi, first, gfirst, total[None]))
    return off.astype(jnp.int32), tables


def _moe_sorted(l, x, wr, br, wg, wu, wd, ln):
    N = x.shape[0]
    code, cnt = _route_tokens(l, x, wr, br)
    off, tables = _step_tables(cnt, N // BM + N_GROUPS - 1)
    xs, dest = _scatter(off, code, x)
    ys = _moe_grouped(l, tables, xs, wr, br, wg, wu, wd, ln)
    return _combine(dest, ys)


def _q_perm(w, axis):
    shape = w.shape
    w = w.reshape(shape[:axis] + (N_KV_HEADS, GQ, HEAD_DIM) + shape[axis + 1:])
    return jnp.swapaxes(w, axis, axis + 1).reshape(shape)


def _block_diag(w):
    eye = jnp.eye(LRU_BLOCKS, dtype=w.dtype)
    return jnp.einsum('nij,nm->nimj', w, eye).reshape(D_LRU, D_LRU)


def kernel(x_prompt, x_sample, cache_k, cache_v, state_lru_h, state_lru_conv, state_sc_conv, w_in, attn_sinks, lru_conv_w, lru_conv_b, lru_w_a, lru_b_a, lru_w_x, lru_b_x, lru_lambda, sc_conv_w, g_mix, w_out, ln1_g, ln1_b, w_grp, b_grp, w_route, b_route, w_gate, w_up, w_down, ln2_g, ln2_b):
    B, S, _ = x_prompt.shape
    DB, TS, _ = x_sample.shape
    assert S % T_PROMPT == 0 and (B * S) % T_PERM == 0 and (B * S) % BM == 0 and DB % BT_SAMPLE == 0
    assert T_PERM % T_DISPATCH == 0
    assert B * S < (1 << 24)
    assert TS <= SUBLANES and cache_k.shape[2] == WINDOW

    win = jnp.concatenate([_q_perm(w_in[:, :, :Q_W], 2), w_in[:, :, Q_W:]], axis=2).astype(_BF)
    wout = jnp.concatenate([_q_perm(w_out[:, :Q_W], 1), w_out[:, Q_W:]], axis=1).astype(_BF)
    gmix = jnp.concatenate([_q_perm(g_mix[:, :Q_W], 1), g_mix[:, Q_W:]], axis=1)[:, None, :]
    wlru = jnp.concatenate([jax.vmap(_block_diag)(lru_w_a), jax.vmap(_block_diag)(lru_w_x)], axis=2).astype(_BF)
    v256 = jnp.stack([lru_conv_b, lru_b_a, lru_b_x, lru_lambda], axis=1)
    ln1 = jnp.stack([ln1_g, ln1_b], axis=1)
    ln2 = jnp.stack([ln2_g, ln2_b], axis=1)
    pad = ROUTE_W - N_GROUPS - N_EXPERTS
    wr = jnp.pad(jnp.concatenate([w_grp, w_route], axis=2), ((0, 0), (0, 0), (0, pad))).astype(_BF)
    br = jnp.pad(jnp.concatenate([b_grp, b_route], axis=1), ((0, 0), (0, pad)))[:, None, :]
    mix_w = (win, wlru, wout, lru_conv_w, v256, sc_conv_w, gmix, ln1)
    moe_w = (wr, br, w_gate, w_up, w_down, ln2)
    ck = cache_k.reshape(DEPTH, DB, WINDOW, KV_W)
    cv = cache_v.reshape(DEPTH, DB, WINDOW, KV_W)
    lbuf = state_lru_conv.reshape(DEPTH, DB, -1)
    sbuf = state_sc_conv.reshape(DEPTH, DB, -1)

    yp = x_prompt
    ys = jnp.swapaxes(x_sample, 0, 1)
    outs_p = [[] for _ in range(5)]
    outs_s = [[] for _ in range(5)]
    for l in range(DEPTH):
        yp, kp, vp, hp, lbp, sbp = _mixer_prompt(l, yp, attn_sinks[l], *mix_w)
        yp = _moe_sorted(l, yp.reshape(B * S, D_MODEL), *moe_w).reshape(B, S, D_MODEL)
        ys, ks, vs, hs, lbs, sbs = _mixer_sample(l, ys, ck, cv, state_lru_h, lbuf, sbuf, attn_sinks[l], *mix_w)
        ys = _moe(l, ys.reshape(TS * DB, D_MODEL), *moe_w).reshape(TS, DB, D_MODEL)
        for lst, arr in zip(outs_p, (kp.reshape(B, WINDOW, N_KV_HEADS, HEAD_DIM),
                                     vp.reshape(B, WINDOW, N_KV_HEADS, HEAD_DIM),
                                     hp.reshape(B, D_LRU), lbp, sbp)):
            lst.append(arr)
        for lst, arr in zip(outs_s, (ks.reshape(DB, WINDOW, N_KV_HEADS, HEAD_DIM),
                                     vs.reshape(DB, WINDOW, N_KV_HEADS, HEAD_DIM),
                                     hs, lbs.reshape(DB, LRU_CONV - 1, D_LRU),
                                     sbs.reshape(DB, SC_CONV - 1, D_SC))):
            lst.append(arr)
    return (yp, jnp.swapaxes(ys, 0, 1),
            *[jnp.stack(o) for o in outs_p], *[jnp.stack(o) for o in outs_s])
```

```python
import functools
import math

import jax
import jax.numpy as jnp
from jax import lax
from jax.experimental import pallas as pl
from jax.experimental.pallas import tpu as pltpu

D_MODEL = 1024
DEPTH = 2
HEAD_DIM = 64
N_HEADS = 8
N_KV_HEADS = 2
GQ = N_HEADS // N_KV_HEADS
WINDOW = 128
ALIBI_MAX = 8.0
D_LRU = 256
LRU_BLOCKS = 4
LRU_BLOCK_W = D_LRU // LRU_BLOCKS
LRU_CONV = 4
LRU_C = 8.0
D_SC = 256
SC_CONV = 3
Q_W = N_HEADS * HEAD_DIM
KV_W = N_KV_HEADS * HEAD_DIM
IN_W = Q_W + 2 * KV_W + 2 * D_LRU + 3 * D_SC
N_GROUPS = 4
EXPERTS_PER_GROUP = 4
N_EXPERTS = N_GROUPS * EXPERTS_PER_GROUP
D_EXPERT = 256
ALPHA = (2 * DEPTH) ** 0.25
LN_EPS = 1e-5
RMS_EPS = 1e-6
NEG_INF = -1e30

O_K = Q_W
O_V = O_K + KV_W
O_XR = O_V + KV_W
O_GR = O_XR + D_LRU
O_U = O_GR + D_LRU
O_BG = O_U + D_SC
O_CG = O_BG + D_SC

LANES = 128
SUBLANES = 8
N_QTILES = Q_W // LANES
ROUTE_W = LANES
VMEM_LIMIT = 56 * 1024 * 1024

T_PROMPT = 512
T_DISPATCH = 512
T_PERM = 2048
TOK_ROWS = D_MODEL // LANES
DMA_UNROLL = 8
BM = 512
BT_SAMPLE = 32
KPAD = WINDOW + SUBLANES

_BF = jnp.bfloat16
_F32 = jnp.float32


def _slopes():
    return [2.0 ** (-ALIBI_MAX * h / N_HEADS) for h in range(1, N_HEADS + 1)]


def _dot(a, b):
    return jnp.dot(a, b, preferred_element_type=_F32)


def _dot_t(a, b):
    return lax.dot_general(a, b, (((1,), (1,)), ((), ())), preferred_element_type=_F32)


def _rms_rows(y, g):
    return y * lax.rsqrt(jnp.mean(y * y, axis=-1, keepdims=True) + RMS_EPS) * g


def _ln_rows(x, g, b):
    mu = jnp.mean(x, axis=-1, keepdims=True)
    xc = x - mu
    var = jnp.mean(xc * xc, axis=-1, keepdims=True)
    return xc * lax.rsqrt(var + LN_EPS) * g + b


def _lru_coeffs(xcb, wlru, b_a, b_x, lam):
    gates = _dot(xcb.astype(_BF), wlru)
    r = jax.nn.sigmoid(gates[:, :D_LRU] + b_a)
    i = jax.nn.sigmoid(gates[:, D_LRU:] + b_x)
    log_a = (-LRU_C * jax.nn.softplus(-lam)) * r
    a = jnp.exp(log_a)
    b = jnp.sqrt(jnp.tanh(-log_a) * (1.0 + a * a)) * (i * xcb)
    return a, b


def _sink_softmax(s, sink):
    m = jnp.maximum(jnp.max(s, axis=-1, keepdims=True), sink)
    p = jnp.exp(s - m)
    denom = jnp.sum(p, axis=-1, keepdims=True) + jnp.exp(sink - m)
    return p, 1.0 / denom


def _mixer_prompt_kernel(sinks_ref, x_ref, win_ref, wlru_ref, wout_ref, cw_ref, v256_ref, scw_ref,
                         gmix_ref, ln_ref,
                         y_ref, knew_ref, vnew_ref, hnew_ref, lrubuf_ref, scbuf_ref,
                         proj, qsc, klo, khi, vext, xrext, zext, hst, a_s, b_s, hl_s, pc_s, bias_s, mixed):
    T = T_PROMPT
    nblk = T // WINDOW
    s_idx = pl.program_id(1)
    lane = lax.broadcasted_iota(jnp.int32, (1, LANES), 1)
    lo_lane = lane < HEAD_DIM

    @pl.when(s_idx == 0)
    def _():
        klo[0:WINDOW, :] = jnp.zeros((WINDOW, LANES), _BF)
        khi[0:WINDOW, :] = jnp.zeros((WINDOW, LANES), _BF)
        vext[0:WINDOW, :] = jnp.zeros((WINDOW, LANES), _BF)
        xrext[0:SUBLANES, :] = jnp.zeros((SUBLANES, D_LRU), _F32)
        zext[0:SUBLANES, :] = jnp.zeros((SUBLANES, D_SC), _F32)
        hst[...] = jnp.zeros((1, D_LRU), _F32)
        qi = lax.broadcasted_iota(jnp.int32, (WINDOW, 2 * WINDOW), 0)
        kc = lax.broadcasted_iota(jnp.int32, (WINDOW, 2 * WINDOW), 1)
        dist = qi + WINDOW - kc
        valid = (dist >= 0) & (dist < WINDOW)
        valid_first = valid & (kc >= WINDOW)
        distf = dist.astype(_F32)
        for h, slope in enumerate(_slopes()):
            bias = -slope * distf
            bias_s[0, h] = jnp.where(valid, bias, NEG_INF)
            bias_s[1, h] = jnp.where(valid_first, bias, NEG_INF)

    proj[...] = _dot(x_ref[0].astype(_BF), win_ref[...])

    for n in range(nblk):
        rows = slice(n * WINDOW, (n + 1) * WINDOW)
        for j in range(N_QTILES):
            qsc[n, j * WINDOW:(j + 1) * WINDOW, :] = (
                proj[rows, j * LANES:(j + 1) * LANES] * (HEAD_DIM ** -0.5)).astype(_BF)
    k = proj[:, O_K:O_K + KV_W]
    v = proj[:, O_V:O_V + KV_W]
    klo[WINDOW:, :] = jnp.where(lo_lane, k, 0.0).astype(_BF)
    khi[WINDOW:, :] = jnp.where(lo_lane, 0.0, k).astype(_BF)
    vext[WINDOW:, :] = v.astype(_BF)
    knew_ref[0] = k[T - WINDOW:, :]
    vnew_ref[0] = v[T - WINDOW:, :]

    def attn_block(n, carry):
        r0 = pl.multiple_of(n * WINDOW, WINDOW)
        first = jnp.logical_and(s_idx == 0, n == 0).astype(jnp.int32)
        qs = qsc[n]
        kl = klo[pl.ds(r0, 2 * WINDOW), :]
        kh = khi[pl.ds(r0, 2 * WINDOW), :]
        vv = vext[pl.ds(r0, 2 * WINDOW), :]
        s_lo = _dot_t(qs, kl)
        s_hi = _dot_t(qs, kh)
        p_lo, p_hi, inv_lo, inv_hi = [], [], [], []
        for j in range(N_QTILES):
            rj = slice(j * WINDOW, (j + 1) * WINDOW)
            p, inv = _sink_softmax(s_lo[rj] + bias_s[first, j], sinks_ref[j])
            p_lo.append(p.astype(_BF))
            inv_lo.append(inv)
            p, inv = _sink_softmax(s_hi[rj] + bias_s[first, GQ + j], sinks_ref[GQ + j])
            p_hi.append(p.astype(_BF))
            inv_hi.append(inv)
        o_lo = _dot(jnp.concatenate(p_lo, axis=0), vv)
        o_hi = _dot(jnp.concatenate(p_hi, axis=0), vv)
        for j in range(N_QTILES):
            rj = slice(j * WINDOW, (j + 1) * WINDOW)
            mixed[pl.ds(r0, WINDOW), j * LANES:(j + 1) * LANES] = jnp.where(
                lo_lane, o_lo[rj] * inv_lo[j], o_hi[rj] * inv_hi[j])
        return carry

    lax.fori_loop(0, nblk, attn_block, 0, unroll=True)
    klo[0:WINDOW, :] = klo[T:T + WINDOW, :]
    khi[0:WINDOW, :] = khi[T:T + WINDOW, :]
    vext[0:WINDOW, :] = vext[T:T + WINDOW, :]

    xr = proj[:, O_XR:O_XR + D_LRU]
    xrext[SUBLANES:, :] = xr
    xc = xr * cw_ref[LRU_CONV - 1:LRU_CONV, :]
    for kk in range(1, LRU_CONV):
        xc = xc + xrext[pl.ds(SUBLANES - kk, T), :] * cw_ref[LRU_CONV - 1 - kk:LRU_CONV - kk, :]
    xcb = xc + v256_ref[0:1, :]
    a, b = _lru_coeffs(xcb, wlru_ref[...], v256_ref[1:2, :], v256_ref[2:3, :], v256_ref[3:4, :])
    NH = D_LRU // LANES
    C = T // SUBLANES
    PITCH = C + SUBLANES
    for hh in range(NH):
        for c in range(SUBLANES):
            a_s[hh, c * PITCH:c * PITCH + C, :] = a[c * C:(c + 1) * C, hh * LANES:(hh + 1) * LANES]
            b_s[hh, c * PITCH:c * PITCH + C, :] = b[c * C:(c + 1) * C, hh * LANES:(hh + 1) * LANES]
    lrubuf_ref[0] = xrext[pl.ds(T + SUBLANES - (LRU_CONV - 1), LRU_CONV - 1), :]
    xrext[0:SUBLANES, :] = xrext[T:T + SUBLANES, :]

    def scan_step(s, carry):
        out = []
        for hh in range(NH):
            h, p = carry[hh]
            at = a_s[hh, pl.ds(s, SUBLANES, stride=PITCH), :]
            bt = b_s[hh, pl.ds(s, SUBLANES, stride=PITCH), :]
            h = at * h + bt
            p = at * p
            hl_s[hh, pl.ds(s, SUBLANES, stride=PITCH), :] = h
            pc_s[hh, pl.ds(s, SUBLANES, stride=PITCH), :] = p
            out.append((h, p))
        return tuple(out)

    init = tuple((jnp.zeros((SUBLANES, LANES), _F32), jnp.ones((SUBLANES, LANES), _F32))
                 for _ in range(NH))
    ends = lax.fori_loop(0, C, scan_step, init, unroll=4)
    h_end = jnp.concatenate([e[0] for e in ends], axis=-1)
    p_end = jnp.concatenate([e[1] for e in ends], axis=-1)
    h_in = hst[...]
    gmix_lru = gmix_ref[:, Q_W:Q_W + D_LRU]
    for c in range(SUBLANES):
        rows = slice(c * C, (c + 1) * C)
        prow = slice(c * PITCH, c * PITCH + C)
        hl = jnp.concatenate([hl_s[hh, prow, :] for hh in range(NH)], axis=-1)
        pc = jnp.concatenate([pc_s[hh, prow, :] for hh in range(NH)], axis=-1)
        hs = hl + pc * h_in
        y_lru = hs * jax.nn.gelu(proj[rows, O_GR:O_GR + D_LRU])
        mixed[rows, Q_W:Q_W + D_LRU] = _rms_rows(y_lru, gmix_lru)
        h_in = p_end[c:c + 1, :] * h_in + h_end[c:c + 1, :]
    hst[...] = h_in
    hnew_ref[0] = h_in

    z = proj[:, O_CG:O_CG + D_SC] * proj[:, O_U:O_U + D_SC]
    zext[SUBLANES:, :] = z
    yc = z * scw_ref[SC_CONV - 1:SC_CONV, :]
    for kk in range(1, SC_CONV):
        yc = yc + zext[pl.ds(SUBLANES - kk, T), :] * scw_ref[SC_CONV - 1 - kk:SC_CONV - kk, :]
    y_sc = proj[:, O_BG:O_BG + D_SC] * yc
    mixed[:, Q_W + D_LRU:] = _rms_rows(y_sc, gmix_ref[:, Q_W + D_LRU:])
    scbuf_ref[0] = zext[pl.ds(T + SUBLANES - (SC_CONV - 1), SC_CONV - 1), :]
    zext[0:SUBLANES, :] = zext[T:T + SUBLANES, :]

    mixed[:, 0:Q_W] = _rms_rows(mixed[:, 0:Q_W], gmix_ref[:, 0:Q_W])
    m = _dot(mixed[...].astype(_BF), wout_ref[...])
    y_ref[0] = _ln_rows(ALPHA * x_ref[0] + m, ln_ref[0:1, :], ln_ref[1:2, :])


def _layer_spec(stacked, l):
    nd = stacked.ndim - 1
    return pl.BlockSpec((None,) + stacked.shape[1:], lambda *_: (l,) + (0,) * nd,
                        pipeline_mode=pl.Buffered(1))


def _mixer_prompt(l, x, sinks, *params):
    B, S, _ = x.shape
    T = T_PROMPT
    grid = (B, S // T)
    seq_out = lambda shape: pl.BlockSpec((1,) + shape, lambda b, s: (b, 0, 0))
    return pl.pallas_call(
        _mixer_prompt_kernel,
        grid=grid,
        in_specs=[
            pl.BlockSpec(memory_space=pltpu.SMEM),
            pl.BlockSpec((1, T, D_MODEL), lambda b, s: (b, s, 0)),
            *[_layer_spec(p, l) for p in params],
        ],
        out_specs=[
            pl.BlockSpec((1, T, D_MODEL), lambda b, s: (b, s, 0)),
            seq_out((WINDOW, KV_W)), seq_out((WINDOW, KV_W)), seq_out((1, D_LRU)),
            seq_out((LRU_CONV - 1, D_LRU)), seq_out((SC_CONV - 1, D_SC)),
        ],
        out_shape=[
            jax.ShapeDtypeStruct((B, S, D_MODEL), _F32),
            jax.ShapeDtypeStruct((B, WINDOW, KV_W), _F32),
            jax.ShapeDtypeStruct((B, WINDOW, KV_W), _F32),
            jax.ShapeDtypeStruct((B, 1, D_LRU), _F32),
            jax.ShapeDtypeStruct((B, LRU_CONV - 1, D_LRU), _F32),
            jax.ShapeDtypeStruct((B, SC_CONV - 1, D_SC), _F32),
        ],
        scratch_shapes=[
            pltpu.VMEM((T, IN_W), _F32),
            pltpu.VMEM((T // WINDOW, N_QTILES * WINDOW, LANES), _BF),
            pltpu.VMEM((T + WINDOW, LANES), _BF),
            pltpu.VMEM((T + WINDOW, LANES), _BF),
            pltpu.VMEM((T + WINDOW, LANES), _BF),
            pltpu.VMEM((T + SUBLANES, D_LRU), _F32),
            pltpu.VMEM((T + SUBLANES, D_SC), _F32),
            pltpu.VMEM((1, D_LRU), _F32),
            pltpu.VMEM((D_LRU // LANES, T + SUBLANES * SUBLANES, LANES), _F32),
            pltpu.VMEM((D_LRU // LANES, T + SUBLANES * SUBLANES, LANES), _F32),
            pltpu.VMEM((D_LRU // LANES, T + SUBLANES * SUBLANES, LANES), _F32),
            pltpu.VMEM((D_LRU // LANES, T + SUBLANES * SUBLANES, LANES), _F32),
            pltpu.VMEM((2, N_HEADS, WINDOW, 2 * WINDOW), _F32),
            pltpu.VMEM((T, D_MODEL), _F32),
        ],
        compiler_params=pltpu.CompilerParams(
            dimension_semantics=("arbitrary", "arbitrary"), vmem_limit_bytes=VMEM_LIMIT),
        name="mixer_prompt",
    )(sinks, x, *params)


def _mixer_sample_kernel(sinks_ref, x_ref, ck_ref, cv_ref, h0_ref, lbuf_ref, sbuf_ref,
                         win_ref, wlru_ref, wout_ref, cw_ref, v256_ref, scw_ref, gmix_ref, ln_ref,
                         y_ref, knew_ref, vnew_ref, hnew_ref, lrubuf_ref, scbuf_ref,
                         kx2, vx2, qs2, os2):
    TS = x_ref.shape[0]
    BT = BT_SAMPLE
    NQ = TS * N_QTILES
    lane = lax.broadcasted_iota(jnp.int32, (1, LANES), 1)
    lo_lane = lane < HEAD_DIM

    x = x_ref[...].reshape(TS * BT, D_MODEL)
    proj = _dot(x.astype(_BF), win_ref[...])

    for bb in range(BT):
        kx2[bb * KPAD:bb * KPAD + WINDOW, :] = ck_ref[bb]
        vx2[bb * KPAD:bb * KPAD + WINDOW, :] = cv_ref[bb]
    for t in range(SUBLANES):
        if t < TS:
            rows = slice(t * BT, (t + 1) * BT)
            kx2[pl.ds(WINDOW + t, BT, stride=KPAD), :] = proj[rows, O_K:O_K + KV_W]
            vx2[pl.ds(WINDOW + t, BT, stride=KPAD), :] = proj[rows, O_V:O_V + KV_W]
            for j in range(N_QTILES):
                qs2[pl.ds(t * N_QTILES + j, BT, stride=NQ), :] = (
                    proj[rows, j * LANES:(j + 1) * LANES] * (HEAD_DIM ** -0.5))
        else:
            kx2[pl.ds(WINDOW + t, BT, stride=KPAD), :] = jnp.zeros((BT, LANES), _F32)
            vx2[pl.ds(WINDOW + t, BT, stride=KPAD), :] = jnp.zeros((BT, LANES), _F32)
    for bb in range(BT):
        knew_ref[bb] = kx2[bb * KPAD + TS:bb * KPAD + TS + WINDOW, :]
        vnew_ref[bb] = vx2[bb * KPAD + TS:bb * KPAD + TS + WINDOW, :]

    kall = kx2[...].reshape(BT, KPAD, LANES)
    qs = qs2[...].reshape(BT, NQ, LANES).astype(_BF)
    k_lo = jnp.where(lo_lane, kall, 0.0).astype(_BF)
    k_hi = jnp.where(lo_lane, 0.0, kall).astype(_BF)
    vv = vx2[...].reshape(BT, KPAD, LANES).astype(_BF)
    s_lo = jnp.einsum('bqd,bkd->bqk', qs, k_lo, preferred_element_type=_F32)
    s_hi = jnp.einsum('bqd,bkd->bqk', qs, k_hi, preferred_element_type=_F32)
    ri = lax.broadcasted_iota(jnp.int32, (NQ, KPAD), 0)
    ci = lax.broadcasted_iota(jnp.int32, (NQ, KPAD), 1)
    tq = lax.shift_right_logical(ri, int(math.log2(N_QTILES)))
    jq = ri & (N_QTILES - 1)
    dist = WINDOW + tq - ci
    valid = (dist >= 0) & (dist < WINDOW)
    distf = dist.astype(_F32)
    rj = lax.broadcasted_iota(jnp.int32, (NQ, 1), 0) & (N_QTILES - 1)
    slope_lo = jnp.zeros((NQ, KPAD), _F32)
    slope_hi = jnp.zeros((NQ, KPAD), _F32)
    sink_lo = jnp.zeros((NQ, 1), _F32)
    sink_hi = jnp.zeros((NQ, 1), _F32)
    slopes = _slopes()
    for j in range(N_QTILES):
        slope_lo = jnp.where(jq == j, slopes[j], slope_lo)
        slope_hi = jnp.where(jq == j, slopes[GQ + j], slope_hi)
        sink_lo = jnp.where(rj == j, sinks_ref[j], sink_lo)
        sink_hi = jnp.where(rj == j, sinks_ref[GQ + j], sink_hi)
    bias_lo = jnp.where(valid, -slope_lo * distf, NEG_INF)
    bias_hi = jnp.where(valid, -slope_hi * distf, NEG_INF)
    p_lo, inv_lo = _sink_softmax(s_lo + bias_lo[None], sink_lo[None])
    p_hi, inv_hi = _sink_softmax(s_hi + bias_hi[None], sink_hi[None])
    o_lo = jnp.einsum('bqk,bkd->bqd', p_lo.astype(_BF), vv, preferred_element_type=_F32)
    o_hi = jnp.einsum('bqk,bkd->bqd', p_hi.astype(_BF), vv, preferred_element_type=_F32)
    o = jnp.where(lo_lane[None], o_lo * inv_lo, o_hi * inv_hi)
    os2[...] = o.reshape(BT * NQ, LANES)

    lbuf = lbuf_ref[...]
    sbuf = sbuf_ref[...]
    xr_hist = [lbuf[:, i * D_LRU:(i + 1) * D_LRU] for i in range(LRU_CONV - 1)]
    z_hist = [sbuf[:, i * D_SC:(i + 1) * D_SC] for i in range(SC_CONV - 1)]
    h = h0_ref[...]
    gmix = gmix_ref[...]
    mixed_rows = []
    for t in range(TS):
        rows = slice(t * BT, (t + 1) * BT)
        o_att = jnp.concatenate(
            [os2[pl.ds(t * N_QTILES + j, BT, stride=NQ), :] for j in range(N_QTILES)], axis=-1)
        xr_hist.append(proj[rows, O_XR:O_XR + D_LRU])
        xc = sum(xr_hist[t + kk] * cw_ref[kk:kk + 1, :] for kk in range(LRU_CONV))
        xcb = xc + v256_ref[0:1, :]
        a, b = _lru_coeffs(xcb, wlru_ref[...], v256_ref[1:2, :], v256_ref[2:3, :], v256_ref[3:4, :])
        h = a * h + b
        y_lru = h * jax.nn.gelu(proj[rows, O_GR:O_GR + D_LRU])
        z_hist.append(proj[rows, O_CG:O_CG + D_SC] * proj[rows, O_U:O_U + D_SC])
        yc = sum(z_hist[t + kk] * scw_ref[kk:kk + 1, :] for kk in range(SC_CONV))
        y_sc = proj[rows, O_BG:O_BG + D_SC] * yc
        mixed_rows.append(jnp.concatenate(
            [_rms_rows(o_att, gmix[:, 0:Q_W]),
             _rms_rows(y_lru, gmix[:, Q_W:Q_W + D_LRU]),
             _rms_rows(y_sc, gmix[:, Q_W + D_LRU:])], axis=-1))
    hnew_ref[...] = h
    lrubuf_ref[...] = jnp.concatenate(xr_hist[TS:], axis=-1)
    scbuf_ref[...] = jnp.concatenate(z_hist[TS:], axis=-1)
    mixed = jnp.concatenate(mixed_rows, axis=0)
    m = _dot(mixed.astype(_BF), wout_ref[...])
    y = _ln_rows(ALPHA * x + m, ln_ref[0:1, :], ln_ref[1:2, :])
    y_ref[...] = y.reshape(TS, BT, D_MODEL)


def _mixer_sample(l, x, ck, cv, h0, lbuf, sbuf, sinks, *params):
    TS, B, _ = x.shape
    BT = BT_SAMPLE
    rows2 = lambda w: pl.BlockSpec((BT, w), lambda i: (i, 0))
    rows2_l = lambda w: pl.BlockSpec((None, BT, w), lambda i: (l, i, 0))
    cache = pl.BlockSpec((BT, WINDOW, KV_W), lambda i: (i, 0, 0))
    cache_l = pl.BlockSpec((None, BT, WINDOW, KV_W), lambda i: (l, i, 0, 0))
    xspec = pl.BlockSpec((TS, BT, D_MODEL), lambda i: (0, i, 0))
    return pl.pallas_call(
        _mixer_sample_kernel,
        grid=(B // BT,),
        in_specs=[
            pl.BlockSpec(memory_space=pltpu.SMEM),
            xspec, cache_l, cache_l, rows2_l(D_LRU), rows2_l((LRU_CONV - 1) * D_LRU),
            rows2_l((SC_CONV - 1) * D_SC),
            *[_layer_spec(p, l) for p in params],
        ],
        out_specs=[xspec, cache, cache, rows2(D_LRU), rows2((LRU_CONV - 1) * D_LRU),
                   rows2((SC_CONV - 1) * D_SC)],
        out_shape=[
            jax.ShapeDtypeStruct((TS, B, D_MODEL), _F32),
            jax.ShapeDtypeStruct((B, WINDOW, KV_W), _F32),
            jax.ShapeDtypeStruct((B, WINDOW, KV_W), _F32),
            jax.ShapeDtypeStruct((B, D_LRU), _F32),
            jax.ShapeDtypeStruct((B, (LRU_CONV - 1) * D_LRU), _F32),
            jax.ShapeDtypeStruct((B, (SC_CONV - 1) * D_SC), _F32),
        ],
        scratch_shapes=[
            pltpu.VMEM((BT * KPAD, LANES), _F32),
            pltpu.VMEM((BT * KPAD, LANES), _F32),
            pltpu.VMEM((BT * TS * N_QTILES, LANES), _F32),
            pltpu.VMEM((BT * TS * N_QTILES, LANES), _F32),
        ],
        compiler_params=pltpu.CompilerParams(
            dimension_semantics=("arbitrary",), vmem_limit_bytes=VMEM_LIMIT),
        name="mixer_sample",
    )(sinks, x, ck, cv, h0, lbuf, sbuf, *params)


def _route(logits):
    lane = lax.broadcasted_iota(jnp.int32, logits.shape, 1)
    big = jnp.int32(ROUTE_W)
    is_grp = lane < N_GROUPS
    glog = jnp.where(is_grp, logits, -jnp.inf)
    gmax = jnp.max(glog, axis=-1, keepdims=True)
    gsel = jnp.min(jnp.where(glog == gmax, lane, big), axis=-1, keepdims=True)
    gw = 1.0 / jnp.sum(jnp.exp(glog - gmax), axis=-1, keepdims=True)
    e_lo = N_GROUPS + gsel * EXPERTS_PER_GROUP
    in_grp = (lane >= e_lo) & (lane < e_lo + EXPERTS_PER_GROUP)
    elog = jnp.where(in_grp, logits, -jnp.inf)
    v1 = jnp.max(elog, axis=-1, keepdims=True)
    i1 = jnp.min(jnp.where(elog == v1, lane, big), axis=-1, keepdims=True)
    elog2 = jnp.where(lane == i1, -jnp.inf, elog)
    v2 = jnp.max(elog2, axis=-1, keepdims=True)
    i2 = jnp.min(jnp.where(elog2 == v2, lane, big), axis=-1, keepdims=True)
    e2 = jnp.exp(v2 - v1)
    w1 = gw / (1.0 + e2)
    w2 = gw * e2 / (1.0 + e2)
    return jnp.where(lane == i1, w1, jnp.where(lane == i2, w2, 0.0))


def _group_experts(xb, gates, g, wg, wu, wd, hsc):
    lane = lax.broadcasted_iota(jnp.int32, (1, ROUTE_W), 1)
    for e in range(EXPERTS_PER_GROUP):
        hg = _dot(xb, wg[e])
        hu = _dot(xb, wu[e])
        gate = jnp.sum(jnp.where(lane == N_GROUPS + g * EXPERTS_PER_GROUP + e, gates, 0.0),
                       axis=-1, keepdims=True)
        hsc[:, e * D_EXPERT:(e + 1) * D_EXPERT] = (jax.nn.silu(hg) * hu * gate).astype(_BF)
    return _dot(hsc[...], wd)


def _moe_kernel(x_ref, wr_ref, br_ref, wg_ref, wu_ref, wd_ref, ln_ref, y_ref, xb_s, gates_s, hsc, acc):
    g = pl.program_id(0)

    @pl.when(g == 0)
    def _():
        xb = x_ref[...].astype(_BF)
        xb_s[...] = xb
        gates_s[...] = _route(_dot(xb, wr_ref[...]) + br_ref[...])
        acc[...] = jnp.zeros(acc.shape, _F32)

    wd = wd_ref[...].astype(_BF).reshape(EXPERTS_PER_GROUP * D_EXPERT, D_MODEL)
    acc[...] += _group_experts(xb_s[...], gates_s[...], g, wg_ref[...].astype(_BF),
                               wu_ref[...].astype(_BF), wd, hsc)

    @pl.when(g == N_GROUPS - 1)
    def _():
        y_ref[...] = _ln_rows(ALPHA * x_ref[...] + acc[...], ln_ref[0:1, :], ln_ref[1:2, :])


def _group_weight_spec(w, index_map):
    return pl.BlockSpec((None, EXPERTS_PER_GROUP) + w.shape[2:], index_map)


def _moe(l, x, wr, br, wg, wu, wd, ln):
    N = x.shape[0]
    tok = pl.BlockSpec((N, D_MODEL), lambda g: (0, 0))
    gw = lambda w: _group_weight_spec(w, lambda g: (l, g, 0, 0))
    return pl.pallas_call(
        _moe_kernel,
        grid=(N_GROUPS,),
        in_specs=[tok, _layer_spec(wr, l), _layer_spec(br, l), gw(wg), gw(wu), gw(wd), _layer_spec(ln, l)],
        out_specs=tok,
        out_shape=jax.ShapeDtypeStruct((N, D_MODEL), _F32),
        scratch_shapes=[
            pltpu.VMEM((N, D_MODEL), _BF),
            pltpu.VMEM((N, ROUTE_W), _F32),
            pltpu.VMEM((N, EXPERTS_PER_GROUP * D_EXPERT), _BF),
            pltpu.VMEM((N, D_MODEL), _F32),
        ],
        compiler_params=pltpu.CompilerParams(
            dimension_semantics=("arbitrary",), vmem_limit_bytes=VMEM_LIMIT),
        name="moe",
    )(x, wr, br, wg, wu, wd, ln)


CODE_ROWS = SUBLANES


def _route_kernel(x_ref, wr_ref, br_ref, code_ref, cnt_ref, utri, cnt_s):
    T = T_DISPATCH
    i = pl.program_id(0)

    @pl.when(i == 0)
    def _():
        cnt_s[...] = jnp.zeros((SUBLANES, T), _F32)
        r = lax.broadcasted_iota(jnp.int32, (T, T), 0)
        c = lax.broadcasted_iota(jnp.int32, (T, T), 1)
        utri[...] = jnp.where(r < c, 1.0, 0.0).astype(_BF)

    logits = _dot(x_ref[...].astype(_BF), wr_ref[...]) + br_ref[...]
    lt = logits.T[0:SUBLANES]
    g = [lt[k:k + 1] for k in range(N_GROUPS)]
    gmax = functools.reduce(jnp.maximum, g)
    gsel = jnp.full((1, T), N_GROUPS - 1, jnp.int32)
    for k in range(N_GROUPS - 2, -1, -1):
        gsel = jnp.where(g[k] == gmax, k, gsel)
    row = lax.broadcasted_iota(jnp.int32, (SUBLANES, T), 0)
    onehot = row == gsel
    onehot_f = jnp.where(onehot, 1.0, 0.0)
    rank = _dot(onehot_f.astype(_BF), utri[...])
    cnt = cnt_s[...]
    loc = jnp.sum(jnp.where(onehot, cnt + rank, 0.0), axis=0, keepdims=True).astype(jnp.int32)
    cnt_new = cnt + jnp.sum(onehot_f, axis=1, keepdims=True)
    cnt_s[...] = cnt_new
    cnt_ref[...] = cnt_new[:, 0:LANES].astype(jnp.int32)
    code_ref[0] = jnp.where(row == 0, loc, jnp.where(row == 1, gsel, 0))


def _route_tokens(l, x, wr, br):
    N = x.shape[0]
    T = T_DISPATCH
    return pl.pallas_call(
        _route_kernel,
        grid=(N // T,),
        in_specs=[pl.BlockSpec((T, D_MODEL), lambda i: (i, 0)),
                  _layer_spec(wr, l), _layer_spec(br, l)],
        out_specs=[pl.BlockSpec((1, CODE_ROWS, T), lambda i: (i, 0, 0)),
                   pl.BlockSpec((SUBLANES, LANES), lambda i: (0, 0))],
        out_shape=[jax.ShapeDtypeStruct((N // T, CODE_ROWS, T), jnp.int32),
                   jax.ShapeDtypeStruct((SUBLANES, LANES), jnp.int32)],
        scratch_shapes=[
            pltpu.VMEM((T, T), _BF),
            pltpu.VMEM((SUBLANES, T), _F32),
        ],
        compiler_params=pltpu.CompilerParams(
            dimension_semantics=("arbitrary",), vmem_limit_bytes=VMEM_LIMIT),
        name="moe_route",
    )(x, wr, br)


def _to_token_rows(x, rows_ref):
    R = x.shape[0]
    for c in range(TOK_ROWS):
        rows_ref[pl.ds(c, R, stride=TOK_ROWS), :] = x[:, c * LANES:(c + 1) * LANES]


def _from_token_rows(rows_ref, R):
    return jnp.concatenate(
        [rows_ref[pl.ds(c, R, stride=TOK_ROWS), :] for c in range(TOK_ROWS)], axis=-1)


def _token_copy(src_ref, s, dst_ref, d, sem):
    return pltpu.make_async_copy(
        src_ref.at[pl.ds(pl.multiple_of(s * TOK_ROWS, TOK_ROWS), TOK_ROWS)],
        dst_ref.at[pl.ds(pl.multiple_of(d * TOK_ROWS, TOK_ROWS), TOK_ROWS)], sem)


def _load_dest(dest_ref, dest_sm, sem_idx):
    copies = [pltpu.make_async_copy(dest_ref.at[j, 0], dest_sm.at[pl.ds(j * T_DISPATCH, T_DISPATCH)], sem_idx)
              for j in range(T_PERM // T_DISPATCH)]
    for cp in copies:
        cp.start()
    for cp in copies:
        cp.wait()


def _scatter_kernel(off_ref, code_ref, x_ref, xs_ref, dest_ref, xrow, dest_sm, sem_idx, sem_rows):
    code = code_ref[...]
    loc = code[:, 0:1]
    grp = code[:, 1:2]
    off = jnp.zeros(loc.shape, jnp.int32)
    for g in range(N_GROUPS):
        off = jnp.where(grp == g, off_ref[g], off)
    dest_ref[...] = off + loc
    _to_token_rows(x_ref[...], xrow)
    _load_dest(dest_ref, dest_sm, sem_idx)

    def start(i, c):
        for k in range(DMA_UNROLL):
            t = i * DMA_UNROLL + k
            _token_copy(xrow, t, xs_ref, dest_sm[t], sem_rows).start(priority=k % 2)
        return c

    lax.fori_loop(0, T_PERM // DMA_UNROLL, start, 0)

    def wait(t, c):
        _token_copy(xrow, 0, xs_ref, 0, sem_rows).wait()
        return c

    lax.fori_loop(0, T_PERM, wait, 0, unroll=DMA_UNROLL)


def _scatter(off, code, x):
    N = x.shape[0]
    T = T_PERM
    nt = T // T_DISPATCH
    return pl.pallas_call(
        _scatter_kernel,
        grid=(N // T,),
        in_specs=[pl.BlockSpec(memory_space=pltpu.SMEM),
                  pl.BlockSpec((nt, CODE_ROWS, T_DISPATCH), lambda i: (i, 0, 0)),
                  pl.BlockSpec((T, D_MODEL), lambda i: (i, 0))],
        out_specs=[pl.BlockSpec(memory_space=pl.ANY),
                   pl.BlockSpec((nt, 1, T_DISPATCH), lambda i: (i, 0, 0))],
        out_shape=[jax.ShapeDtypeStruct((N * TOK_ROWS, LANES), _F32),
                   jax.ShapeDtypeStruct((N // T_DISPATCH, 1, T_DISPATCH), jnp.int32)],
        scratch_shapes=[pltpu.VMEM((T * TOK_ROWS, LANES), _F32),
                        pltpu.SMEM((T,), jnp.int32), pltpu.SemaphoreType.DMA, pltpu.SemaphoreType.DMA],
        compiler_params=pltpu.CompilerParams(
            dimension_semantics=("arbitrary",), vmem_limit_bytes=VMEM_LIMIT),
        name="moe_scatter",
    )(off, code, x)


def _route_in_group(logits, g):
    lane = lax.broadcasted_iota(jnp.int32, logits.shape, 1)
    big = jnp.int32(ROUTE_W)
    glog = jnp.where(lane < N_GROUPS, logits, -jnp.inf)
    gmax = jnp.max(glog, axis=-1, keepdims=True)
    lg = jnp.sum(jnp.where(lane == g, logits, 0.0), axis=-1, keepdims=True)
    gw = jnp.exp(lg - gmax) / jnp.sum(jnp.exp(glog - gmax), axis=-1, keepdims=True)
    e_lo = N_GROUPS + g * EXPERTS_PER_GROUP
    in_grp = (lane >= e_lo) & (lane < e_lo + EXPERTS_PER_GROUP)
    elog = jnp.where(in_grp, logits, -jnp.inf)
    v1 = jnp.max(elog, axis=-1, keepdims=True)
    i1 = jnp.min(jnp.where(elog == v1, lane, big), axis=-1, keepdims=True)
    elog2 = jnp.where(lane == i1, -jnp.inf, elog)
    v2 = jnp.max(elog2, axis=-1, keepdims=True)
    i2 = jnp.min(jnp.where(elog2 == v2, lane, big), axis=-1, keepdims=True)
    e2 = jnp.exp(v2 - v1)
    w1 = gw / (1.0 + e2)
    w2 = gw * e2 / (1.0 + e2)
    return jnp.where(lane == i1, w1, jnp.where(lane == i2, w2, 0.0))


def _moe_grouped_kernel(bidx_ref, bgrp_ref, blo_ref, bhi_ref, bfirst_ref, gfirst_ref, nstep_ref,
                        x_ref, wr_ref, br_ref, wg_ref, wu_ref, wd_ref, ln_ref, y_ref,
                        wg_b, wu_b, wd_b, hsc):
    i = pl.program_id(0)

    @pl.when(gfirst_ref[i] == 1)
    def _():
        wg_b[...] = wg_ref[...].astype(_BF)
        wu_b[...] = wu_ref[...].astype(_BF)
        wd_b[...] = wd_ref[...].astype(_BF).reshape(wd_b.shape)

    @pl.when(i < nstep_ref[0])
    def _():
        g = bgrp_ref[i]
        x = _from_token_rows(x_ref, BM)
        xb = x.astype(_BF)
        gates = _route_in_group(_dot(xb, wr_ref[...]) + br_ref[...], g)
        y = _group_experts(xb, gates, g, wg_b, wu_b, wd_b[...], hsc)
        y_new = _ln_rows(ALPHA * x + y, ln_ref[0:1, :], ln_ref[1:2, :])

        @pl.when(bfirst_ref[i] == 1)
        def _():
            _to_token_rows(y_new, y_ref)

        @pl.when(bfirst_ref[i] == 0)
        def _():
            row = lax.broadcasted_iota(jnp.int32, (BM, 1), 0)
            mine = (row >= blo_ref[i]) & (row < bhi_ref[i])
            _to_token_rows(jnp.where(mine, y_new, _from_token_rows(y_ref, BM)), y_ref)


def _moe_grouped(l, tables, xs, wr, br, wg, wu, wd, ln):
    nsteps = tables[0].shape[0]
    tok = pl.BlockSpec((BM * TOK_ROWS, LANES), lambda i, bidx, *_: (bidx[i], 0))
    gw = lambda w: _group_weight_spec(w, lambda i, bidx, bgrp, *_: (l, bgrp[i], 0, 0))
    return pl.pallas_call(
        _moe_grouped_kernel,
        grid_spec=pltpu.PrefetchScalarGridSpec(
            num_scalar_prefetch=len(tables),
            grid=(nsteps,),
            in_specs=[tok, _layer_spec(wr, l), _layer_spec(br, l), gw(wg), gw(wu), gw(wd),
                      _layer_spec(ln, l)],
            out_specs=tok,
            scratch_shapes=[
                pltpu.VMEM((EXPERTS_PER_GROUP, D_MODEL, D_EXPERT), _BF),
                pltpu.VMEM((EXPERTS_PER_GROUP, D_MODEL, D_EXPERT), _BF),
                pltpu.VMEM((EXPERTS_PER_GROUP * D_EXPERT, D_MODEL), _BF),
                pltpu.VMEM((BM, EXPERTS_PER_GROUP * D_EXPERT), _BF),
            ],
        ),
        out_shape=jax.ShapeDtypeStruct(xs.shape, _F32),
        compiler_params=pltpu.CompilerParams(
            dimension_semantics=("arbitrary",), vmem_limit_bytes=VMEM_LIMIT),
        name="moe_grouped",
    )(*tables, xs, wr, br, wg, wu, wd, ln)


def _combine_kernel(dest_ref, ys_ref, out_ref, yrow, dest_sm, sem_idx, sem_rows):
    _load_dest(dest_ref, dest_sm, sem_idx)

    def start(i, c):
        for k in range(DMA_UNROLL):
            t = i * DMA_UNROLL + k
            _token_copy(ys_ref, dest_sm[t], yrow, t, sem_rows).start(priority=k % 2)
        return c

    lax.fori_loop(0, T_PERM // DMA_UNROLL, start, 0)

    def wait(t, c):
        _token_copy(ys_ref, 0, yrow, 0, sem_rows).wait()
        return c

    lax.fori_loop(0, T_PERM, wait, 0, unroll=DMA_UNROLL)
    out_ref[...] = _from_token_rows(yrow, T_PERM)


def _combine(dest, ys):
    N = dest.shape[0] * T_DISPATCH
    T = T_PERM
    nt = T // T_DISPATCH
    return pl.pallas_call(
        _combine_kernel,
        grid=(N // T,),
        in_specs=[pl.BlockSpec((nt, 1, T_DISPATCH), lambda i: (i, 0, 0)),
                  pl.BlockSpec(memory_space=pl.ANY)],
        out_specs=pl.BlockSpec((T, D_MODEL), lambda i: (i, 0)),
        out_shape=jax.ShapeDtypeStruct((N, D_MODEL), _F32),
        scratch_shapes=[pltpu.VMEM((T * TOK_ROWS, LANES), _F32),
                        pltpu.SMEM((T,), jnp.int32), pltpu.SemaphoreType.DMA, pltpu.SemaphoreType.DMA],
        compiler_params=pltpu.CompilerParams(
            dimension_semantics=("arbitrary",), vmem_limit_bytes=VMEM_LIMIT),
        name="moe_combine",
    )(dest, ys)


def _step_tables(cnt, nsteps):
    n = cnt[:N_GROUPS, 0]
    off = jnp.cumsum(n) - n
    b_lo = off // BM
    nb = jnp.where(n > 0, (off + n - 1) // BM - b_lo + 1, 0)
    ends = jnp.cumsum(nb)
    total = ends[-1]
    step = jnp.minimum(jnp.arange(nsteps, dtype=jnp.int32), total - 1)
    grp = jnp.sum(step[:, None] >= ends[None, :], axis=1).astype(jnp.int32)
    bidx = b_lo[grp] + step - (ends - nb)[grp]
    lo = jnp.clip(off[grp] - bidx * BM, 0, BM)
    hi = jnp.clip(off[grp] + n[grp] - bidx * BM, 0, BM)
    one = jnp.ones((1,), jnp.int32)
    first = jnp.concatenate([one, (bidx[1:] != bidx[:-1]).astype(jnp.int32)])
    gfirst = jnp.concatenate([one, (grp[1:] != grp[:-1]).astype(jnp.int32)])
    tables = tuple(a.astype(jnp.int32) for a in (bidx, grp, lo, hi, first, gfirst, total[None]))
    return off.astype(jnp.int32), tables


def _moe_sorted(l, x, wr, br, wg, wu, wd, ln):
    N = x.shape[0]
    code, cnt = _route_tokens(l, x, wr, br)
    off, tables = _step_tables(cnt, N // BM + N_GROUPS - 1)
    xs, dest = _scatter(off, code, x)
    ys = _moe_grouped(l, tables, xs, wr, br, wg, wu, wd, ln)
    return _combine(dest, ys)


def _q_perm(w, axis):
    shape = w.shape
    w = w.reshape(shape[:axis] + (N_KV_HEADS, GQ, HEAD_DIM) + shape[axis + 1:])
    return jnp.swapaxes(w, axis, axis + 1).reshape(shape)


def _block_diag(w):
    eye = jnp.eye(LRU_BLOCKS, dtype=w.dtype)
    return jnp.einsum('nij,nm->nimj', w, eye).reshape(D_LRU, D_LRU)


def kernel(x_prompt, x_sample, cache_k, cache_v, state_lru_h, state_lru_conv, state_sc_conv, w_in, attn_sinks, lru_conv_w, lru_conv_b, lru_w_a, lru_b_a, lru_w_x, lru_b_x, lru_lambda, sc_conv_w, g_mix, w_out, ln1_g, ln1_b, w_grp, b_grp, w_route, b_route, w_gate, w_up, w_down, ln2_g, ln2_b):
    B, S, _ = x_prompt.shape
    DB, TS, _ = x_sample.shape
    assert S % T_PROMPT == 0 and (B * S) % T_PERM == 0 and (B * S) % BM == 0 and DB % BT_SAMPLE == 0
    assert T_PERM % T_DISPATCH == 0
    assert B * S < (1 << 24)
    assert TS <= SUBLANES and cache_k.shape[2] == WINDOW

    win = jnp.concatenate([_q_perm(w_in[:, :, :Q_W], 2), w_in[:, :, Q_W:]], axis=2).astype(_BF)
    wout = jnp.concatenate([_q_perm(w_out[:, :Q_W], 1), w_out[:, Q_W:]], axis=1).astype(_BF)
    gmix = jnp.concatenate([_q_perm(g_mix[:, :Q_W], 1), g_mix[:, Q_W:]], axis=1)[:, None, :]
    wlru = jnp.concatenate([jax.vmap(_block_diag)(lru_w_a), jax.vmap(_block_diag)(lru_w_x)], axis=2).astype(_BF)
    v256 = jnp.stack([lru_conv_b, lru_b_a, lru_b_x, lru_lambda], axis=1)
    ln1 = jnp.stack([ln1_g, ln1_b], axis=1)
    ln2 = jnp.stack([ln2_g, ln2_b], axis=1)
    pad = ROUTE_W - N_GROUPS - N_EXPERTS
    wr = jnp.pad(jnp.concatenate([w_grp, w_route], axis=2), ((0, 0), (0, 0), (0, pad))).astype(_BF)
    br = jnp.pad(jnp.concatenate([b_grp, b_route], axis=1), ((0, 0), (0, pad)))[:, None, :]
    mix_w = (win, wlru, wout, lru_conv_w, v256, sc_conv_w, gmix, ln1)
    moe_w = (wr, br, w_gate, w_up, w_down, ln2)
    ck = cache_k.reshape(DEPTH, DB, WINDOW, KV_W)
    cv = cache_v.reshape(DEPTH, DB, WINDOW, KV_W)
    lbuf = state_lru_conv.reshape(DEPTH, DB, -1)
    sbuf = state_sc_conv.reshape(DEPTH, DB, -1)

    yp = x_prompt
    ys = jnp.swapaxes(x_sample, 0, 1)
    outs_p = [[] for _ in range(5)]
    outs_s = [[] for _ in range(5)]
    for l in range(DEPTH):
        yp, kp, vp, hp, lbp, sbp = _mixer_prompt(l, yp, attn_sinks[l], *mix_w)
        yp = _moe_sorted(l, yp.reshape(B * S, D_MODEL), *moe_w).reshape(B, S, D_MODEL)
        ys, ks, vs, hs, lbs, sbs = _mixer_sample(l, ys, ck, cv, state_lru_h, lbuf, sbuf, attn_sinks[l], *mix_w)
        ys = _moe(l, ys.reshape(TS * DB, D_MODEL), *moe_w).reshape(TS, DB, D_MODEL)
        for lst, arr in zip(outs_p, (kp.reshape(B, WINDOW, N_KV_HEADS, HEAD_DIM),
                                     vp.reshape(B, WINDOW, N_KV_HEADS, HEAD_DIM),
                                     hp.reshape(B, D_LRU), lbp, sbp)):
            lst.append(arr)
        for lst, arr in zip(outs_s, (ks.reshape(DB, WINDOW, N_KV_HEADS, HEAD_DIM),
                                     vs.reshape(DB, WINDOW, N_KV_HEADS, HEAD_DIM),
                                     hs, lbs.reshape(DB, LRU_CONV - 1, D_LRU),
                                     sbs.reshape(DB, SC_CONV - 1, D_SC))):
            lst.append(arr)
    return (yp, jnp.swapaxes(ys, 0, 1),
            *[jnp.stack(o) for o in outs_p], *[jnp.stack(o) for o in outs_s])
```

```python
import functools
import math

import jax
import jax.numpy as jnp
from jax import lax
from jax.experimental import pallas as pl
from jax.experimental.pallas import tpu as pltpu

D_MODEL = 1024
DEPTH = 2
HEAD_DIM = 64
N_HEADS = 8
N_KV_HEADS = 2
GQ = N_HEADS // N_KV_HEADS
WINDOW = 128
ALIBI_MAX = 8.0
D_LRU = 256
LRU_BLOCKS = 4
LRU_BLOCK_W = D_LRU // LRU_BLOCKS
LRU_CONV = 4
LRU_C = 8.0
D_SC = 256
SC_CONV = 3
Q_W = N_HEADS * HEAD_DIM
KV_W = N_KV_HEADS * HEAD_DIM
IN_W = Q_W + 2 * KV_W + 2 * D_LRU + 3 * D_SC
N_GROUPS = 4
EXPERTS_PER_GROUP = 4
N_EXPERTS = N_GROUPS * EXPERTS_PER_GROUP
D_EXPERT = 256
ALPHA = (2 * DEPTH) ** 0.25
LN_EPS = 1e-5
RMS_EPS = 1e-6
NEG_INF = -1e30

O_K = Q_W
O_V = O_K + KV_W
O_XR = O_V + KV_W
O_GR = O_XR + D_LRU
O_U = O_GR + D_LRU
O_BG = O_U + D_SC
O_CG = O_BG + D_SC

LANES = 128
SUBLANES = 8
N_QTILES = Q_W // LANES
ROUTE_W = LANES
VMEM_LIMIT = 56 * 1024 * 1024

T_PROMPT = 1024
SEQ_PER_STEP = 1
T_DISPATCH = 512
T_PERM = 2048
TOK_ROWS = D_MODEL // LANES
DMA_UNROLL = 8
BM = 512
BT_SAMPLE = 32
KPAD = WINDOW + SUBLANES

_BF = jnp.bfloat16
_F32 = jnp.float32


def _slopes():
    return [2.0 ** (-ALIBI_MAX * h / N_HEADS) for h in range(1, N_HEADS + 1)]


def _dot(a, b):
    return jnp.dot(a, b, preferred_element_type=_F32)


def _dot_t(a, b):
    return lax.dot_general(a, b, (((1,), (1,)), ((), ())), preferred_element_type=_F32)


def _rms_rows(y, g):
    return y * lax.rsqrt(jnp.mean(y * y, axis=-1, keepdims=True) + RMS_EPS) * g


def _ln_rows(x, g, b):
    mu = jnp.mean(x, axis=-1, keepdims=True)
    xc = x - mu
    var = jnp.mean(xc * xc, axis=-1, keepdims=True)
    return xc * lax.rsqrt(var + LN_EPS) * g + b


def _lru_coeffs(xcb, wlru, b_a, b_x, lam):
    gates = _dot(xcb.astype(_BF), wlru)
    r = jax.nn.sigmoid(gates[:, :D_LRU] + b_a)
    i = jax.nn.sigmoid(gates[:, D_LRU:] + b_x)
    log_a = (-LRU_C * jax.nn.softplus(-lam)) * r
    a = jnp.exp(log_a)
    b = jnp.sqrt(jnp.tanh(-log_a) * (1.0 + a * a)) * (i * xcb)
    return a, b


def _sink_softmax(s, sink):
    m = jnp.maximum(jnp.max(s, axis=-1, keepdims=True), sink)
    p = jnp.exp(s - m)
    denom = jnp.sum(p, axis=-1, keepdims=True) + jnp.exp(sink - m)
    return p, 1.0 / denom


def _mixer_prompt_kernel(sinks_ref, x_ref, win_ref, wlru_ref, wout_ref, cw_ref, v256_ref, scw_ref,
                         gmix_ref, ln_ref,
                         y_ref, knew_ref, vnew_ref, hnew_ref, lrubuf_ref, scbuf_ref,
                         proj, qsc, klo, khi, vext, xrext, zext, hst, a_s, b_s, hl_s, pc_s, bias_s, mixed):
    s_idx = pl.program_id(1)

    @pl.when(s_idx == 0)
    def _():
        klo[:, 0:WINDOW, :] = jnp.zeros((SEQ_PER_STEP, WINDOW, LANES), _BF)
        khi[:, 0:WINDOW, :] = jnp.zeros((SEQ_PER_STEP, WINDOW, LANES), _BF)
        vext[:, 0:WINDOW, :] = jnp.zeros((SEQ_PER_STEP, WINDOW, LANES), _BF)
        xrext[:, 0:SUBLANES, :] = jnp.zeros((SEQ_PER_STEP, SUBLANES, D_LRU), _F32)
        zext[:, 0:SUBLANES, :] = jnp.zeros((SEQ_PER_STEP, SUBLANES, D_SC), _F32)
        hst[...] = jnp.zeros(hst.shape, _F32)
        qi = lax.broadcasted_iota(jnp.int32, (WINDOW, 2 * WINDOW), 0)
        kc = lax.broadcasted_iota(jnp.int32, (WINDOW, 2 * WINDOW), 1)
        dist = qi + WINDOW - kc
        valid = (dist >= 0) & (dist < WINDOW)
        valid_first = valid & (kc >= WINDOW)
        distf = dist.astype(_F32)
        for h, slope in enumerate(_slopes()):
            bias = -slope * distf
            bias_s[0, h] = jnp.where(valid, bias, NEG_INF)
            bias_s[1, h] = jnp.where(valid_first, bias, NEG_INF)

    for r in range(SEQ_PER_STEP):
        _mixer_prompt_row(
            r, s_idx, sinks_ref, x_ref, win_ref, wlru_ref, wout_ref, cw_ref, v256_ref, scw_ref, gmix_ref, ln_ref,
            y_ref, knew_ref, vnew_ref, hnew_ref, lrubuf_ref, scbuf_ref,
            proj.at[r], qsc.at[r], klo.at[r], khi.at[r], vext.at[r], xrext.at[r], zext.at[r], hst.at[r],
            a_s.at[r], b_s.at[r], hl_s.at[r], pc_s.at[r], bias_s, mixed.at[r])


def _mixer_prompt_row(r, s_idx, sinks_ref, x_ref, win_ref, wlru_ref, wout_ref, cw_ref, v256_ref, scw_ref,
                      gmix_ref, ln_ref,
                      y_ref, knew_ref, vnew_ref, hnew_ref, lrubuf_ref, scbuf_ref,
                      proj, qsc, klo, khi, vext, xrext, zext, hst, a_s, b_s, hl_s, pc_s, bias_s, mixed):
    T = T_PROMPT
    nblk = T // WINDOW
    lane = lax.broadcasted_iota(jnp.int32, (1, LANES), 1)
    lo_lane = lane < HEAD_DIM

    proj[...] = _dot(x_ref[r].astype(_BF), win_ref[...])

    for n in range(nblk):
        rows = slice(n * WINDOW, (n + 1) * WINDOW)
        for j in range(N_QTILES):
            qsc[n, j * WINDOW:(j + 1) * WINDOW, :] = (
                proj[rows, j * LANES:(j + 1) * LANES] * (HEAD_DIM ** -0.5)).astype(_BF)
    k = proj[:, O_K:O_K + KV_W]
    v = proj[:, O_V:O_V + KV_W]
    klo[WINDOW:, :] = jnp.where(lo_lane, k, 0.0).astype(_BF)
    khi[WINDOW:, :] = jnp.where(lo_lane, 0.0, k).astype(_BF)
    vext[WINDOW:, :] = v.astype(_BF)
    knew_ref[r] = k[T - WINDOW:, :]
    vnew_ref[r] = v[T - WINDOW:, :]

    def attn_block(n, carry):
        r0 = pl.multiple_of(n * WINDOW, WINDOW)
        first = jnp.logical_and(s_idx == 0, n == 0).astype(jnp.int32)
        qs = qsc[n]
        kl = klo[pl.ds(r0, 2 * WINDOW), :]
        kh = khi[pl.ds(r0, 2 * WINDOW), :]
        vv = vext[pl.ds(r0, 2 * WINDOW), :]
        s_lo = _dot_t(qs, kl)
        s_hi = _dot_t(qs, kh)
        p_lo, p_hi, inv_lo, inv_hi = [], [], [], []
        for j in range(N_QTILES):
            rj = slice(j * WINDOW, (j + 1) * WINDOW)
            p, inv = _sink_softmax(s_lo[rj] + bias_s[first, j], sinks_ref[j])
            p_lo.append(p.astype(_BF))
            inv_lo.append(inv)
            p, inv = _sink_softmax(s_hi[rj] + bias_s[first, GQ + j], sinks_ref[GQ + j])
            p_hi.append(p.astype(_BF))
            inv_hi.append(inv)
        o_lo = _dot(jnp.concatenate(p_lo, axis=0), vv)
        o_hi = _dot(jnp.concatenate(p_hi, axis=0), vv)
        for j in range(N_QTILES):
            rj = slice(j * WINDOW, (j + 1) * WINDOW)
            mixed[pl.ds(r0, WINDOW), j * LANES:(j + 1) * LANES] = jnp.where(
                lo_lane, o_lo[rj] * inv_lo[j], o_hi[rj] * inv_hi[j])
        return carry

    lax.fori_loop(0, nblk, attn_block, 0, unroll=True)
    klo[0:WINDOW, :] = klo[T:T + WINDOW, :]
    khi[0:WINDOW, :] = khi[T:T + WINDOW, :]
    vext[0:WINDOW, :] = vext[T:T + WINDOW, :]

    xr = proj[:, O_XR:O_XR + D_LRU]
    xrext[SUBLANES:, :] = xr
    xc = xr * cw_ref[LRU_CONV - 1:LRU_CONV, :]
    for kk in range(1, LRU_CONV):
        xc = xc + xrext[pl.ds(SUBLANES - kk, T), :] * cw_ref[LRU_CONV - 1 - kk:LRU_CONV - kk, :]
    xcb = xc + v256_ref[0:1, :]
    a, b = _lru_coeffs(xcb, wlru_ref[...], v256_ref[1:2, :], v256_ref[2:3, :], v256_ref[3:4, :])
    NH = D_LRU // LANES
    C = T // SUBLANES
    PITCH = C + SUBLANES
    for hh in range(NH):
        for c in range(SUBLANES):
            a_s[hh, c * PITCH:c * PITCH + C, :] = a[c * C:(c + 1) * C, hh * LANES:(hh + 1) * LANES]
            b_s[hh, c * PITCH:c * PITCH + C, :] = b[c * C:(c + 1) * C, hh * LANES:(hh + 1) * LANES]
    lrubuf_ref[r] = xrext[pl.ds(T + SUBLANES - (LRU_CONV - 1), LRU_CONV - 1), :]
    xrext[0:SUBLANES, :] = xrext[T:T + SUBLANES, :]

    def scan_step(s, carry):
        out = []
        for hh in range(NH):
            h, p = carry[hh]
            at = a_s[hh, pl.ds(s, SUBLANES, stride=PITCH), :]
            bt = b_s[hh, pl.ds(s, SUBLANES, stride=PITCH), :]
            h = at * h + bt
            p = at * p
            hl_s[hh, pl.ds(s, SUBLANES, stride=PITCH), :] = h
            pc_s[hh, pl.ds(s, SUBLANES, stride=PITCH), :] = p
            out.append((h, p))
        return tuple(out)

    init = tuple((jnp.zeros((SUBLANES, LANES), _F32), jnp.ones((SUBLANES, LANES), _F32))
                 for _ in range(NH))
    ends = lax.fori_loop(0, C, scan_step, init, unroll=True)
    h_end = jnp.concatenate([e[0] for e in ends], axis=-1)
    p_end = jnp.concatenate([e[1] for e in ends], axis=-1)
    h_in = hst[...]
    gmix_lru = gmix_ref[:, Q_W:Q_W + D_LRU]
    for c in range(SUBLANES):
        rows = slice(c * C, (c + 1) * C)
        prow = slice(c * PITCH, c * PITCH + C)
        hl = jnp.concatenate([hl_s[hh, prow, :] for hh in range(NH)], axis=-1)
        pc = jnp.concatenate([pc_s[hh, prow, :] for hh in range(NH)], axis=-1)
        hs = hl + pc * h_in
        y_lru = hs * jax.nn.gelu(proj[rows, O_GR:O_GR + D_LRU])
        mixed[rows, Q_W:Q_W + D_LRU] = _rms_rows(y_lru, gmix_lru)
        h_in = p_end[c:c + 1, :] * h_in + h_end[c:c + 1, :]
    hst[...] = h_in
    hnew_ref[r] = h_in

    z = proj[:, O_CG:O_CG + D_SC] * proj[:, O_U:O_U + D_SC]
    zext[SUBLANES:, :] = z
    yc = z * scw_ref[SC_CONV - 1:SC_CONV, :]
    for kk in range(1, SC_CONV):
        yc = yc + zext[pl.ds(SUBLANES - kk, T), :] * scw_ref[SC_CONV - 1 - kk:SC_CONV - kk, :]
    y_sc = proj[:, O_BG:O_BG + D_SC] * yc
    mixed[:, Q_W + D_LRU:] = _rms_rows(y_sc, gmix_ref[:, Q_W + D_LRU:])
    scbuf_ref[r] = zext[pl.ds(T + SUBLANES - (SC_CONV - 1), SC_CONV - 1), :]
    zext[0:SUBLANES, :] = zext[T:T + SUBLANES, :]

    mixed[:, 0:Q_W] = _rms_rows(mixed[:, 0:Q_W], gmix_ref[:, 0:Q_W])
    m = _dot(mixed[...].astype(_BF), wout_ref[...])
    y_ref[r] = _ln_rows(ALPHA * x_ref[r] + m, ln_ref[0:1, :], ln_ref[1:2, :])


def _layer_spec(stacked, l):
    nd = stacked.ndim - 1
    return pl.BlockSpec((None,) + stacked.shape[1:], lambda *_: (l,) + (0,) * nd,
                        pipeline_mode=pl.Buffered(1))


def _mixer_prompt(l, x, sinks, *params):
    B, S, _ = x.shape
    T = T_PROMPT
    R = SEQ_PER_STEP
    grid = (B // R, S // T)
    seq_out = lambda shape: pl.BlockSpec((R,) + shape, lambda b, s: (b, 0, 0))
    return pl.pallas_call(
        _mixer_prompt_kernel,
        grid=grid,
        in_specs=[
            pl.BlockSpec(memory_space=pltpu.SMEM),
            pl.BlockSpec((R, T, D_MODEL), lambda b, s: (b, s, 0)),
            *[_layer_spec(p, l) for p in params],
        ],
        out_specs=[
            pl.BlockSpec((R, T, D_MODEL), lambda b, s: (b, s, 0)),
            seq_out((WINDOW, KV_W)), seq_out((WINDOW, KV_W)), seq_out((1, D_LRU)),
            seq_out((LRU_CONV - 1, D_LRU)), seq_out((SC_CONV - 1, D_SC)),
        ],
        out_shape=[
            jax.ShapeDtypeStruct((B, S, D_MODEL), _F32),
            jax.ShapeDtypeStruct((B, WINDOW, KV_W), _F32),
            jax.ShapeDtypeStruct((B, WINDOW, KV_W), _F32),
            jax.ShapeDtypeStruct((B, 1, D_LRU), _F32),
            jax.ShapeDtypeStruct((B, LRU_CONV - 1, D_LRU), _F32),
            jax.ShapeDtypeStruct((B, SC_CONV - 1, D_SC), _F32),
        ],
        scratch_shapes=[
            pltpu.VMEM((R, T, IN_W), _F32),
            pltpu.VMEM((R, T // WINDOW, N_QTILES * WINDOW, LANES), _BF),
            pltpu.VMEM((R, T + WINDOW, LANES), _BF),
            pltpu.VMEM((R, T + WINDOW, LANES), _BF),
            pltpu.VMEM((R, T + WINDOW, LANES), _BF),
            pltpu.VMEM((R, T + SUBLANES, D_LRU), _F32),
            pltpu.VMEM((R, T + SUBLANES, D_SC), _F32),
            pltpu.VMEM((R, 1, D_LRU), _F32),
            pltpu.VMEM((R, D_LRU // LANES, T + SUBLANES * SUBLANES, LANES), _F32),
            pltpu.VMEM((R, D_LRU // LANES, T + SUBLANES * SUBLANES, LANES), _F32),
            pltpu.VMEM((R, D_LRU // LANES, T + SUBLANES * SUBLANES, LANES), _F32),
            pltpu.VMEM((R, D_LRU // LANES, T + SUBLANES * SUBLANES, LANES), _F32),
            pltpu.VMEM((2, N_HEADS, WINDOW, 2 * WINDOW), _F32),
            pltpu.VMEM((R, T, D_MODEL), _F32),
        ],
        compiler_params=pltpu.CompilerParams(
            dimension_semantics=("arbitrary", "arbitrary"), vmem_limit_bytes=VMEM_LIMIT),
        name="mixer_prompt",
    )(sinks, x, *params)


def _mixer_sample_kernel(sinks_ref, x_ref, ck_ref, cv_ref, h0_ref, lbuf_ref, sbuf_ref,
                         win_ref, wlru_ref, wout_ref, cw_ref, v256_ref, scw_ref, gmix_ref, ln_ref,
                         y_ref, knew_ref, vnew_ref, hnew_ref, lrubuf_ref, scbuf_ref,
                         kx2, vx2, qs2, os2):
    TS = x_ref.shape[0]
    BT = BT_SAMPLE
    NQ = TS * N_QTILES
    lane = lax.broadcasted_iota(jnp.int32, (1, LANES), 1)
    lo_lane = lane < HEAD_DIM

    x = x_ref[...].reshape(TS * BT, D_MODEL)
    proj = _dot(x.astype(_BF), win_ref[...])

    for bb in range(BT):
        kx2[bb * KPAD:bb * KPAD + WINDOW, :] = ck_ref[bb]
        vx2[bb * KPAD:bb * KPAD + WINDOW, :] = cv_ref[bb]
    for t in range(SUBLANES):
        if t < TS:
            rows = slice(t * BT, (t + 1) * BT)
            kx2[pl.ds(WINDOW + t, BT, stride=KPAD), :] = proj[rows, O_K:O_K + KV_W]
            vx2[pl.ds(WINDOW + t, BT, stride=KPAD), :] = proj[rows, O_V:O_V + KV_W]
            for j in range(N_QTILES):
                qs2[pl.ds(t * N_QTILES + j, BT, stride=NQ), :] = (
                    proj[rows, j * LANES:(j + 1) * LANES] * (HEAD_DIM ** -0.5))
        else:
            kx2[pl.ds(WINDOW + t, BT, stride=KPAD), :] = jnp.zeros((BT, LANES), _F32)
            vx2[pl.ds(WINDOW + t, BT, stride=KPAD), :] = jnp.zeros((BT, LANES), _F32)
    for bb in range(BT):
        knew_ref[bb] = kx2[bb * KPAD + TS:bb * KPAD + TS + WINDOW, :]
        vnew_ref[bb] = vx2[bb * KPAD + TS:bb * KPAD + TS + WINDOW, :]

    kall = kx2[...].reshape(BT, KPAD, LANES)
    qs = qs2[...].reshape(BT, NQ, LANES).astype(_BF)
    k_lo = jnp.where(lo_lane, kall, 0.0).astype(_BF)
    k_hi = jnp.where(lo_lane, 0.0, kall).astype(_BF)
    vv = vx2[...].reshape(BT, KPAD, LANES).astype(_BF)
    s_lo = jnp.einsum('bqd,bkd->bqk', qs, k_lo, preferred_element_type=_F32)
    s_hi = jnp.einsum('bqd,bkd->bqk', qs, k_hi, preferred_element_type=_F32)
    ri = lax.broadcasted_iota(jnp.int32, (NQ, KPAD), 0)
    ci = lax.broadcasted_iota(jnp.int32, (NQ, KPAD), 1)
    tq = lax.shift_right_logical(ri, int(math.log2(N_QTILES)))
    jq = ri & (N_QTILES - 1)
    dist = WINDOW + tq - ci
    valid = (dist >= 0) & (dist < WINDOW)
    distf = dist.astype(_F32)
    rj = lax.broadcasted_iota(jnp.int32, (NQ, 1), 0) & (N_QTILES - 1)
    slope_lo = jnp.zeros((NQ, KPAD), _F32)
    slope_hi = jnp.zeros((NQ, KPAD), _F32)
    sink_lo = jnp.zeros((NQ, 1), _F32)
    sink_hi = jnp.zeros((NQ, 1), _F32)
    slopes = _slopes()
    for j in range(N_QTILES):
        slope_lo = jnp.where(jq == j, slopes[j], slope_lo)
        slope_hi = jnp.where(jq == j, slopes[GQ + j], slope_hi)
        sink_lo = jnp.where(rj == j, sinks_ref[j], sink_lo)
        sink_hi = jnp.where(rj == j, sinks_ref[GQ + j], sink_hi)
    bias_lo = jnp.where(valid, -slope_lo * distf, NEG_INF)
    bias_hi = jnp.where(valid, -slope_hi * distf, NEG_INF)
    p_lo, inv_lo = _sink_softmax(s_lo + bias_lo[None], sink_lo[None])
    p_hi, inv_hi = _sink_softmax(s_hi + bias_hi[None], sink_hi[None])
    o_lo = jnp.einsum('bqk,bkd->bqd', p_lo.astype(_BF), vv, preferred_element_type=_F32)
    o_hi = jnp.einsum('bqk,bkd->bqd', p_hi.astype(_BF), vv, preferred_element_type=_F32)
    o = jnp.where(lo_lane[None], o_lo * inv_lo, o_hi * inv_hi)
    os2[...] = o.reshape(BT * NQ, LANES)

    lbuf = lbuf_ref[...]
    sbuf = sbuf_ref[...]
    xr_hist = [lbuf[:, i * D_LRU:(i + 1) * D_LRU] for i in range(LRU_CONV - 1)]
    z_hist = [sbuf[:, i * D_SC:(i + 1) * D_SC] for i in range(SC_CONV - 1)]
    h = h0_ref[...]
    gmix = gmix_ref[...]
    mixed_rows = []
    for t in range(TS):
        rows = slice(t * BT, (t + 1) * BT)
        o_att = jnp.concatenate(
            [os2[pl.ds(t * N_QTILES + j, BT, stride=NQ), :] for j in range(N_QTILES)], axis=-1)
        xr_hist.append(proj[rows, O_XR:O_XR + D_LRU])
        xc = sum(xr_hist[t + kk] * cw_ref[kk:kk + 1, :] for kk in range(LRU_CONV))
        xcb = xc + v256_ref[0:1, :]
        a, b = _lru_coeffs(xcb, wlru_ref[...], v256_ref[1:2, :], v256_ref[2:3, :], v256_ref[3:4, :])
        h = a * h + b
        y_lru = h * jax.nn.gelu(proj[rows, O_GR:O_GR + D_LRU])
        z_hist.append(proj[rows, O_CG:O_CG + D_SC] * proj[rows, O_U:O_U + D_SC])
        yc = sum(z_hist[t + kk] * scw_ref[kk:kk + 1, :] for kk in range(SC_CONV))
        y_sc = proj[rows, O_BG:O_BG + D_SC] * yc
        mixed_rows.append(jnp.concatenate(
            [_rms_rows(o_att, gmix[:, 0:Q_W]),
             _rms_rows(y_lru, gmix[:, Q_W:Q_W + D_LRU]),
             _rms_rows(y_sc, gmix[:, Q_W + D_LRU:])], axis=-1))
    hnew_ref[...] = h
    lrubuf_ref[...] = jnp.concatenate(xr_hist[TS:], axis=-1)
    scbuf_ref[...] = jnp.concatenate(z_hist[TS:], axis=-1)
    mixed = jnp.concatenate(mixed_rows, axis=0)
    m = _dot(mixed.astype(_BF), wout_ref[...])
    y = _ln_rows(ALPHA * x + m, ln_ref[0:1, :], ln_ref[1:2, :])
    y_ref[...] = y.reshape(TS, BT, D_MODEL)


def _mixer_sample(l, x, ck, cv, h0, lbuf, sbuf, sinks, *params):
    TS, B, _ = x.shape
    BT = BT_SAMPLE
    rows2 = lambda w: pl.BlockSpec((BT, w), lambda i: (i, 0))
    rows2_l = lambda w: pl.BlockSpec((None, BT, w), lambda i: (l, i, 0))
    cache = pl.BlockSpec((BT, WINDOW, KV_W), lambda i: (i, 0, 0))
    cache_l = pl.BlockSpec((None, BT, WINDOW, KV_W), lambda i: (l, i, 0, 0))
    xspec = pl.BlockSpec((TS, BT, D_MODEL), lambda i: (0, i, 0))
    return pl.pallas_call(
        _mixer_sample_kernel,
        grid=(B // BT,),
        in_specs=[
            pl.BlockSpec(memory_space=pltpu.SMEM),
            xspec, cache_l, cache_l, rows2_l(D_LRU), rows2_l((LRU_CONV - 1) * D_LRU),
            rows2_l((SC_CONV - 1) * D_SC),
            *[_layer_spec(p, l) for p in params],
        ],
        out_specs=[xspec, cache, cache, rows2(D_LRU), rows2((LRU_CONV - 1) * D_LRU),
                   rows2((SC_CONV - 1) * D_SC)],
        out_shape=[
            jax.ShapeDtypeStruct((TS, B, D_MODEL), _F32),
            jax.ShapeDtypeStruct((B, WINDOW, KV_W), _F32),
            jax.ShapeDtypeStruct((B, WINDOW, KV_W), _F32),
            jax.ShapeDtypeStruct((B, D_LRU), _F32),
            jax.ShapeDtypeStruct((B, (LRU_CONV - 1) * D_LRU), _F32),
            jax.ShapeDtypeStruct((B, (SC_CONV - 1) * D_SC), _F32),
        ],
        scratch_shapes=[
            pltpu.VMEM((BT * KPAD, LANES), _F32),
            pltpu.VMEM((BT * KPAD, LANES), _F32),
            pltpu.VMEM((BT * TS * N_QTILES, LANES), _F32),
            pltpu.VMEM((BT * TS * N_QTILES, LANES), _F32),
        ],
        compiler_params=pltpu.CompilerParams(
            dimension_semantics=("arbitrary",), vmem_limit_bytes=VMEM_LIMIT),
        name="mixer_sample",
    )(sinks, x, ck, cv, h0, lbuf, sbuf, *params)


def _route(logits):
    lane = lax.broadcasted_iota(jnp.int32, logits.shape, 1)
    big = jnp.int32(ROUTE_W)
    is_grp = lane < N_GROUPS
    glog = jnp.where(is_grp, logits, -jnp.inf)
    gmax = jnp.max(glog, axis=-1, keepdims=True)
    gsel = jnp.min(jnp.where(glog == gmax, lane, big), axis=-1, keepdims=True)
    gw = 1.0 / jnp.sum(jnp.exp(glog - gmax), axis=-1, keepdims=True)
    e_lo = N_GROUPS + gsel * EXPERTS_PER_GROUP
    in_grp = (lane >= e_lo) & (lane < e_lo + EXPERTS_PER_GROUP)
    elog = jnp.where(in_grp, logits, -jnp.inf)
    v1 = jnp.max(elog, axis=-1, keepdims=True)
    i1 = jnp.min(jnp.where(elog == v1, lane, big), axis=-1, keepdims=True)
    elog2 = jnp.where(lane == i1, -jnp.inf, elog)
    v2 = jnp.max(elog2, axis=-1, keepdims=True)
    i2 = jnp.min(jnp.where(elog2 == v2, lane, big), axis=-1, keepdims=True)
    e2 = jnp.exp(v2 - v1)
    w1 = gw / (1.0 + e2)
    w2 = gw * e2 / (1.0 + e2)
    return jnp.where(lane == i1, w1, jnp.where(lane == i2, w2, 0.0))


def _group_experts(xb, gates, g, wg, wu, wd, hsc):
    lane = lax.broadcasted_iota(jnp.int32, (1, ROUTE_W), 1)
    for e in range(EXPERTS_PER_GROUP):
        hg = _dot(xb, wg[e])
        hu = _dot(xb, wu[e])
        gate = jnp.sum(jnp.where(lane == N_GROUPS + g * EXPERTS_PER_GROUP + e, gates, 0.0),
                       axis=-1, keepdims=True)
        hsc[:, e * D_EXPERT:(e + 1) * D_EXPERT] = (jax.nn.silu(hg) * hu * gate).astype(_BF)
    return _dot(hsc[...], wd)


def _moe_kernel(x_ref, wr_ref, br_ref, wg_ref, wu_ref, wd_ref, ln_ref, y_ref, xb_s, gates_s, hsc, acc):
    g = pl.program_id(0)

    @pl.when(g == 0)
    def _():
        xb = x_ref[...].astype(_BF)
        xb_s[...] = xb
        gates_s[...] = _route(_dot(xb, wr_ref[...]) + br_ref[...])
        acc[...] = jnp.zeros(acc.shape, _F32)

    wd = wd_ref[...].astype(_BF).reshape(EXPERTS_PER_GROUP * D_EXPERT, D_MODEL)
    acc[...] += _group_experts(xb_s[...], gates_s[...], g, wg_ref[...].astype(_BF),
                               wu_ref[...].astype(_BF), wd, hsc)

    @pl.when(g == N_GROUPS - 1)
    def _():
        y_ref[...] = _ln_rows(ALPHA * x_ref[...] + acc[...], ln_ref[0:1, :], ln_ref[1:2, :])


def _group_weight_spec(w, index_map):
    return pl.BlockSpec((None, EXPERTS_PER_GROUP) + w.shape[2:], index_map)


def _moe(l, x, wr, br, wg, wu, wd, ln):
    N = x.shape[0]
    tok = pl.BlockSpec((N, D_MODEL), lambda g: (0, 0))
    gw = lambda w: _group_weight_spec(w, lambda g: (l, g, 0, 0))
    return pl.pallas_call(
        _moe_kernel,
        grid=(N_GROUPS,),
        in_specs=[tok, _layer_spec(wr, l), _layer_spec(br, l), gw(wg), gw(wu), gw(wd), _layer_spec(ln, l)],
        out_specs=tok,
        out_shape=jax.ShapeDtypeStruct((N, D_MODEL), _F32),
        scratch_shapes=[
            pltpu.VMEM((N, D_MODEL), _BF),
            pltpu.VMEM((N, ROUTE_W), _F32),
            pltpu.VMEM((N, EXPERTS_PER_GROUP * D_EXPERT), _BF),
            pltpu.VMEM((N, D_MODEL), _F32),
        ],
        compiler_params=pltpu.CompilerParams(
            dimension_semantics=("arbitrary",), vmem_limit_bytes=VMEM_LIMIT),
        name="moe",
    )(x, wr, br, wg, wu, wd, ln)


CODE_ROWS = SUBLANES


def _route_kernel(x_ref, wr_ref, br_ref, code_ref, cnt_ref, utri, cnt_s):
    T = T_DISPATCH
    i = pl.program_id(0)

    @pl.when(i == 0)
    def _():
        cnt_s[...] = jnp.zeros((SUBLANES, T), _F32)
        r = lax.broadcasted_iota(jnp.int32, (T, T), 0)
        c = lax.broadcasted_iota(jnp.int32, (T, T), 1)
        utri[...] = jnp.where(r < c, 1.0, 0.0).astype(_BF)

    logits = _dot(x_ref[...].astype(_BF), wr_ref[...]) + br_ref[...]
    lt = logits.T[0:SUBLANES]
    g = [lt[k:k + 1] for k in range(N_GROUPS)]
    gmax = functools.reduce(jnp.maximum, g)
    gsel = jnp.full((1, T), N_GROUPS - 1, jnp.int32)
    for k in range(N_GROUPS - 2, -1, -1):
        gsel = jnp.where(g[k] == gmax, k, gsel)
    row = lax.broadcasted_iota(jnp.int32, (SUBLANES, T), 0)
    onehot = row == gsel
    onehot_f = jnp.where(onehot, 1.0, 0.0)
    rank = _dot(onehot_f.astype(_BF), utri[...])
    cnt = cnt_s[...]
    loc = jnp.sum(jnp.where(onehot, cnt + rank, 0.0), axis=0, keepdims=True).astype(jnp.int32)
    cnt_new = cnt + jnp.sum(onehot_f, axis=1, keepdims=True)
    cnt_s[...] = cnt_new
    cnt_ref[...] = cnt_new[:, 0:LANES].astype(jnp.int32)
    code_ref[0] = jnp.where(row == 0, loc, jnp.where(row == 1, gsel, 0))


def _route_tokens(l, x, wr, br):
    N = x.shape[0]
    T = T_DISPATCH
    return pl.pallas_call(
        _route_kernel,
        grid=(N // T,),
        in_specs=[pl.BlockSpec((T, D_MODEL), lambda i: (i, 0)),
                  _layer_spec(wr, l), _layer_spec(br, l)],
        out_specs=[pl.BlockSpec((1, CODE_ROWS, T), lambda i: (i, 0, 0)),
                   pl.BlockSpec((SUBLANES, LANES), lambda i: (0, 0))],
        out_shape=[jax.ShapeDtypeStruct((N // T, CODE_ROWS, T), jnp.int32),
                   jax.ShapeDtypeStruct((SUBLANES, LANES), jnp.int32)],
        scratch_shapes=[
            pltpu.VMEM((T, T), _BF),
            pltpu.VMEM((SUBLANES, T), _F32),
        ],
        compiler_params=pltpu.CompilerParams(
            dimension_semantics=("arbitrary",), vmem_limit_bytes=VMEM_LIMIT),
        name="moe_route",
    )(x, wr, br)


def _to_token_rows(x, rows_ref):
    R = x.shape[0]
    for c in range(TOK_ROWS):
        rows_ref[pl.ds(c, R, stride=TOK_ROWS), :] = x[:, c * LANES:(c + 1) * LANES]


def _from_token_rows(rows_ref, R):
    return jnp.concatenate(
        [rows_ref[pl.ds(c, R, stride=TOK_ROWS), :] for c in range(TOK_ROWS)], axis=-1)


def _token_copy(src_ref, s, dst_ref, d, sem):
    return pltpu.make_async_copy(
        src_ref.at[pl.ds(pl.multiple_of(s * TOK_ROWS, TOK_ROWS), TOK_ROWS)],
        dst_ref.at[pl.ds(pl.multiple_of(d * TOK_ROWS, TOK_ROWS), TOK_ROWS)], sem)


def _load_dest(dest_ref, dest_sm, sem_idx):
    copies = [pltpu.make_async_copy(dest_ref.at[j, 0], dest_sm.at[pl.ds(j * T_DISPATCH, T_DISPATCH)], sem_idx)
              for j in range(T_PERM // T_DISPATCH)]
    for cp in copies:
        cp.start()
    for cp in copies:
        cp.wait()


def _scatter_kernel(off_ref, code_ref, x_ref, xs_ref, dest_ref, xrow, dest_sm, sem_idx, sem_rows):
    code = code_ref[...]
    loc = code[:, 0:1]
    grp = code[:, 1:2]
    off = jnp.zeros(loc.shape, jnp.int32)
    for g in range(N_GROUPS):
        off = jnp.where(grp == g, off_ref[g], off)
    dest_ref[...] = off + loc
    _to_token_rows(x_ref[...], xrow)
    _load_dest(dest_ref, dest_sm, sem_idx)

    def start(i, c):
        for k in range(DMA_UNROLL):
            t = i * DMA_UNROLL + k
            _token_copy(xrow, t, xs_ref, dest_sm[t], sem_rows).start(priority=k % 2)
        return c

    lax.fori_loop(0, T_PERM // DMA_UNROLL, start, 0)

    def wait(t, c):
        _token_copy(xrow, 0, xs_ref, 0, sem_rows).wait()
        return c

    lax.fori_loop(0, T_PERM, wait, 0, unroll=DMA_UNROLL)


def _scatter(off, code, x):
    N = x.shape[0]
    T = T_PERM
    nt = T // T_DISPATCH
    return pl.pallas_call(
        _scatter_kernel,
        grid=(N // T,),
        in_specs=[pl.BlockSpec(memory_space=pltpu.SMEM),
                  pl.BlockSpec((nt, CODE_ROWS, T_DISPATCH), lambda i: (i, 0, 0)),
                  pl.BlockSpec((T, D_MODEL), lambda i: (i, 0))],
        out_specs=[pl.BlockSpec(memory_space=pl.ANY),
                   pl.BlockSpec((nt, 1, T_DISPATCH), lambda i: (i, 0, 0))],
        out_shape=[jax.ShapeDtypeStruct((N * TOK_ROWS, LANES), _F32),
                   jax.ShapeDtypeStruct((N // T_DISPATCH, 1, T_DISPATCH), jnp.int32)],
        scratch_shapes=[pltpu.VMEM((T * TOK_ROWS, LANES), _F32),
                        pltpu.SMEM((T,), jnp.int32), pltpu.SemaphoreType.DMA, pltpu.SemaphoreType.DMA],
        compiler_params=pltpu.CompilerParams(
            dimension_semantics=("arbitrary",), vmem_limit_bytes=VMEM_LIMIT),
        name="moe_scatter",
    )(off, code, x)


def _route_in_group(logits, g):
    lane = lax.broadcasted_iota(jnp.int32, logits.shape, 1)
    big = jnp.int32(ROUTE_W)
    glog = jnp.where(lane < N_GROUPS, logits, -jnp.inf)
    gmax = jnp.max(glog, axis=-1, keepdims=True)
    lg = jnp.sum(jnp.where(lane == g, logits, 0.0), axis=-1, keepdims=True)
    gw = jnp.exp(lg - gmax) / jnp.sum(jnp.exp(glog - gmax), axis=-1, keepdims=True)
    e_lo = N_GROUPS + g * EXPERTS_PER_GROUP
    in_grp = (lane >= e_lo) & (lane < e_lo + EXPERTS_PER_GROUP)
    elog = jnp.where(in_grp, logits, -jnp.inf)
    v1 = jnp.max(elog, axis=-1, keepdims=True)
    i1 = jnp.min(jnp.where(elog == v1, lane, big), axis=-1, keepdims=True)
    elog2 = jnp.where(lane == i1, -jnp.inf, elog)
    v2 = jnp.max(elog2, axis=-1, keepdims=True)
    i2 = jnp.min(jnp.where(elog2 == v2, lane, big), axis=-1, keepdims=True)
    e2 = jnp.exp(v2 - v1)
    w1 = gw / (1.0 + e2)
    w2 = gw * e2 / (1.0 + e2)
    return jnp.where(lane == i1, w1, jnp.where(lane == i2, w2, 0.0))


def _moe_grouped_kernel(bidx_ref, bgrp_ref, blo_ref, bhi_ref, bfirst_ref, gfirst_ref, nstep_ref,
                        x_ref, wr_ref, br_ref, wg_ref, wu_ref, wd_ref, ln_ref, y_ref,
                        wg_b, wu_b, wd_b, hsc):
    i = pl.program_id(0)

    @pl.when(gfirst_ref[i] == 1)
    def _():
        wg_b[...] = wg_ref[...].astype(_BF)
        wu_b[...] = wu_ref[...].astype(_BF)
        wd_b[...] = wd_ref[...].astype(_BF).reshape(wd_b.shape)

    @pl.when(i < nstep_ref[0])
    def _():
        g = bgrp_ref[i]
        x = _from_token_rows(x_ref, BM)
        xb = x.astype(_BF)
        gates = _route_in_group(_dot(xb, wr_ref[...]) + br_ref[...], g)
        y = _group_experts(xb, gates, g, wg_b, wu_b, wd_b[...], hsc)
        y_new = _ln_rows(ALPHA * x + y, ln_ref[0:1, :], ln_ref[1:2, :])

        @pl.when(bfirst_ref[i] == 1)
        def _():
            _to_token_rows(y_new, y_ref)

        @pl.when(bfirst_ref[i] == 0)
        def _():
            row = lax.broadcasted_iota(jnp.int32, (BM, 1), 0)
            mine = (row >= blo_ref[i]) & (row < bhi_ref[i])
            _to_token_rows(jnp.where(mine, y_new, _from_token_rows(y_ref, BM)), y_ref)


def _moe_grouped(l, tables, xs, wr, br, wg, wu, wd, ln):
    nsteps = tables[0].shape[0]
    tok = pl.BlockSpec((BM * TOK_ROWS, LANES), lambda i, bidx, *_: (bidx[i], 0))
    gw = lambda w: _group_weight_spec(w, lambda i, bidx, bgrp, *_: (l, bgrp[i], 0, 0))
    return pl.pallas_call(
        _moe_grouped_kernel,
        grid_spec=pltpu.PrefetchScalarGridSpec(
            num_scalar_prefetch=len(tables),
            grid=(nsteps,),
            in_specs=[tok, _layer_spec(wr, l), _layer_spec(br, l), gw(wg), gw(wu), gw(wd),
                      _layer_spec(ln, l)],
            out_specs=tok,
            scratch_shapes=[
                pltpu.VMEM((EXPERTS_PER_GROUP, D_MODEL, D_EXPERT), _BF),
                pltpu.VMEM((EXPERTS_PER_GROUP, D_MODEL, D_EXPERT), _BF),
                pltpu.VMEM((EXPERTS_PER_GROUP * D_EXPERT, D_MODEL), _BF),
                pltpu.VMEM((BM, EXPERTS_PER_GROUP * D_EXPERT), _BF),
            ],
        ),
        out_shape=jax.ShapeDtypeStruct(xs.shape, _F32),
        compiler_params=pltpu.CompilerParams(
            dimension_semantics=("arbitrary",), vmem_limit_bytes=VMEM_LIMIT),
        name="moe_grouped",
    )(*tables, xs, wr, br, wg, wu, wd, ln)


def _combine_kernel(dest_ref, ys_ref, out_ref, yrow, dest_sm, sem_idx, sem_rows):
    _load_dest(dest_ref, dest_sm, sem_idx)

    def start(i, c):
        for k in range(DMA_UNROLL):
            t = i * DMA_UNROLL + k
            _token_copy(ys_ref, dest_sm[t], yrow, t, sem_rows).start(priority=k % 2)
        return c

    lax.fori_loop(0, T_PERM // DMA_UNROLL, start, 0)

    def wait(t, c):
        _token_copy(ys_ref, 0, yrow, 0, sem_rows).wait()
        return c

    lax.fori_loop(0, T_PERM, wait, 0, unroll=DMA_UNROLL)
    out_ref[...] = _from_token_rows(yrow, T_PERM)


def _combine(dest, ys):
    N = dest.shape[0] * T_DISPATCH
    T = T_PERM
    nt = T // T_DISPATCH
    return pl.pallas_call(
        _combine_kernel,
        grid=(N // T,),
        in_specs=[pl.BlockSpec((nt, 1, T_DISPATCH), lambda i: (i, 0, 0)),
                  pl.BlockSpec(memory_space=pl.ANY)],
        out_specs=pl.BlockSpec((T, D_MODEL), lambda i: (i, 0)),
        out_shape=jax.ShapeDtypeStruct((N, D_MODEL), _F32),
        scratch_shapes=[pltpu.VMEM((T * TOK_ROWS, LANES), _F32),
                        pltpu.SMEM((T,), jnp.int32), pltpu.SemaphoreType.DMA, pltpu.SemaphoreType.DMA],
        compiler_params=pltpu.CompilerParams(
            dimension_semantics=("arbitrary",), vmem_limit_bytes=VMEM_LIMIT),
        name="moe_combine",
    )(dest, ys)


def _step_tables(cnt, nsteps):
    n = cnt[:N_GROUPS, 0]
    off = jnp.cumsum(n) - n
    b_lo = off // BM
    nb = jnp.where(n > 0, (off + n - 1) // BM - b_lo + 1, 0)
    ends = jnp.cumsum(nb)
    total = ends[-1]
    step = jnp.minimum(jnp.arange(nsteps, dtype=jnp.int32), total - 1)
    grp = jnp.sum(step[:, None] >= ends[None, :], axis=1).astype(jnp.int32)
    bidx = b_lo[grp] + step - (ends - nb)[grp]
    lo = jnp.clip(off[grp] - bidx * BM, 0, BM)
    hi = jnp.clip(off[grp] + n[grp] - bidx * BM, 0, BM)
    one = jnp.ones((1,), jnp.int32)
    first = jnp.concatenate([one, (bidx[1:] != bidx[:-1]).astype(jnp.int32)])
    gfirst = jnp.concatenate([one, (grp[1:] != grp[:-1]).astype(jnp.int32)])
    tables = tuple(a.astype(jnp.int32) for a in (bidx, grp, lo, hi, first, gfirst, total[None]))
    return off.astype(jnp.int32), tables


def _moe_sorted(l, x, wr, br, wg, wu, wd, ln):
    N = x.shape[0]
    code, cnt = _route_tokens(l, x, wr, br)
    off, tables = _step_tables(cnt, N // BM + N_GROUPS - 1)
    xs, dest = _scatter(off, code, x)
    ys = _moe_grouped(l, tables, xs, wr, br, wg, wu, wd, ln)
    return _combine(dest, ys)


def _q_perm(w, axis):
    shape = w.shape
    w = w.reshape(shape[:axis] + (N_KV_HEADS, GQ, HEAD_DIM) + shape[axis + 1:])
    return jnp.swapaxes(w, axis, axis + 1).reshape(shape)


def _block_diag(w):
    eye = jnp.eye(LRU_BLOCKS, dtype=w.dtype)
    return jnp.einsum('nij,nm->nimj', w, eye).reshape(D_LRU, D_LRU)


def kernel(x_prompt, x_sample, cache_k, cache_v, state_lru_h, state_lru_conv, state_sc_conv, w_in, attn_sinks, lru_conv_w, lru_conv_b, lru_w_a, lru_b_a, lru_w_x, lru_b_x, lru_lambda, sc_conv_w, g_mix, w_out, ln1_g, ln1_b, w_grp, b_grp, w_route, b_route, w_gate, w_up, w_down, ln2_g, ln2_b):
    B, S, _ = x_prompt.shape
    DB, TS, _ = x_sample.shape
    assert S % T_PROMPT == 0 and (B * S) % T_PERM == 0 and (B * S) % BM == 0 and DB % BT_SAMPLE == 0
    assert T_PERM % T_DISPATCH == 0
    assert B * S < (1 << 24)
    assert TS <= SUBLANES and cache_k.shape[2] == WINDOW

    win = jnp.concatenate([_q_perm(w_in[:, :, :Q_W], 2), w_in[:, :, Q_W:]], axis=2).astype(_BF)
    wout = jnp.concatenate([_q_perm(w_out[:, :Q_W], 1), w_out[:, Q_W:]], axis=1).astype(_BF)
    gmix = jnp.concatenate([_q_perm(g_mix[:, :Q_W], 1), g_mix[:, Q_W:]], axis=1)[:, None, :]
    wlru = jnp.concatenate([jax.vmap(_block_diag)(lru_w_a), jax.vmap(_block_diag)(lru_w_x)], axis=2).astype(_BF)
    v256 = jnp.stack([lru_conv_b, lru_b_a, lru_b_x, lru_lambda], axis=1)
    ln1 = jnp.stack([ln1_g, ln1_b], axis=1)
    ln2 = jnp.stack([ln2_g, ln2_b], axis=1)
    pad = ROUTE_W - N_GROUPS - N_EXPERTS
    wr = jnp.pad(jnp.concatenate([w_grp, w_route], axis=2), ((0, 0), (0, 0), (0, pad))).astype(_BF)
    br = jnp.pad(jnp.concatenate([b_grp, b_route], axis=1), ((0, 0), (0, pad)))[:, None, :]
    mix_w = (win, wlru, wout, lru_conv_w, v256, sc_conv_w, gmix, ln1)
    moe_w = (wr, br, w_gate, w_up, w_down, ln2)
    ck = cache_k.reshape(DEPTH, DB, WINDOW, KV_W)
    cv = cache_v.reshape(DEPTH, DB, WINDOW, KV_W)
    lbuf = state_lru_conv.reshape(DEPTH, DB, -1)
    sbuf = state_sc_conv.reshape(DEPTH, DB, -1)

    yp = x_prompt
    ys = jnp.swapaxes(x_sample, 0, 1)
    outs_p = [[] for _ in range(5)]
    outs_s = [[] for _ in range(5)]
    for l in range(DEPTH):
        yp, kp, vp, hp, lbp, sbp = _mixer_prompt(l, yp, attn_sinks[l], *mix_w)
        yp = _moe_sorted(l, yp.reshape(B * S, D_MODEL), *moe_w).reshape(B, S, D_MODEL)
        ys, ks, vs, hs, lbs, sbs = _mixer_sample(l, ys, ck, cv, state_lru_h, lbuf, sbuf, attn_sinks[l], *mix_w)
        ys = _moe(l, ys.reshape(TS * DB, D_MODEL), *moe_w).reshape(TS, DB, D_MODEL)
        for lst, arr in zip(outs_p, (kp.reshape(B, WINDOW, N_KV_HEADS, HEAD_DIM),
                                     vp.reshape(B, WINDOW, N_KV_HEADS, HEAD_DIM),
                                     hp.reshape(B, D_LRU), lbp, sbp)):
            lst.append(arr)
        for lst, arr in zip(outs_s, (ks.reshape(DB, WINDOW, N_KV_HEADS, HEAD_DIM),
                                     vs.reshape(DB, WINDOW, N_KV_HEADS, HEAD_DIM),
                                     hs, lbs.reshape(DB, LRU_CONV - 1, D_LRU),
                                     sbs.reshape(DB, SC_CONV - 1, D_SC))):
            lst.append(arr)
    return (yp, jnp.swapaxes(ys, 0, 1),
            *[jnp.stack(o) for o in outs_p], *[jnp.stack(o) for o in outs_s])
```

```python
import functools
import math

import jax
import jax.numpy as jnp
from jax import lax
from jax.experimental import pallas as pl
from jax.experimental.pallas import tpu as pltpu

D_MODEL = 1024
DEPTH = 2
HEAD_DIM = 64
N_HEADS = 8
N_KV_HEADS = 2
GQ = N_HEADS // N_KV_HEADS
WINDOW = 128
ALIBI_MAX = 8.0
D_LRU = 256
LRU_BLOCKS = 4
LRU_BLOCK_W = D_LRU // LRU_BLOCKS
LRU_CONV = 4
LRU_C = 8.0
D_SC = 256
SC_CONV = 3
Q_W = N_HEADS * HEAD_DIM
KV_W = N_KV_HEADS * HEAD_DIM
IN_W = Q_W + 2 * KV_W + 2 * D_LRU + 3 * D_SC
N_GROUPS = 4
EXPERTS_PER_GROUP = 4
N_EXPERTS = N_GROUPS * EXPERTS_PER_GROUP
D_EXPERT = 256
ALPHA = (2 * DEPTH) ** 0.25
LN_EPS = 1e-5
RMS_EPS = 1e-6
NEG_INF = -1e30

O_K = Q_W
O_V = O_K + KV_W
O_XR = O_V + KV_W
O_GR = O_XR + D_LRU
O_U = O_GR + D_LRU
O_BG = O_U + D_SC
O_CG = O_BG + D_SC

LANES = 128
SUBLANES = 8
N_QTILES = Q_W // LANES
ROUTE_W = LANES
VMEM_LIMIT = 56 * 1024 * 1024

T_PROMPT = 1024
SEQ_PER_STEP = 1
T_DISPATCH = 512
T_PERM = 2048
PERM_CHUNK = 256
TOK_ROWS = D_MODEL // LANES
DMA_UNROLL = 8
BM = 512
BT_SAMPLE = 32
KPAD = WINDOW + SUBLANES

_BF = jnp.bfloat16
_F32 = jnp.float32


def _slopes():
    return [2.0 ** (-ALIBI_MAX * h / N_HEADS) for h in range(1, N_HEADS + 1)]


def _dot(a, b):
    return jnp.dot(a, b, preferred_element_type=_F32)


def _dot_t(a, b):
    return lax.dot_general(a, b, (((1,), (1,)), ((), ())), preferred_element_type=_F32)


def _rms_rows(y, g):
    return y * lax.rsqrt(jnp.mean(y * y, axis=-1, keepdims=True) + RMS_EPS) * g


def _ln_rows(x, g, b):
    mu = jnp.mean(x, axis=-1, keepdims=True)
    xc = x - mu
    var = jnp.mean(xc * xc, axis=-1, keepdims=True)
    return xc * lax.rsqrt(var + LN_EPS) * g + b


def _lru_coeffs(xcb, wlru, b_a, b_x, lam):
    gates = _dot(xcb.astype(_BF), wlru)
    r = jax.nn.sigmoid(gates[:, :D_LRU] + b_a)
    i = jax.nn.sigmoid(gates[:, D_LRU:] + b_x)
    log_a = (-LRU_C * jax.nn.softplus(-lam)) * r
    a = jnp.exp(log_a)
    b = jnp.sqrt(jnp.tanh(-log_a) * (1.0 + a * a)) * (i * xcb)
    return a, b


def _sink_softmax(s, sink):
    m = jnp.maximum(jnp.max(s, axis=-1, keepdims=True), sink)
    p = jnp.exp(s - m)
    denom = jnp.sum(p, axis=-1, keepdims=True) + jnp.exp(sink - m)
    return p, 1.0 / denom


def _mixer_prompt_kernel(sinks_ref, x_ref, win_ref, wlru_ref, wout_ref, cw_ref, v256_ref, scw_ref,
                         gmix_ref, ln_ref,
                         y_ref, knew_ref, vnew_ref, hnew_ref, lrubuf_ref, scbuf_ref,
                         proj, qsc, klo, khi, vext, xrext, zext, hst, a_s, b_s, hl_s, pc_s, bias_s, mixed):
    s_idx = pl.program_id(1)

    @pl.when(s_idx == 0)
    def _():
        klo[:, 0:WINDOW, :] = jnp.zeros((SEQ_PER_STEP, WINDOW, LANES), _BF)
        khi[:, 0:WINDOW, :] = jnp.zeros((SEQ_PER_STEP, WINDOW, LANES), _BF)
        vext[:, 0:WINDOW, :] = jnp.zeros((SEQ_PER_STEP, WINDOW, LANES), _BF)
        xrext[:, 0:SUBLANES, :] = jnp.zeros((SEQ_PER_STEP, SUBLANES, D_LRU), _F32)
        zext[:, 0:SUBLANES, :] = jnp.zeros((SEQ_PER_STEP, SUBLANES, D_SC), _F32)
        hst[...] = jnp.zeros(hst.shape, _F32)
        qi = lax.broadcasted_iota(jnp.int32, (WINDOW, 2 * WINDOW), 0)
        kc = lax.broadcasted_iota(jnp.int32, (WINDOW, 2 * WINDOW), 1)
        dist = qi + WINDOW - kc
        valid = (dist >= 0) & (dist < WINDOW)
        valid_first = valid & (kc >= WINDOW)
        distf = dist.astype(_F32)
        for h, slope in enumerate(_slopes()):
            bias = -slope * distf
            bias_s[0, h] = jnp.where(valid, bias, NEG_INF)
            bias_s[1, h] = jnp.where(valid_first, bias, NEG_INF)

    for r in range(SEQ_PER_STEP):
        _mixer_prompt_row(
            r, s_idx, sinks_ref, x_ref, win_ref, wlru_ref, wout_ref, cw_ref, v256_ref, scw_ref, gmix_ref, ln_ref,
            y_ref, knew_ref, vnew_ref, hnew_ref, lrubuf_ref, scbuf_ref,
            proj.at[r], qsc.at[r], klo.at[r], khi.at[r], vext.at[r], xrext.at[r], zext.at[r], hst.at[r],
            a_s.at[r], b_s.at[r], hl_s.at[r], pc_s.at[r], bias_s, mixed.at[r])


def _mixer_prompt_row(r, s_idx, sinks_ref, x_ref, win_ref, wlru_ref, wout_ref, cw_ref, v256_ref, scw_ref,
                      gmix_ref, ln_ref,
                      y_ref, knew_ref, vnew_ref, hnew_ref, lrubuf_ref, scbuf_ref,
                      proj, qsc, klo, khi, vext, xrext, zext, hst, a_s, b_s, hl_s, pc_s, bias_s, mixed):
    T = T_PROMPT
    nblk = T // WINDOW
    lane = lax.broadcasted_iota(jnp.int32, (1, LANES), 1)
    lo_lane = lane < HEAD_DIM

    proj[...] = _dot(x_ref[r].astype(_BF), win_ref[...])

    for n in range(nblk):
        rows = slice(n * WINDOW, (n + 1) * WINDOW)
        for j in range(N_QTILES):
            qsc[n, j * WINDOW:(j + 1) * WINDOW, :] = (
                proj[rows, j * LANES:(j + 1) * LANES] * (HEAD_DIM ** -0.5)).astype(_BF)
    k = proj[:, O_K:O_K + KV_W]
    v = proj[:, O_V:O_V + KV_W]
    klo[WINDOW:, :] = jnp.where(lo_lane, k, 0.0).astype(_BF)
    khi[WINDOW:, :] = jnp.where(lo_lane, 0.0, k).astype(_BF)
    vext[WINDOW:, :] = v.astype(_BF)
    knew_ref[r] = k[T - WINDOW:, :]
    vnew_ref[r] = v[T - WINDOW:, :]

    def attn_block(n, carry):
        r0 = pl.multiple_of(n * WINDOW, WINDOW)
        first = jnp.logical_and(s_idx == 0, n == 0).astype(jnp.int32)
        qs = qsc[n]
        kl = klo[pl.ds(r0, 2 * WINDOW), :]
        kh = khi[pl.ds(r0, 2 * WINDOW), :]
        vv = vext[pl.ds(r0, 2 * WINDOW), :]
        s_lo = _dot_t(qs, kl)
        s_hi = _dot_t(qs, kh)
        p_lo, p_hi, inv_lo, inv_hi = [], [], [], []
        for j in range(N_QTILES):
            rj = slice(j * WINDOW, (j + 1) * WINDOW)
            p, inv = _sink_softmax(s_lo[rj] + bias_s[first, j], sinks_ref[j])
            p_lo.append(p.astype(_BF))
            inv_lo.append(inv)
            p, inv = _sink_softmax(s_hi[rj] + bias_s[first, GQ + j], sinks_ref[GQ + j])
            p_hi.append(p.astype(_BF))
            inv_hi.append(inv)
        o_lo = _dot(jnp.concatenate(p_lo, axis=0), vv)
        o_hi = _dot(jnp.concatenate(p_hi, axis=0), vv)
        for j in range(N_QTILES):
            rj = slice(j * WINDOW, (j + 1) * WINDOW)
            mixed[pl.ds(r0, WINDOW), j * LANES:(j + 1) * LANES] = jnp.where(
                lo_lane, o_lo[rj] * inv_lo[j], o_hi[rj] * inv_hi[j])
        return carry

    lax.fori_loop(0, nblk, attn_block, 0, unroll=True)
    klo[0:WINDOW, :] = klo[T:T + WINDOW, :]
    khi[0:WINDOW, :] = khi[T:T + WINDOW, :]
    vext[0:WINDOW, :] = vext[T:T + WINDOW, :]

    xr = proj[:, O_XR:O_XR + D_LRU]
    xrext[SUBLANES:, :] = xr
    xc = xr * cw_ref[LRU_CONV - 1:LRU_CONV, :]
    for kk in range(1, LRU_CONV):
        xc = xc + xrext[pl.ds(SUBLANES - kk, T), :] * cw_ref[LRU_CONV - 1 - kk:LRU_CONV - kk, :]
    xcb = xc + v256_ref[0:1, :]
    a, b = _lru_coeffs(xcb, wlru_ref[...], v256_ref[1:2, :], v256_ref[2:3, :], v256_ref[3:4, :])
    NH = D_LRU // LANES
    C = T // SUBLANES
    PITCH = C + SUBLANES
    for hh in range(NH):
        for c in range(SUBLANES):
            a_s[hh, c * PITCH:c * PITCH + C, :] = a[c * C:(c + 1) * C, hh * LANES:(hh + 1) * LANES]
            b_s[hh, c * PITCH:c * PITCH + C, :] = b[c * C:(c + 1) * C, hh * LANES:(hh + 1) * LANES]
    lrubuf_ref[r] = xrext[pl.ds(T + SUBLANES - (LRU_CONV - 1), LRU_CONV - 1), :]
    xrext[0:SUBLANES, :] = xrext[T:T + SUBLANES, :]

    def scan_step(s, carry):
        out = []
        for hh in range(NH):
            h, p = carry[hh]
            at = a_s[hh, pl.ds(s, SUBLANES, stride=PITCH), :]
            bt = b_s[hh, pl.ds(s, SUBLANES, stride=PITCH), :]
            h = at * h + bt
            p = at * p
            hl_s[hh, pl.ds(s, SUBLANES, stride=PITCH), :] = h
            pc_s[hh, pl.ds(s, SUBLANES, stride=PITCH), :] = p
            out.append((h, p))
        return tuple(out)

    init = tuple((jnp.zeros((SUBLANES, LANES), _F32), jnp.ones((SUBLANES, LANES), _F32))
                 for _ in range(NH))
    ends = lax.fori_loop(0, C, scan_step, init, unroll=True)
    h_end = jnp.concatenate([e[0] for e in ends], axis=-1)
    p_end = jnp.concatenate([e[1] for e in ends], axis=-1)
    h_in = hst[...]
    gmix_lru = gmix_ref[:, Q_W:Q_W + D_LRU]
    for c in range(SUBLANES):
        rows = slice(c * C, (c + 1) * C)
        prow = slice(c * PITCH, c * PITCH + C)
        hl = jnp.concatenate([hl_s[hh, prow, :] for hh in range(NH)], axis=-1)
        pc = jnp.concatenate([pc_s[hh, prow, :] for hh in range(NH)], axis=-1)
        hs = hl + pc * h_in
        y_lru = hs * jax.nn.gelu(proj[rows, O_GR:O_GR + D_LRU])
        mixed[rows, Q_W:Q_W + D_LRU] = _rms_rows(y_lru, gmix_lru)
        h_in = p_end[c:c + 1, :] * h_in + h_end[c:c + 1, :]
    hst[...] = h_in
    hnew_ref[r] = h_in

    z = proj[:, O_CG:O_CG + D_SC] * proj[:, O_U:O_U + D_SC]
    zext[SUBLANES:, :] = z
    yc = z * scw_ref[SC_CONV - 1:SC_CONV, :]
    for kk in range(1, SC_CONV):
        yc = yc + zext[pl.ds(SUBLANES - kk, T), :] * scw_ref[SC_CONV - 1 - kk:SC_CONV - kk, :]
    y_sc = proj[:, O_BG:O_BG + D_SC] * yc
    mixed[:, Q_W + D_LRU:] = _rms_rows(y_sc, gmix_ref[:, Q_W + D_LRU:])
    scbuf_ref[r] = zext[pl.ds(T + SUBLANES - (SC_CONV - 1), SC_CONV - 1), :]
    zext[0:SUBLANES, :] = zext[T:T + SUBLANES, :]

    mixed[:, 0:Q_W] = _rms_rows(mixed[:, 0:Q_W], gmix_ref[:, 0:Q_W])
    m = _dot(mixed[...].astype(_BF), wout_ref[...])
    y_ref[r] = _ln_rows(ALPHA * x_ref[r] + m, ln_ref[0:1, :], ln_ref[1:2, :])


def _layer_spec(stacked, l):
    nd = stacked.ndim - 1
    return pl.BlockSpec((None,) + stacked.shape[1:], lambda *_: (l,) + (0,) * nd,
                        pipeline_mode=pl.Buffered(1))


def _mixer_prompt(l, x, sinks, *params):
    B, S, _ = x.shape
    T = T_PROMPT
    R = SEQ_PER_STEP
    grid = (B // R, S // T)
    seq_out = lambda shape: pl.BlockSpec((R,) + shape, lambda b, s: (b, 0, 0))
    return pl.pallas_call(
        _mixer_prompt_kernel,
        grid=grid,
        in_specs=[
            pl.BlockSpec(memory_space=pltpu.SMEM),
            pl.BlockSpec((R, T, D_MODEL), lambda b, s: (b, s, 0)),
            *[_layer_spec(p, l) for p in params],
        ],
        out_specs=[
            pl.BlockSpec((R, T, D_MODEL), lambda b, s: (b, s, 0)),
            seq_out((WINDOW, KV_W)), seq_out((WINDOW, KV_W)), seq_out((1, D_LRU)),
            seq_out((LRU_CONV - 1, D_LRU)), seq_out((SC_CONV - 1, D_SC)),
        ],
        out_shape=[
            jax.ShapeDtypeStruct((B, S, D_MODEL), _F32),
            jax.ShapeDtypeStruct((B, WINDOW, KV_W), _F32),
            jax.ShapeDtypeStruct((B, WINDOW, KV_W), _F32),
            jax.ShapeDtypeStruct((B, 1, D_LRU), _F32),
            jax.ShapeDtypeStruct((B, LRU_CONV - 1, D_LRU), _F32),
            jax.ShapeDtypeStruct((B, SC_CONV - 1, D_SC), _F32),
        ],
        scratch_shapes=[
            pltpu.VMEM((R, T, IN_W), _F32),
            pltpu.VMEM((R, T // WINDOW, N_QTILES * WINDOW, LANES), _BF),
            pltpu.VMEM((R, T + WINDOW, LANES), _BF),
            pltpu.VMEM((R, T + WINDOW, LANES), _BF),
            pltpu.VMEM((R, T + WINDOW, LANES), _BF),
            pltpu.VMEM((R, T + SUBLANES, D_LRU), _F32),
            pltpu.VMEM((R, T + SUBLANES, D_SC), _F32),
            pltpu.VMEM((R, 1, D_LRU), _F32),
            pltpu.VMEM((R, D_LRU // LANES, T + SUBLANES * SUBLANES, LANES), _F32),
            pltpu.VMEM((R, D_LRU // LANES, T + SUBLANES * SUBLANES, LANES), _F32),
            pltpu.VMEM((R, D_LRU // LANES, T + SUBLANES * SUBLANES, LANES), _F32),
            pltpu.VMEM((R, D_LRU // LANES, T + SUBLANES * SUBLANES, LANES), _F32),
            pltpu.VMEM((2, N_HEADS, WINDOW, 2 * WINDOW), _F32),
            pltpu.VMEM((R, T, D_MODEL), _F32),
        ],
        compiler_params=pltpu.CompilerParams(
            dimension_semantics=("arbitrary", "arbitrary"), vmem_limit_bytes=VMEM_LIMIT),
        name="mixer_prompt",
    )(sinks, x, *params)


def _mixer_sample_kernel(sinks_ref, x_ref, ck_ref, cv_ref, h0_ref, lbuf_ref, sbuf_ref,
                         win_ref, wlru_ref, wout_ref, cw_ref, v256_ref, scw_ref, gmix_ref, ln_ref,
                         y_ref, knew_ref, vnew_ref, hnew_ref, lrubuf_ref, scbuf_ref,
                         kx2, vx2, qs2, os2):
    TS = x_ref.shape[0]
    BT = BT_SAMPLE
    NQ = TS * N_QTILES
    lane = lax.broadcasted_iota(jnp.int32, (1, LANES), 1)
    lo_lane = lane < HEAD_DIM

    x = x_ref[...].reshape(TS * BT, D_MODEL)
    proj = _dot(x.astype(_BF), win_ref[...])

    for bb in range(BT):
        kx2[bb * KPAD:bb * KPAD + WINDOW, :] = ck_ref[bb]
        vx2[bb * KPAD:bb * KPAD + WINDOW, :] = cv_ref[bb]
    for t in range(SUBLANES):
        if t < TS:
            rows = slice(t * BT, (t + 1) * BT)
            kx2[pl.ds(WINDOW + t, BT, stride=KPAD), :] = proj[rows, O_K:O_K + KV_W]
            vx2[pl.ds(WINDOW + t, BT, stride=KPAD), :] = proj[rows, O_V:O_V + KV_W]
            for j in range(N_QTILES):
                qs2[pl.ds(t * N_QTILES + j, BT, stride=NQ), :] = (
                    proj[rows, j * LANES:(j + 1) * LANES] * (HEAD_DIM ** -0.5))
        else:
            kx2[pl.ds(WINDOW + t, BT, stride=KPAD), :] = jnp.zeros((BT, LANES), _F32)
            vx2[pl.ds(WINDOW + t, BT, stride=KPAD), :] = jnp.zeros((BT, LANES), _F32)
    for bb in range(BT):
        knew_ref[bb] = kx2[bb * KPAD + TS:bb * KPAD + TS + WINDOW, :]
        vnew_ref[bb] = vx2[bb * KPAD + TS:bb * KPAD + TS + WINDOW, :]

    kall = kx2[...].reshape(BT, KPAD, LANES)
    qs = qs2[...].reshape(BT, NQ, LANES).astype(_BF)
    k_lo = jnp.where(lo_lane, kall, 0.0).astype(_BF)
    k_hi = jnp.where(lo_lane, 0.0, kall).astype(_BF)
    vv = vx2[...].reshape(BT, KPAD, LANES).astype(_BF)
    s_lo = jnp.einsum('bqd,bkd->bqk', qs, k_lo, preferred_element_type=_F32)
    s_hi = jnp.einsum('bqd,bkd->bqk', qs, k_hi, preferred_element_type=_F32)
    ri = lax.broadcasted_iota(jnp.int32, (NQ, KPAD), 0)
    ci = lax.broadcasted_iota(jnp.int32, (NQ, KPAD), 1)
    tq = lax.shift_right_logical(ri, int(math.log2(N_QTILES)))
    jq = ri & (N_QTILES - 1)
    dist = WINDOW + tq - ci
    valid = (dist >= 0) & (dist < WINDOW)
    distf = dist.astype(_F32)
    rj = lax.broadcasted_iota(jnp.int32, (NQ, 1), 0) & (N_QTILES - 1)
    slope_lo = jnp.zeros((NQ, KPAD), _F32)
    slope_hi = jnp.zeros((NQ, KPAD), _F32)
    sink_lo = jnp.zeros((NQ, 1), _F32)
    sink_hi = jnp.zeros((NQ, 1), _F32)
    slopes = _slopes()
    for j in range(N_QTILES):
        slope_lo = jnp.where(jq == j, slopes[j], slope_lo)
        slope_hi = jnp.where(jq == j, slopes[GQ + j], slope_hi)
        sink_lo = jnp.where(rj == j, sinks_ref[j], sink_lo)
        sink_hi = jnp.where(rj == j, sinks_ref[GQ + j], sink_hi)
    bias_lo = jnp.where(valid, -slope_lo * distf, NEG_INF)
    bias_hi = jnp.where(valid, -slope_hi * distf, NEG_INF)
    p_lo, inv_lo = _sink_softmax(s_lo + bias_lo[None], sink_lo[None])
    p_hi, inv_hi = _sink_softmax(s_hi + bias_hi[None], sink_hi[None])
    o_lo = jnp.einsum('bqk,bkd->bqd', p_lo.astype(_BF), vv, preferred_element_type=_F32)
    o_hi = jnp.einsum('bqk,bkd->bqd', p_hi.astype(_BF), vv, preferred_element_type=_F32)
    o = jnp.where(lo_lane[None], o_lo * inv_lo, o_hi * inv_hi)
    os2[...] = o.reshape(BT * NQ, LANES)

    lbuf = lbuf_ref[...]
    sbuf = sbuf_ref[...]
    xr_hist = [lbuf[:, i * D_LRU:(i + 1) * D_LRU] for i in range(LRU_CONV - 1)]
    z_hist = [sbuf[:, i * D_SC:(i + 1) * D_SC] for i in range(SC_CONV - 1)]
    h = h0_ref[...]
    gmix = gmix_ref[...]
    mixed_rows = []
    for t in range(TS):
        rows = slice(t * BT, (t + 1) * BT)
        o_att = jnp.concatenate(
            [os2[pl.ds(t * N_QTILES + j, BT, stride=NQ), :] for j in range(N_QTILES)], axis=-1)
        xr_hist.append(proj[rows, O_XR:O_XR + D_LRU])
        xc = sum(xr_hist[t + kk] * cw_ref[kk:kk + 1, :] for kk in range(LRU_CONV))
        xcb = xc + v256_ref[0:1, :]
        a, b = _lru_coeffs(xcb, wlru_ref[...], v256_ref[1:2, :], v256_ref[2:3, :], v256_ref[3:4, :])
        h = a * h + b
        y_lru = h * jax.nn.gelu(proj[rows, O_GR:O_GR + D_LRU])
        z_hist.append(proj[rows, O_CG:O_CG + D_SC] * proj[rows, O_U:O_U + D_SC])
        yc = sum(z_hist[t + kk] * scw_ref[kk:kk + 1, :] for kk in range(SC_CONV))
        y_sc = proj[rows, O_BG:O_BG + D_SC] * yc
        mixed_rows.append(jnp.concatenate(
            [_rms_rows(o_att, gmix[:, 0:Q_W]),
             _rms_rows(y_lru, gmix[:, Q_W:Q_W + D_LRU]),
             _rms_rows(y_sc, gmix[:, Q_W + D_LRU:])], axis=-1))
    hnew_ref[...] = h
    lrubuf_ref[...] = jnp.concatenate(xr_hist[TS:], axis=-1)
    scbuf_ref[...] = jnp.concatenate(z_hist[TS:], axis=-1)
    mixed = jnp.concatenate(mixed_rows, axis=0)
    m = _dot(mixed.astype(_BF), wout_ref[...])
    y = _ln_rows(ALPHA * x + m, ln_ref[0:1, :], ln_ref[1:2, :])
    y_ref[...] = y.reshape(TS, BT, D_MODEL)


def _mixer_sample(l, x, ck, cv, h0, lbuf, sbuf, sinks, *params):
    TS, B, _ = x.shape
    BT = BT_SAMPLE
    rows2 = lambda w: pl.BlockSpec((BT, w), lambda i: (i, 0))
    rows2_l = lambda w: pl.BlockSpec((None, BT, w), lambda i: (l, i, 0))
    cache = pl.BlockSpec((BT, WINDOW, KV_W), lambda i: (i, 0, 0))
    cache_l = pl.BlockSpec((None, BT, WINDOW, KV_W), lambda i: (l, i, 0, 0))
    xspec = pl.BlockSpec((TS, BT, D_MODEL), lambda i: (0, i, 0))
    return pl.pallas_call(
        _mixer_sample_kernel,
        grid=(B // BT,),
        in_specs=[
            pl.BlockSpec(memory_space=pltpu.SMEM),
            xspec, cache_l, cache_l, rows2_l(D_LRU), rows2_l((LRU_CONV - 1) * D_LRU),
            rows2_l((SC_CONV - 1) * D_SC),
            *[_layer_spec(p, l) for p in params],
        ],
        out_specs=[xspec, cache, cache, rows2(D_LRU), rows2((LRU_CONV - 1) * D_LRU),
                   rows2((SC_CONV - 1) * D_SC)],
        out_shape=[
            jax.ShapeDtypeStruct((TS, B, D_MODEL), _F32),
            jax.ShapeDtypeStruct((B, WINDOW, KV_W), _F32),
            jax.ShapeDtypeStruct((B, WINDOW, KV_W), _F32),
            jax.ShapeDtypeStruct((B, D_LRU), _F32),
            jax.ShapeDtypeStruct((B, (LRU_CONV - 1) * D_LRU), _F32),
            jax.ShapeDtypeStruct((B, (SC_CONV - 1) * D_SC), _F32),
        ],
        scratch_shapes=[
            pltpu.VMEM((BT * KPAD, LANES), _F32),
            pltpu.VMEM((BT * KPAD, LANES), _F32),
            pltpu.VMEM((BT * TS * N_QTILES, LANES), _F32),
            pltpu.VMEM((BT * TS * N_QTILES, LANES), _F32),
        ],
        compiler_params=pltpu.CompilerParams(
            dimension_semantics=("arbitrary",), vmem_limit_bytes=VMEM_LIMIT),
        name="mixer_sample",
    )(sinks, x, ck, cv, h0, lbuf, sbuf, *params)


def _route(logits):
    lane = lax.broadcasted_iota(jnp.int32, logits.shape, 1)
    big = jnp.int32(ROUTE_W)
    is_grp = lane < N_GROUPS
    glog = jnp.where(is_grp, logits, -jnp.inf)
    gmax = jnp.max(glog, axis=-1, keepdims=True)
    gsel = jnp.min(jnp.where(glog == gmax, lane, big), axis=-1, keepdims=True)
    gw = 1.0 / jnp.sum(jnp.exp(glog - gmax), axis=-1, keepdims=True)
    e_lo = N_GROUPS + gsel * EXPERTS_PER_GROUP
    in_grp = (lane >= e_lo) & (lane < e_lo + EXPERTS_PER_GROUP)
    elog = jnp.where(in_grp, logits, -jnp.inf)
    v1 = jnp.max(elog, axis=-1, keepdims=True)
    i1 = jnp.min(jnp.where(elog == v1, lane, big), axis=-1, keepdims=True)
    elog2 = jnp.where(lane == i1, -jnp.inf, elog)
    v2 = jnp.max(elog2, axis=-1, keepdims=True)
    i2 = jnp.min(jnp.where(elog2 == v2, lane, big), axis=-1, keepdims=True)
    e2 = jnp.exp(v2 - v1)
    w1 = gw / (1.0 + e2)
    w2 = gw * e2 / (1.0 + e2)
    return jnp.where(lane == i1, w1, jnp.where(lane == i2, w2, 0.0))


def _group_experts(xb, gates, g, wg, wu, wd, hsc):
    lane = lax.broadcasted_iota(jnp.int32, (1, ROUTE_W), 1)
    for e in range(EXPERTS_PER_GROUP):
        hg = _dot(xb, wg[e])
        hu = _dot(xb, wu[e])
        gate = jnp.sum(jnp.where(lane == N_GROUPS + g * EXPERTS_PER_GROUP + e, gates, 0.0),
                       axis=-1, keepdims=True)
        hsc[:, e * D_EXPERT:(e + 1) * D_EXPERT] = (jax.nn.silu(hg) * hu * gate).astype(_BF)
    return _dot(hsc[...], wd)


def _moe_kernel(x_ref, wr_ref, br_ref, wg_ref, wu_ref, wd_ref, ln_ref, y_ref, xb_s, gates_s, hsc, acc):
    g = pl.program_id(0)

    @pl.when(g == 0)
    def _():
        xb = x_ref[...].astype(_BF)
        xb_s[...] = xb
        gates_s[...] = _route(_dot(xb, wr_ref[...]) + br_ref[...])
        acc[...] = jnp.zeros(acc.shape, _F32)

    wd = wd_ref[...].astype(_BF).reshape(EXPERTS_PER_GROUP * D_EXPERT, D_MODEL)
    acc[...] += _group_experts(xb_s[...], gates_s[...], g, wg_ref[...].astype(_BF),
                               wu_ref[...].astype(_BF), wd, hsc)

    @pl.when(g == N_GROUPS - 1)
    def _():
        y_ref[...] = _ln_rows(ALPHA * x_ref[...] + acc[...], ln_ref[0:1, :], ln_ref[1:2, :])


def _group_weight_spec(w, index_map):
    return pl.BlockSpec((None, EXPERTS_PER_GROUP) + w.shape[2:], index_map)


def _moe(l, x, wr, br, wg, wu, wd, ln):
    N = x.shape[0]
    tok = pl.BlockSpec((N, D_MODEL), lambda g: (0, 0))
    gw = lambda w: _group_weight_spec(w, lambda g: (l, g, 0, 0))
    return pl.pallas_call(
        _moe_kernel,
        grid=(N_GROUPS,),
        in_specs=[tok, _layer_spec(wr, l), _layer_spec(br, l), gw(wg), gw(wu), gw(wd), _layer_spec(ln, l)],
        out_specs=tok,
        out_shape=jax.ShapeDtypeStruct((N, D_MODEL), _F32),
        scratch_shapes=[
            pltpu.VMEM((N, D_MODEL), _BF),
            pltpu.VMEM((N, ROUTE_W), _F32),
            pltpu.VMEM((N, EXPERTS_PER_GROUP * D_EXPERT), _BF),
            pltpu.VMEM((N, D_MODEL), _F32),
        ],
        compiler_params=pltpu.CompilerParams(
            dimension_semantics=("arbitrary",), vmem_limit_bytes=VMEM_LIMIT),
        name="moe",
    )(x, wr, br, wg, wu, wd, ln)


CODE_ROWS = SUBLANES


def _route_kernel(x_ref, wr_ref, br_ref, code_ref, cnt_ref, utri, cnt_s):
    T = T_DISPATCH
    i = pl.program_id(0)

    @pl.when(i == 0)
    def _():
        cnt_s[...] = jnp.zeros((SUBLANES, T), _F32)
        r = lax.broadcasted_iota(jnp.int32, (T, T), 0)
        c = lax.broadcasted_iota(jnp.int32, (T, T), 1)
        utri[...] = jnp.where(r < c, 1.0, 0.0).astype(_BF)

    logits = _dot(x_ref[...].astype(_BF), wr_ref[...]) + br_ref[...]
    lt = logits.T[0:SUBLANES]
    g = [lt[k:k + 1] for k in range(N_GROUPS)]
    gmax = functools.reduce(jnp.maximum, g)
    gsel = jnp.full((1, T), N_GROUPS - 1, jnp.int32)
    for k in range(N_GROUPS - 2, -1, -1):
        gsel = jnp.where(g[k] == gmax, k, gsel)
    row = lax.broadcasted_iota(jnp.int32, (SUBLANES, T), 0)
    onehot = row == gsel
    onehot_f = jnp.where(onehot, 1.0, 0.0)
    rank = _dot(onehot_f.astype(_BF), utri[...])
    cnt = cnt_s[...]
    loc = jnp.sum(jnp.where(onehot, cnt + rank, 0.0), axis=0, keepdims=True).astype(jnp.int32)
    cnt_new = cnt + jnp.sum(onehot_f, axis=1, keepdims=True)
    cnt_s[...] = cnt_new
    cnt_ref[...] = cnt_new[:, 0:LANES].astype(jnp.int32)
    code_ref[0] = jnp.where(row == 0, loc, jnp.where(row == 1, gsel, 0))


def _route_tokens(l, x, wr, br):
    N = x.shape[0]
    T = T_DISPATCH
    return pl.pallas_call(
        _route_kernel,
        grid=(N // T,),
        in_specs=[pl.BlockSpec((T, D_MODEL), lambda i: (i, 0)),
                  _layer_spec(wr, l), _layer_spec(br, l)],
        out_specs=[pl.BlockSpec((1, CODE_ROWS, T), lambda i: (i, 0, 0)),
                   pl.BlockSpec((SUBLANES, LANES), lambda i: (0, 0))],
        out_shape=[jax.ShapeDtypeStruct((N // T, CODE_ROWS, T), jnp.int32),
                   jax.ShapeDtypeStruct((SUBLANES, LANES), jnp.int32)],
        scratch_shapes=[
            pltpu.VMEM((T, T), _BF),
            pltpu.VMEM((SUBLANES, T), _F32),
        ],
        compiler_params=pltpu.CompilerParams(
            dimension_semantics=("arbitrary",), vmem_limit_bytes=VMEM_LIMIT),
        name="moe_route",
    )(x, wr, br)


def _to_token_rows(x, rows_ref, first=0):
    R = x.shape[0]
    for c in range(TOK_ROWS):
        rows_ref[pl.ds(first * TOK_ROWS + c, R, stride=TOK_ROWS), :] = x[:, c * LANES:(c + 1) * LANES]


def _from_token_rows(rows_ref, R, first=0):
    return jnp.concatenate(
        [rows_ref[pl.ds(first * TOK_ROWS + c, R, stride=TOK_ROWS), :] for c in range(TOK_ROWS)], axis=-1)


def _token_copy(src_ref, s, dst_ref, d, sem):
    return pltpu.make_async_copy(
        src_ref.at[pl.ds(pl.multiple_of(s * TOK_ROWS, TOK_ROWS), TOK_ROWS)],
        dst_ref.at[pl.ds(pl.multiple_of(d * TOK_ROWS, TOK_ROWS), TOK_ROWS)], sem)


def _load_dest(dest_ref, dest_sm, sem_idx):
    copies = [pltpu.make_async_copy(dest_ref.at[j, 0], dest_sm.at[pl.ds(j * T_DISPATCH, T_DISPATCH)], sem_idx)
              for j in range(T_PERM // T_DISPATCH)]
    for cp in copies:
        cp.start()
    for cp in copies:
        cp.wait()


def _scatter_kernel(off_ref, code_ref, x_ref, xs_ref, dest_ref, xrow, dest_sm, sem_idx, sem_rows):
    code = code_ref[...]
    loc = code[:, 0:1]
    grp = code[:, 1:2]
    off = jnp.zeros(loc.shape, jnp.int32)
    for g in range(N_GROUPS):
        off = jnp.where(grp == g, off_ref[g], off)
    dest_ref[...] = off + loc
    _load_dest(dest_ref, dest_sm, sem_idx)

    for ch in range(T_PERM // PERM_CHUNK):
        t0 = ch * PERM_CHUNK
        _to_token_rows(x_ref[t0:t0 + PERM_CHUNK, :], xrow, t0)

        def start(i, c, t0=t0):
            for k in range(DMA_UNROLL):
                t = t0 + i * DMA_UNROLL + k
                _token_copy(xrow, t, xs_ref, dest_sm[t], sem_rows).start(priority=k % 2)
            return c

        lax.fori_loop(0, PERM_CHUNK // DMA_UNROLL, start, 0)

    def wait(t, c):
        _token_copy(xrow, 0, xs_ref, 0, sem_rows).wait()
        return c

    lax.fori_loop(0, T_PERM, wait, 0, unroll=DMA_UNROLL)


def _scatter(off, code, x):
    N = x.shape[0]
    T = T_PERM
    nt = T // T_DISPATCH
    return pl.pallas_call(
        _scatter_kernel,
        grid=(N // T,),
        in_specs=[pl.BlockSpec(memory_space=pltpu.SMEM),
                  pl.BlockSpec((nt, CODE_ROWS, T_DISPATCH), lambda i: (i, 0, 0)),
                  pl.BlockSpec((T, D_MODEL), lambda i: (i, 0))],
        out_specs=[pl.BlockSpec(memory_space=pl.ANY),
                   pl.BlockSpec((nt, 1, T_DISPATCH), lambda i: (i, 0, 0))],
        out_shape=[jax.ShapeDtypeStruct((N * TOK_ROWS, LANES), _F32),
                   jax.ShapeDtypeStruct((N // T_DISPATCH, 1, T_DISPATCH), jnp.int32)],
        scratch_shapes=[pltpu.VMEM((T * TOK_ROWS, LANES), _F32),
                        pltpu.SMEM((T,), jnp.int32), pltpu.SemaphoreType.DMA, pltpu.SemaphoreType.DMA],
        compiler_params=pltpu.CompilerParams(
            dimension_semantics=("arbitrary",), vmem_limit_bytes=VMEM_LIMIT),
        name="moe_scatter",
    )(off, code, x)


def _route_in_group(logits, g):
    lane = lax.broadcasted_iota(jnp.int32, logits.shape, 1)
    big = jnp.int32(ROUTE_W)
    glog = jnp.where(lane < N_GROUPS, logits, -jnp.inf)
    gmax = jnp.max(glog, axis=-1, keepdims=True)
    lg = jnp.sum(jnp.where(lane == g, logits, 0.0), axis=-1, keepdims=True)
    gw = jnp.exp(lg - gmax) / jnp.sum(jnp.exp(glog - gmax), axis=-1, keepdims=True)
    e_lo = N_GROUPS + g * EXPERTS_PER_GROUP
    in_grp = (lane >= e_lo) & (lane < e_lo + EXPERTS_PER_GROUP)
    elog = jnp.where(in_grp, logits, -jnp.inf)
    v1 = jnp.max(elog, axis=-1, keepdims=True)
    i1 = jnp.min(jnp.where(elog == v1, lane, big), axis=-1, keepdims=True)
    elog2 = jnp.where(lane == i1, -jnp.inf, elog)
    v2 = jnp.max(elog2, axis=-1, keepdims=True)
    i2 = jnp.min(jnp.where(elog2 == v2, lane, big), axis=-1, keepdims=True)
    e2 = jnp.exp(v2 - v1)
    w1 = gw / (1.0 + e2)
    w2 = gw * e2 / (1.0 + e2)
    return jnp.where(lane == i1, w1, jnp.where(lane == i2, w2, 0.0))


def _moe_grouped_kernel(bidx_ref, bgrp_ref, blo_ref, bhi_ref, bfirst_ref, gfirst_ref, nstep_ref,
                        x_ref, wr_ref, br_ref, wg_ref, wu_ref, wd_ref, ln_ref, y_ref,
                        wg_b, wu_b, wd_b, hsc):
    i = pl.program_id(0)

    @pl.when(gfirst_ref[i] == 1)
    def _():
        wg_b[...] = wg_ref[...].astype(_BF)
        wu_b[...] = wu_ref[...].astype(_BF)
        wd_b[...] = wd_ref[...].astype(_BF).reshape(wd_b.shape)

    @pl.when(i < nstep_ref[0])
    def _():
        g = bgrp_ref[i]
        x = _from_token_rows(x_ref, BM)
        xb = x.astype(_BF)
        gates = _route_in_group(_dot(xb, wr_ref[...]) + br_ref[...], g)
        y = _group_experts(xb, gates, g, wg_b, wu_b, wd_b[...], hsc)
        y_new = _ln_rows(ALPHA * x + y, ln_ref[0:1, :], ln_ref[1:2, :])

        @pl.when(bfirst_ref[i] == 1)
        def _():
            _to_token_rows(y_new, y_ref)

        @pl.when(bfirst_ref[i] == 0)
        def _():
            row = lax.broadcasted_iota(jnp.int32, (BM, 1), 0)
            mine = (row >= blo_ref[i]) & (row < bhi_ref[i])
            _to_token_rows(jnp.where(mine, y_new, _from_token_rows(y_ref, BM)), y_ref)


def _moe_grouped(l, tables, xs, wr, br, wg, wu, wd, ln):
    nsteps = tables[0].shape[0]
    tok = pl.BlockSpec((BM * TOK_ROWS, LANES), lambda i, bidx, *_: (bidx[i], 0))
    gw = lambda w: _group_weight_spec(w, lambda i, bidx, bgrp, *_: (l, bgrp[i], 0, 0))
    return pl.pallas_call(
        _moe_grouped_kernel,
        grid_spec=pltpu.PrefetchScalarGridSpec(
            num_scalar_prefetch=len(tables),
            grid=(nsteps,),
            in_specs=[tok, _layer_spec(wr, l), _layer_spec(br, l), gw(wg), gw(wu), gw(wd),
                      _layer_spec(ln, l)],
            out_specs=tok,
            scratch_shapes=[
                pltpu.VMEM((EXPERTS_PER_GROUP, D_MODEL, D_EXPERT), _BF),
                pltpu.VMEM((EXPERTS_PER_GROUP, D_MODEL, D_EXPERT), _BF),
                pltpu.VMEM((EXPERTS_PER_GROUP * D_EXPERT, D_MODEL), _BF),
                pltpu.VMEM((BM, EXPERTS_PER_GROUP * D_EXPERT), _BF),
            ],
        ),
        out_shape=jax.ShapeDtypeStruct(xs.shape, _F32),
        compiler_params=pltpu.CompilerParams(
            dimension_semantics=("arbitrary",), vmem_limit_bytes=VMEM_LIMIT),
        name="moe_grouped",
    )(*tables, xs, wr, br, wg, wu, wd, ln)


def _combine_kernel(dest_ref, ys_ref, out_ref, yrow, dest_sm, sem_idx, sem_rows):
    _load_dest(dest_ref, dest_sm, sem_idx)
    chunks = range(T_PERM // PERM_CHUNK)

    for ch in chunks:
        def start(i, c, ch=ch):
            for k in range(DMA_UNROLL):
                t = ch * PERM_CHUNK + i * DMA_UNROLL + k
                _token_copy(ys_ref, dest_sm[t], yrow, t, sem_rows.at[ch]).start(priority=k % 2)
            return c

        lax.fori_loop(0, PERM_CHUNK // DMA_UNROLL, start, 0)

    for ch in chunks:
        def wait(t, c, ch=ch):
            _token_copy(ys_ref, 0, yrow, 0, sem_rows.at[ch]).wait()
            return c

        lax.fori_loop(0, PERM_CHUNK, wait, 0, unroll=DMA_UNROLL)
        t0 = ch * PERM_CHUNK
        out_ref[t0:t0 + PERM_CHUNK, :] = _from_token_rows(yrow, PERM_CHUNK, t0)


def _combine(dest, ys):
    N = dest.shape[0] * T_DISPATCH
    T = T_PERM
    nt = T // T_DISPATCH
    return pl.pallas_call(
        _combine_kernel,
        grid=(N // T,),
        in_specs=[pl.BlockSpec((nt, 1, T_DISPATCH), lambda i: (i, 0, 0)),
                  pl.BlockSpec(memory_space=pl.ANY)],
        out_specs=pl.BlockSpec((T, D_MODEL), lambda i: (i, 0)),
        out_shape=jax.ShapeDtypeStruct((N, D_MODEL), _F32),
        scratch_shapes=[pltpu.VMEM((T * TOK_ROWS, LANES), _F32),
                        pltpu.SMEM((T,), jnp.int32), pltpu.SemaphoreType.DMA,
                        pltpu.SemaphoreType.DMA((T // PERM_CHUNK,))],
        compiler_params=pltpu.CompilerParams(
            dimension_semantics=("arbitrary",), vmem_limit_bytes=VMEM_LIMIT),
        name="moe_combine",
    )(dest, ys)


def _step_tables(cnt, nsteps):
    n = cnt[:N_GROUPS, 0]
    off = jnp.cumsum(n) - n
    b_lo = off // BM
    nb = jnp.where(n > 0, (off + n - 1) // BM - b_lo + 1, 0)
    ends = jnp.cumsum(nb)
    total = ends[-1]
    step = jnp.minimum(jnp.arange(nsteps, dtype=jnp.int32), total - 1)
    grp = jnp.sum(step[:, None] >= ends[None, :], axis=1).astype(jnp.int32)
    bidx = b_lo[grp] + step - (ends - nb)[grp]
    lo = jnp.clip(off[grp] - bidx * BM, 0, BM)
    hi = jnp.clip(off[grp] + n[grp] - bidx * BM, 0, BM)
    one = jnp.ones((1,), jnp.int32)
    first = jnp.concatenate([one, (bidx[1:] != bidx[:-1]).astype(jnp.int32)])
    gfirst = jnp.concatenate([one, (grp[1:] != grp[:-1]).astype(jnp.int32)])
    tables = tuple(a.astype(jnp.int32) for a in (bidx, grp, lo, hi, first, gfirst, total[None]))
    return off.astype(jnp.int32), tables


def _moe_sorted(l, x, wr, br, wg, wu, wd, ln):
    N = x.shape[0]
    code, cnt = _route_tokens(l, x, wr, br)
    off, tables = _step_tables(cnt, N // BM + N_GROUPS - 1)
    xs, dest = _scatter(off, code, x)
    ys = _moe_grouped(l, tables, xs, wr, br, wg, wu, wd, ln)
    return _combine(dest, ys)


def _q_perm(w, axis):
    shape = w.shape
    w = w.reshape(shape[:axis] + (N_KV_HEADS, GQ, HEAD_DIM) + shape[axis + 1:])
    return jnp.swapaxes(w, axis, axis + 1).reshape(shape)


def _block_diag(w):
    eye = jnp.eye(LRU_BLOCKS, dtype=w.dtype)
    return jnp.einsum('nij,nm->nimj', w, eye).reshape(D_LRU, D_LRU)


def kernel(x_prompt, x_sample, cache_k, cache_v, state_lru_h, state_lru_conv, state_sc_conv, w_in, attn_sinks, lru_conv_w, lru_conv_b, lru_w_a, lru_b_a, lru_w_x, lru_b_x, lru_lambda, sc_conv_w, g_mix, w_out, ln1_g, ln1_b, w_grp, b_grp, w_route, b_route, w_gate, w_up, w_down, ln2_g, ln2_b):
    B, S, _ = x_prompt.shape
    DB, TS, _ = x_sample.shape
    assert S % T_PROMPT == 0 and (B * S) % T_PERM == 0 and (B * S) % BM == 0 and DB % BT_SAMPLE == 0
    assert T_PERM % T_DISPATCH == 0
    assert B * S < (1 << 24)
    assert TS <= SUBLANES and cache_k.shape[2] == WINDOW

    win = jnp.concatenate([_q_perm(w_in[:, :, :Q_W], 2), w_in[:, :, Q_W:]], axis=2).astype(_BF)
    wout = jnp.concatenate([_q_perm(w_out[:, :Q_W], 1), w_out[:, Q_W:]], axis=1).astype(_BF)
    gmix = jnp.concatenate([_q_perm(g_mix[:, :Q_W], 1), g_mix[:, Q_W:]], axis=1)[:, None, :]
    wlru = jnp.concatenate([jax.vmap(_block_diag)(lru_w_a), jax.vmap(_block_diag)(lru_w_x)], axis=2).astype(_BF)
    v256 = jnp.stack([lru_conv_b, lru_b_a, lru_b_x, lru_lambda], axis=1)
    ln1 = jnp.stack([ln1_g, ln1_b], axis=1)
    ln2 = jnp.stack([ln2_g, ln2_b], axis=1)
    pad = ROUTE_W - N_GROUPS - N_EXPERTS
    wr = jnp.pad(jnp.concatenate([w_grp, w_route], axis=2), ((0, 0), (0, 0), (0, pad))).astype(_BF)
    br = jnp.pad(jnp.concatenate([b_grp, b_route], axis=1), ((0, 0), (0, pad)))[:, None, :]
    mix_w = (win, wlru, wout, lru_conv_w, v256, sc_conv_w, gmix, ln1)
    moe_w = (wr, br, w_gate, w_up, w_down, ln2)
    ck = cache_k.reshape(DEPTH, DB, WINDOW, KV_W)
    cv = cache_v.reshape(DEPTH, DB, WINDOW, KV_W)
    lbuf = state_lru_conv.reshape(DEPTH, DB, -1)
    sbuf = state_sc_conv.reshape(DEPTH, DB, -1)

    yp = x_prompt
    ys = jnp.swapaxes(x_sample, 0, 1)
    outs_p = [[] for _ in range(5)]
    outs_s = [[] for _ in range(5)]
    for l in range(DEPTH):
        yp, kp, vp, hp, lbp, sbp = _mixer_prompt(l, yp, attn_sinks[l], *mix_w)
        yp = _moe_sorted(l, yp.reshape(B * S, D_MODEL), *moe_w).reshape(B, S, D_MODEL)
        ys, ks, vs, hs, lbs, sbs = _mixer_sample(l, ys, ck, cv, state_lru_h, lbuf, sbuf, attn_sinks[l], *mix_w)
        ys = _moe(l, ys.reshape(TS * DB, D_MODEL), *moe_w).reshape(TS, DB, D_MODEL)
        for lst, arr in zip(outs_p, (kp.reshape(B, WINDOW, N_KV_HEADS, HEAD_DIM),
                                     vp.reshape(B, WINDOW, N_KV_HEADS, HEAD_DIM),
                                     hp.reshape(B, D_LRU), lbp, sbp)):
            lst.append(arr)
        for lst, arr in zip(outs_s, (ks.reshape(DB, WINDOW, N_KV_HEADS, HEAD_DIM),
                                     vs.reshape(DB, WINDOW, N_KV_HEADS, HEAD_DIM),
                                     hs, lbs.reshape(DB, LRU_CONV - 1, D_LRU),
                                     sbs.reshape(DB, SC_CONV - 1, D_SC))):
            lst.append(arr)
    return (yp, jnp.swapaxes(ys, 0, 1),
            *[jnp.stack(o) for o in outs_p], *[jnp.stack(o) for o in outs_s])
```

```python
import functools
import math

import jax
import jax.numpy as jnp
from jax import lax
from jax.experimental import pallas as pl
from jax.experimental.pallas import tpu as pltpu

D_MODEL = 1024
DEPTH = 2
HEAD_DIM = 64
N_HEADS = 8
N_KV_HEADS = 2
GQ = N_HEADS // N_KV_HEADS
WINDOW = 128
ALIBI_MAX = 8.0
D_LRU = 256
LRU_BLOCKS = 4
LRU_BLOCK_W = D_LRU // LRU_BLOCKS
LRU_CONV = 4
LRU_C = 8.0
D_SC = 256
SC_CONV = 3
Q_W = N_HEADS * HEAD_DIM
KV_W = N_KV_HEADS * HEAD_DIM
IN_W = Q_W + 2 * KV_W + 2 * D_LRU + 3 * D_SC
N_GROUPS = 4
EXPERTS_PER_GROUP = 4
N_EXPERTS = N_GROUPS * EXPERTS_PER_GROUP
D_EXPERT = 256
ALPHA = (2 * DEPTH) ** 0.25
LN_EPS = 1e-5
RMS_EPS = 1e-6
NEG_INF = -1e30

O_K = Q_W
O_V = O_K + KV_W
O_XR = O_V + KV_W
O_GR = O_XR + D_LRU
O_U = O_GR + D_LRU
O_BG = O_U + D_SC
O_CG = O_BG + D_SC

LANES = 128
SUBLANES = 8
N_QTILES = Q_W // LANES
ROUTE_W = LANES
VMEM_LIMIT = 56 * 1024 * 1024

T_PROMPT = 1024
SEQ_PER_STEP = 1
T_DISPATCH = 512
T_PERM = 2048
TOK_ROWS = D_MODEL // LANES
DMA_UNROLL = 8
BM = 512
BT_SAMPLE = 32
KPAD = WINDOW + SUBLANES

_BF = jnp.bfloat16
_F32 = jnp.float32


def _slopes():
    return [2.0 ** (-ALIBI_MAX * h / N_HEADS) for h in range(1, N_HEADS + 1)]


def _dot(a, b):
    return jnp.dot(a, b, preferred_element_type=_F32)


def _dot_t(a, b):
    return lax.dot_general(a, b, (((1,), (1,)), ((), ())), preferred_element_type=_F32)


def _rms_rows(y, g):
    return y * lax.rsqrt(jnp.mean(y * y, axis=-1, keepdims=True) + RMS_EPS) * g


def _ln_rows(x, g, b):
    mu = jnp.mean(x, axis=-1, keepdims=True)
    xc = x - mu
    var = jnp.mean(xc * xc, axis=-1, keepdims=True)
    return xc * lax.rsqrt(var + LN_EPS) * g + b


def _lru_coeffs(xcb, wlru, b_a, b_x, lam):
    gates = _dot(xcb.astype(_BF), wlru)
    r = jax.nn.sigmoid(gates[:, :D_LRU] + b_a)
    i = jax.nn.sigmoid(gates[:, D_LRU:] + b_x)
    log_a = (-LRU_C * jax.nn.softplus(-lam)) * r
    a = jnp.exp(log_a)
    b = jnp.sqrt(jnp.tanh(-log_a) * (1.0 + a * a)) * (i * xcb)
    return a, b


def _sink_softmax(s, sink):
    m = jnp.maximum(jnp.max(s, axis=-1, keepdims=True), sink)
    p = jnp.exp(s - m)
    denom = jnp.sum(p, axis=-1, keepdims=True) + jnp.exp(sink - m)
    return p, 1.0 / denom


def _mixer_prompt_kernel(sinks_ref, x_ref, win_ref, wlru_ref, wout_ref, cw_ref, v256_ref, scw_ref,
                         gmix_ref, ln_ref,
                         y_ref, knew_ref, vnew_ref, hnew_ref, lrubuf_ref, scbuf_ref,
                         proj, qsc, klo, khi, vext, xrext, zext, hst, a_s, b_s, hl_s, pc_s, bias_s, mixed):
    s_idx = pl.program_id(1)

    @pl.when(s_idx == 0)
    def _():
        klo[:, 0:WINDOW, :] = jnp.zeros((SEQ_PER_STEP, WINDOW, LANES), _BF)
        khi[:, 0:WINDOW, :] = jnp.zeros((SEQ_PER_STEP, WINDOW, LANES), _BF)
        vext[:, 0:WINDOW, :] = jnp.zeros((SEQ_PER_STEP, WINDOW, LANES), _BF)
        xrext[:, 0:SUBLANES, :] = jnp.zeros((SEQ_PER_STEP, SUBLANES, D_LRU), _F32)
        zext[:, 0:SUBLANES, :] = jnp.zeros((SEQ_PER_STEP, SUBLANES, D_SC), _F32)
        hst[...] = jnp.zeros(hst.shape, _F32)
        qi = lax.broadcasted_iota(jnp.int32, (WINDOW, 2 * WINDOW), 0)
        kc = lax.broadcasted_iota(jnp.int32, (WINDOW, 2 * WINDOW), 1)
        dist = qi + WINDOW - kc
        valid = (dist >= 0) & (dist < WINDOW)
        valid_first = valid & (kc >= WINDOW)
        distf = dist.astype(_F32)
        for h, slope in enumerate(_slopes()):
            bias = -slope * distf
            bias_s[0, h] = jnp.where(valid, bias, NEG_INF)
            bias_s[1, h] = jnp.where(valid_first, bias, NEG_INF)

    for r in range(SEQ_PER_STEP):
        _mixer_prompt_row(
            r, s_idx, sinks_ref, x_ref, win_ref, wlru_ref, wout_ref, cw_ref, v256_ref, scw_ref, gmix_ref, ln_ref,
            y_ref, knew_ref, vnew_ref, hnew_ref, lrubuf_ref, scbuf_ref,
            proj.at[r], qsc.at[r], klo.at[r], khi.at[r], vext.at[r], xrext.at[r], zext.at[r], hst.at[r],
            a_s.at[r], b_s.at[r], hl_s.at[r], pc_s.at[r], bias_s, mixed.at[r])


def _mixer_prompt_row(r, s_idx, sinks_ref, x_ref, win_ref, wlru_ref, wout_ref, cw_ref, v256_ref, scw_ref,
                      gmix_ref, ln_ref,
                      y_ref, knew_ref, vnew_ref, hnew_ref, lrubuf_ref, scbuf_ref,
                      proj, qsc, klo, khi, vext, xrext, zext, hst, a_s, b_s, hl_s, pc_s, bias_s, mixed):
    T = T_PROMPT
    nblk = T // WINDOW
    lane = lax.broadcasted_iota(jnp.int32, (1, LANES), 1)
    lo_lane = lane < HEAD_DIM

    xb = x_ref[r].astype(_BF)
    proj[:, 0:O_XR] = _dot(xb, win_ref[:, 0:O_XR])

    for n in range(nblk):
        rows = slice(n * WINDOW, (n + 1) * WINDOW)
        for j in range(N_QTILES):
            qsc[n, j * WINDOW:(j + 1) * WINDOW, :] = (
                proj[rows, j * LANES:(j + 1) * LANES] * (HEAD_DIM ** -0.5)).astype(_BF)
    k = proj[:, O_K:O_K + KV_W]
    v = proj[:, O_V:O_V + KV_W]
    klo[WINDOW:, :] = jnp.where(lo_lane, k, 0.0).astype(_BF)
    khi[WINDOW:, :] = jnp.where(lo_lane, 0.0, k).astype(_BF)
    vext[WINDOW:, :] = v.astype(_BF)
    knew_ref[r] = k[T - WINDOW:, :]
    vnew_ref[r] = v[T - WINDOW:, :]

    def attn_block(n, carry):
        r0 = pl.multiple_of(n * WINDOW, WINDOW)
        first = jnp.logical_and(s_idx == 0, n == 0).astype(jnp.int32)
        qs = qsc[n]
        kl = klo[pl.ds(r0, 2 * WINDOW), :]
        kh = khi[pl.ds(r0, 2 * WINDOW), :]
        vv = vext[pl.ds(r0, 2 * WINDOW), :]
        s_lo = _dot_t(qs, kl)
        s_hi = _dot_t(qs, kh)
        p_lo, p_hi, inv_lo, inv_hi = [], [], [], []
        for j in range(N_QTILES):
            rj = slice(j * WINDOW, (j + 1) * WINDOW)
            p, inv = _sink_softmax(s_lo[rj] + bias_s[first, j], sinks_ref[j])
            p_lo.append(p.astype(_BF))
            inv_lo.append(inv)
            p, inv = _sink_softmax(s_hi[rj] + bias_s[first, GQ + j], sinks_ref[GQ + j])
            p_hi.append(p.astype(_BF))
            inv_hi.append(inv)
        o_lo = _dot(jnp.concatenate(p_lo, axis=0), vv)
        o_hi = _dot(jnp.concatenate(p_hi, axis=0), vv)
        for j in range(N_QTILES):
            rj = slice(j * WINDOW, (j + 1) * WINDOW)
            mixed[pl.ds(r0, WINDOW), j * LANES:(j + 1) * LANES] = jnp.where(
                lo_lane, o_lo[rj] * inv_lo[j], o_hi[rj] * inv_hi[j])
        return carry

    lax.fori_loop(0, nblk, attn_block, 0, unroll=True)
    klo[0:WINDOW, :] = klo[T:T + WINDOW, :]
    khi[0:WINDOW, :] = khi[T:T + WINDOW, :]
    vext[0:WINDOW, :] = vext[T:T + WINDOW, :]

    proj[:, O_XR:O_U] = _dot(xb, win_ref[:, O_XR:O_U])
    xr = proj[:, O_XR:O_XR + D_LRU]
    xrext[SUBLANES:, :] = xr
    xc = xr * cw_ref[LRU_CONV - 1:LRU_CONV, :]
    for kk in range(1, LRU_CONV):
        xc = xc + xrext[pl.ds(SUBLANES - kk, T), :] * cw_ref[LRU_CONV - 1 - kk:LRU_CONV - kk, :]
    xcb = xc + v256_ref[0:1, :]
    a, b = _lru_coeffs(xcb, wlru_ref[...], v256_ref[1:2, :], v256_ref[2:3, :], v256_ref[3:4, :])
    NH = D_LRU // LANES
    C = T // SUBLANES
    PITCH = C + SUBLANES
    for hh in range(NH):
        for c in range(SUBLANES):
            a_s[hh, c * PITCH:c * PITCH + C, :] = a[c * C:(c + 1) * C, hh * LANES:(hh + 1) * LANES]
            b_s[hh, c * PITCH:c * PITCH + C, :] = b[c * C:(c + 1) * C, hh * LANES:(hh + 1) * LANES]
    lrubuf_ref[r] = xrext[pl.ds(T + SUBLANES - (LRU_CONV - 1), LRU_CONV - 1), :]
    xrext[0:SUBLANES, :] = xrext[T:T + SUBLANES, :]

    def scan_step(s, carry):
        out = []
        for hh in range(NH):
            h, p = carry[hh]
            at = a_s[hh, pl.ds(s, SUBLANES, stride=PITCH), :]
            bt = b_s[hh, pl.ds(s, SUBLANES, stride=PITCH), :]
            h = at * h + bt
            p = at * p
            hl_s[hh, pl.ds(s, SUBLANES, stride=PITCH), :] = h
            pc_s[hh, pl.ds(s, SUBLANES, stride=PITCH), :] = p
            out.append((h, p))
        return tuple(out)

    init = tuple((jnp.zeros((SUBLANES, LANES), _F32), jnp.ones((SUBLANES, LANES), _F32))
                 for _ in range(NH))
    ends = lax.fori_loop(0, C, scan_step, init, unroll=True)
    h_end = jnp.concatenate([e[0] for e in ends], axis=-1)
    p_end = jnp.concatenate([e[1] for e in ends], axis=-1)
    h_in = hst[...]
    gmix_lru = gmix_ref[:, Q_W:Q_W + D_LRU]
    for c in range(SUBLANES):
        rows = slice(c * C, (c + 1) * C)
        prow = slice(c * PITCH, c * PITCH + C)
        hl = jnp.concatenate([hl_s[hh, prow, :] for hh in range(NH)], axis=-1)
        pc = jnp.concatenate([pc_s[hh, prow, :] for hh in range(NH)], axis=-1)
        hs = hl + pc * h_in
        y_lru = hs * jax.nn.gelu(proj[rows, O_GR:O_GR + D_LRU])
        mixed[rows, Q_W:Q_W + D_LRU] = _rms_rows(y_lru, gmix_lru)
        h_in = p_end[c:c + 1, :] * h_in + h_end[c:c + 1, :]
    hst[...] = h_in
    hnew_ref[r] = h_in

    proj[:, O_U:] = _dot(xb, win_ref[:, O_U:])
    z = proj[:, O_CG:O_CG + D_SC] * proj[:, O_U:O_U + D_SC]
    zext[SUBLANES:, :] = z
    yc = z * scw_ref[SC_CONV - 1:SC_CONV, :]
    for kk in range(1, SC_CONV):
        yc = yc + zext[pl.ds(SUBLANES - kk, T), :] * scw_ref[SC_CONV - 1 - kk:SC_CONV - kk, :]
    y_sc = proj[:, O_BG:O_BG + D_SC] * yc
    mixed[:, Q_W + D_LRU:] = _rms_rows(y_sc, gmix_ref[:, Q_W + D_LRU:])
    scbuf_ref[r] = zext[pl.ds(T + SUBLANES - (SC_CONV - 1), SC_CONV - 1), :]
    zext[0:SUBLANES, :] = zext[T:T + SUBLANES, :]

    mixed[:, 0:Q_W] = _rms_rows(mixed[:, 0:Q_W], gmix_ref[:, 0:Q_W])
    m = _dot(mixed[...].astype(_BF), wout_ref[...])
    y_ref[r] = _ln_rows(ALPHA * x_ref[r] + m, ln_ref[0:1, :], ln_ref[1:2, :])


def _layer_spec(stacked, l):
    nd = stacked.ndim - 1
    return pl.BlockSpec((None,) + stacked.shape[1:], lambda *_: (l,) + (0,) * nd,
                        pipeline_mode=pl.Buffered(1))


def _mixer_prompt(l, x, sinks, *params):
    B, S, _ = x.shape
    T = T_PROMPT
    R = SEQ_PER_STEP
    grid = (B // R, S // T)
    seq_out = lambda shape: pl.BlockSpec((R,) + shape, lambda b, s: (b, 0, 0))
    return pl.pallas_call(
        _mixer_prompt_kernel,
        grid=grid,
        in_specs=[
            pl.BlockSpec(memory_space=pltpu.SMEM),
            pl.BlockSpec((R, T, D_MODEL), lambda b, s: (b, s, 0)),
            *[_layer_spec(p, l) for p in params],
        ],
        out_specs=[
            pl.BlockSpec((R, T, D_MODEL), lambda b, s: (b, s, 0)),
            seq_out((WINDOW, KV_W)), seq_out((WINDOW, KV_W)), seq_out((1, D_LRU)),
            seq_out((LRU_CONV - 1, D_LRU)), seq_out((SC_CONV - 1, D_SC)),
        ],
        out_shape=[
            jax.ShapeDtypeStruct((B, S, D_MODEL), _F32),
            jax.ShapeDtypeStruct((B, WINDOW, KV_W), _F32),
            jax.ShapeDtypeStruct((B, WINDOW, KV_W), _F32),
            jax.ShapeDtypeStruct((B, 1, D_LRU), _F32),
            jax.ShapeDtypeStruct((B, LRU_CONV - 1, D_LRU), _F32),
            jax.ShapeDtypeStruct((B, SC_CONV - 1, D_SC), _F32),
        ],
        scratch_shapes=[
            pltpu.VMEM((R, T, IN_W), _F32),
            pltpu.VMEM((R, T // WINDOW, N_QTILES * WINDOW, LANES), _BF),
            pltpu.VMEM((R, T + WINDOW, LANES), _BF),
            pltpu.VMEM((R, T + WINDOW, LANES), _BF),
            pltpu.VMEM((R, T + WINDOW, LANES), _BF),
            pltpu.VMEM((R, T + SUBLANES, D_LRU), _F32),
            pltpu.VMEM((R, T + SUBLANES, D_SC), _F32),
            pltpu.VMEM((R, 1, D_LRU), _F32),
            pltpu.VMEM((R, D_LRU // LANES, T + SUBLANES * SUBLANES, LANES), _F32),
            pltpu.VMEM((R, D_LRU // LANES, T + SUBLANES * SUBLANES, LANES), _F32),
            pltpu.VMEM((R, D_LRU // LANES, T + SUBLANES * SUBLANES, LANES), _F32),
            pltpu.VMEM((R, D_LRU // LANES, T + SUBLANES * SUBLANES, LANES), _F32),
            pltpu.VMEM((2, N_HEADS, WINDOW, 2 * WINDOW), _F32),
            pltpu.VMEM((R, T, D_MODEL), _F32),
        ],
        compiler_params=pltpu.CompilerParams(
            dimension_semantics=("arbitrary", "arbitrary"), vmem_limit_bytes=VMEM_LIMIT),
        name="mixer_prompt",
    )(sinks, x, *params)


def _mixer_sample_kernel(sinks_ref, x_ref, ck_ref, cv_ref, h0_ref, lbuf_ref, sbuf_ref,
                         win_ref, wlru_ref, wout_ref, cw_ref, v256_ref, scw_ref, gmix_ref, ln_ref,
                         y_ref, knew_ref, vnew_ref, hnew_ref, lrubuf_ref, scbuf_ref,
                         kx2, vx2, qs2, os2):
    TS = x_ref.shape[0]
    BT = BT_SAMPLE
    NQ = TS * N_QTILES
    lane = lax.broadcasted_iota(jnp.int32, (1, LANES), 1)
    lo_lane = lane < HEAD_DIM

    x = x_ref[...].reshape(TS * BT, D_MODEL)
    proj = _dot(x.astype(_BF), win_ref[...])

    for bb in range(BT):
        kx2[bb * KPAD:bb * KPAD + WINDOW, :] = ck_ref[bb]
        vx2[bb * KPAD:bb * KPAD + WINDOW, :] = cv_ref[bb]
    for t in range(SUBLANES):
        if t < TS:
            rows = slice(t * BT, (t + 1) * BT)
            kx2[pl.ds(WINDOW + t, BT, stride=KPAD), :] = proj[rows, O_K:O_K + KV_W]
            vx2[pl.ds(WINDOW + t, BT, stride=KPAD), :] = proj[rows, O_V:O_V + KV_W]
            for j in range(N_QTILES):
                qs2[pl.ds(t * N_QTILES + j, BT, stride=NQ), :] = (
                    proj[rows, j * LANES:(j + 1) * LANES] * (HEAD_DIM ** -0.5))
        else:
            kx2[pl.ds(WINDOW + t, BT, stride=KPAD), :] = jnp.zeros((BT, LANES), _F32)
            vx2[pl.ds(WINDOW + t, BT, stride=KPAD), :] = jnp.zeros((BT, LANES), _F32)
    for bb in range(BT):
        knew_ref[bb] = kx2[bb * KPAD + TS:bb * KPAD + TS + WINDOW, :]
        vnew_ref[bb] = vx2[bb * KPAD + TS:bb * KPAD + TS + WINDOW, :]

    kall = kx2[...].reshape(BT, KPAD, LANES)
    qs = qs2[...].reshape(BT, NQ, LANES).astype(_BF)
    k_lo = jnp.where(lo_lane, kall, 0.0).astype(_BF)
    k_hi = jnp.where(lo_lane, 0.0, kall).astype(_BF)
    vv = vx2[...].reshape(BT, KPAD, LANES).astype(_BF)
    s_lo = jnp.einsum('bqd,bkd->bqk', qs, k_lo, preferred_element_type=_F32)
    s_hi = jnp.einsum('bqd,bkd->bqk', qs, k_hi, preferred_element_type=_F32)
    ri = lax.broadcasted_iota(jnp.int32, (NQ, KPAD), 0)
    ci = lax.broadcasted_iota(jnp.int32, (NQ, KPAD), 1)
    tq = lax.shift_right_logical(ri, int(math.log2(N_QTILES)))
    jq = ri & (N_QTILES - 1)
    dist = WINDOW + tq - ci
    valid = (dist >= 0) & (dist < WINDOW)
    distf = dist.astype(_F32)
    rj = lax.broadcasted_iota(jnp.int32, (NQ, 1), 0) & (N_QTILES - 1)
    slope_lo = jnp.zeros((NQ, KPAD), _F32)
    slope_hi = jnp.zeros((NQ, KPAD), _F32)
    sink_lo = jnp.zeros((NQ, 1), _F32)
    sink_hi = jnp.zeros((NQ, 1), _F32)
    slopes = _slopes()
    for j in range(N_QTILES):
        slope_lo = jnp.where(jq == j, slopes[j], slope_lo)
        slope_hi = jnp.where(jq == j, slopes[GQ + j], slope_hi)
        sink_lo = jnp.where(rj == j, sinks_ref[j], sink_lo)
        sink_hi = jnp.where(rj == j, sinks_ref[GQ + j], sink_hi)
    bias_lo = jnp.where(valid, -slope_lo * distf, NEG_INF)
    bias_hi = jnp.where(valid, -slope_hi * distf, NEG_INF)
    p_lo, inv_lo = _sink_softmax(s_lo + bias_lo[None], sink_lo[None])
    p_hi, inv_hi = _sink_softmax(s_hi + bias_hi[None], sink_hi[None])
    o_lo = jnp.einsum('bqk,bkd->bqd', p_lo.astype(_BF), vv, preferred_element_type=_F32)
    o_hi = jnp.einsum('bqk,bkd->bqd', p_hi.astype(_BF), vv, preferred_element_type=_F32)
    o = jnp.where(lo_lane[None], o_lo * inv_lo, o_hi * inv_hi)
    os2[...] = o.reshape(BT * NQ, LANES)

    lbuf = lbuf_ref[...]
    sbuf = sbuf_ref[...]
    xr_hist = [lbuf[:, i * D_LRU:(i + 1) * D_LRU] for i in range(LRU_CONV - 1)]
    z_hist = [sbuf[:, i * D_SC:(i + 1) * D_SC] for i in range(SC_CONV - 1)]
    h = h0_ref[...]
    gmix = gmix_ref[...]
    mixed_rows = []
    for t in range(TS):
        rows = slice(t * BT, (t + 1) * BT)
        o_att = jnp.concatenate(
            [os2[pl.ds(t * N_QTILES + j, BT, stride=NQ), :] for j in range(N_QTILES)], axis=-1)
        xr_hist.append(proj[rows, O_XR:O_XR + D_LRU])
        xc = sum(xr_hist[t + kk] * cw_ref[kk:kk + 1, :] for kk in range(LRU_CONV))
        xcb = xc + v256_ref[0:1, :]
        a, b = _lru_coeffs(xcb, wlru_ref[...], v256_ref[1:2, :], v256_ref[2:3, :], v256_ref[3:4, :])
        h = a * h + b
        y_lru = h * jax.nn.gelu(proj[rows, O_GR:O_GR + D_LRU])
        z_hist.append(proj[rows, O_CG:O_CG + D_SC] * proj[rows, O_U:O_U + D_SC])
        yc = sum(z_hist[t + kk] * scw_ref[kk:kk + 1, :] for kk in range(SC_CONV))
        y_sc = proj[rows, O_BG:O_BG + D_SC] * yc
        mixed_rows.append(jnp.concatenate(
            [_rms_rows(o_att, gmix[:, 0:Q_W]),
             _rms_rows(y_lru, gmix[:, Q_W:Q_W + D_LRU]),
             _rms_rows(y_sc, gmix[:, Q_W + D_LRU:])], axis=-1))
    hnew_ref[...] = h
    lrubuf_ref[...] = jnp.concatenate(xr_hist[TS:], axis=-1)
    scbuf_ref[...] = jnp.concatenate(z_hist[TS:], axis=-1)
    mixed = jnp.concatenate(mixed_rows, axis=0)
    m = _dot(mixed.astype(_BF), wout_ref[...])
    y = _ln_rows(ALPHA * x + m, ln_ref[0:1, :], ln_ref[1:2, :])
    y_ref[...] = y.reshape(TS, BT, D_MODEL)


def _mixer_sample(l, x, ck, cv, h0, lbuf, sbuf, sinks, *params):
    TS, B, _ = x.shape
    BT = BT_SAMPLE
    rows2 = lambda w: pl.BlockSpec((BT, w), lambda i: (i, 0))
    rows2_l = lambda w: pl.BlockSpec((None, BT, w), lambda i: (l, i, 0))
    cache = pl.BlockSpec((BT, WINDOW, KV_W), lambda i: (i, 0, 0))
    cache_l = pl.BlockSpec((None, BT, WINDOW, KV_W), lambda i: (l, i, 0, 0))
    xspec = pl.BlockSpec((TS, BT, D_MODEL), lambda i: (0, i, 0))
    return pl.pallas_call(
        _mixer_sample_kernel,
        grid=(B // BT,),
        in_specs=[
            pl.BlockSpec(memory_space=pltpu.SMEM),
            xspec, cache_l, cache_l, rows2_l(D_LRU), rows2_l((LRU_CONV - 1) * D_LRU),
            rows2_l((SC_CONV - 1) * D_SC),
            *[_layer_spec(p, l) for p in params],
        ],
        out_specs=[xspec, cache, cache, rows2(D_LRU), rows2((LRU_CONV - 1) * D_LRU),
                   rows2((SC_CONV - 1) * D_SC)],
        out_shape=[
            jax.ShapeDtypeStruct((TS, B, D_MODEL), _F32),
            jax.ShapeDtypeStruct((B, WINDOW, KV_W), _F32),
            jax.ShapeDtypeStruct((B, WINDOW, KV_W), _F32),
            jax.ShapeDtypeStruct((B, D_LRU), _F32),
            jax.ShapeDtypeStruct((B, (LRU_CONV - 1) * D_LRU), _F32),
            jax.ShapeDtypeStruct((B, (SC_CONV - 1) * D_SC), _F32),
        ],
        scratch_shapes=[
            pltpu.VMEM((BT * KPAD, LANES), _F32),
            pltpu.VMEM((BT * KPAD, LANES), _F32),
            pltpu.VMEM((BT * TS * N_QTILES, LANES), _F32),
            pltpu.VMEM((BT * TS * N_QTILES, LANES), _F32),
        ],
        compiler_params=pltpu.CompilerParams(
            dimension_semantics=("arbitrary",), vmem_limit_bytes=VMEM_LIMIT),
        name="mixer_sample",
    )(sinks, x, ck, cv, h0, lbuf, sbuf, *params)


def _route(logits):
    lane = lax.broadcasted_iota(jnp.int32, logits.shape, 1)
    big = jnp.int32(ROUTE_W)
    is_grp = lane < N_GROUPS
    glog = jnp.where(is_grp, logits, -jnp.inf)
    gmax = jnp.max(glog, axis=-1, keepdims=True)
    gsel = jnp.min(jnp.where(glog == gmax, lane, big), axis=-1, keepdims=True)
    gw = 1.0 / jnp.sum(jnp.exp(glog - gmax), axis=-1, keepdims=True)
    e_lo = N_GROUPS + gsel * EXPERTS_PER_GROUP
    in_grp = (lane >= e_lo) & (lane < e_lo + EXPERTS_PER_GROUP)
    elog = jnp.where(in_grp, logits, -jnp.inf)
    v1 = jnp.max(elog, axis=-1, keepdims=True)
    i1 = jnp.min(jnp.where(elog == v1, lane, big), axis=-1, keepdims=True)
    elog2 = jnp.where(lane == i1, -jnp.inf, elog)
    v2 = jnp.max(elog2, axis=-1, keepdims=True)
    i2 = jnp.min(jnp.where(elog2 == v2, lane, big), axis=-1, keepdims=True)
    e2 = jnp.exp(v2 - v1)
    w1 = gw / (1.0 + e2)
    w2 = gw * e2 / (1.0 + e2)
    return jnp.where(lane == i1, w1, jnp.where(lane == i2, w2, 0.0))


def _group_experts(xb, gates, g, wg, wu, wd, hsc):
    lane = lax.broadcasted_iota(jnp.int32, (1, ROUTE_W), 1)
    for e in range(EXPERTS_PER_GROUP):
        hg = _dot(xb, wg[e])
        hu = _dot(xb, wu[e])
        gate = jnp.sum(jnp.where(lane == N_GROUPS + g * EXPERTS_PER_GROUP + e, gates, 0.0),
                       axis=-1, keepdims=True)
        hsc[:, e * D_EXPERT:(e + 1) * D_EXPERT] = (jax.nn.silu(hg) * hu * gate).astype(_BF)
    return _dot(hsc[...], wd)


def _moe_kernel(x_ref, wr_ref, br_ref, wg_ref, wu_ref, wd_ref, ln_ref, y_ref, xb_s, gates_s, hsc, acc):
    g = pl.program_id(0)

    @pl.when(g == 0)
    def _():
        xb = x_ref[...].astype(_BF)
        xb_s[...] = xb
        gates_s[...] = _route(_dot(xb, wr_ref[...]) + br_ref[...])
        acc[...] = jnp.zeros(acc.shape, _F32)

    wd = wd_ref[...].astype(_BF).reshape(EXPERTS_PER_GROUP * D_EXPERT, D_MODEL)
    acc[...] += _group_experts(xb_s[...], gates_s[...], g, wg_ref[...].astype(_BF),
                               wu_ref[...].astype(_BF), wd, hsc)

    @pl.when(g == N_GROUPS - 1)
    def _():
        y_ref[...] = _ln_rows(ALPHA * x_ref[...] + acc[...], ln_ref[0:1, :], ln_ref[1:2, :])


def _group_weight_spec(w, index_map):
    return pl.BlockSpec((None, EXPERTS_PER_GROUP) + w.shape[2:], index_map)


def _moe(l, x, wr, br, wg, wu, wd, ln):
    N = x.shape[0]
    tok = pl.BlockSpec((N, D_MODEL), lambda g: (0, 0))
    gw = lambda w: _group_weight_spec(w, lambda g: (l, g, 0, 0))
    return pl.pallas_call(
        _moe_kernel,
        grid=(N_GROUPS,),
        in_specs=[tok, _layer_spec(wr, l), _layer_spec(br, l), gw(wg), gw(wu), gw(wd), _layer_spec(ln, l)],
        out_specs=tok,
        out_shape=jax.ShapeDtypeStruct((N, D_MODEL), _F32),
        scratch_shapes=[
            pltpu.VMEM((N, D_MODEL), _BF),
            pltpu.VMEM((N, ROUTE_W), _F32),
            pltpu.VMEM((N, EXPERTS_PER_GROUP * D_EXPERT), _BF),
            pltpu.VMEM((N, D_MODEL), _F32),
        ],
        compiler_params=pltpu.CompilerParams(
            dimension_semantics=("arbitrary",), vmem_limit_bytes=VMEM_LIMIT),
        name="moe",
    )(x, wr, br, wg, wu, wd, ln)


CODE_ROWS = SUBLANES


def _route_kernel(x_ref, wr_ref, br_ref, code_ref, cnt_ref, utri, cnt_s):
    T = T_DISPATCH
    i = pl.program_id(0)

    @pl.when(i == 0)
    def _():
        cnt_s[...] = jnp.zeros((SUBLANES, T), _F32)
        r = lax.broadcasted_iota(jnp.int32, (T, T), 0)
        c = lax.broadcasted_iota(jnp.int32, (T, T), 1)
        utri[...] = jnp.where(r < c, 1.0, 0.0).astype(_BF)

    logits = _dot(x_ref[...].astype(_BF), wr_ref[...]) + br_ref[...]
    lt = logits.T[0:SUBLANES]
    g = [lt[k:k + 1] for k in range(N_GROUPS)]
    gmax = functools.reduce(jnp.maximum, g)
    gsel = jnp.full((1, T), N_GROUPS - 1, jnp.int32)
    for k in range(N_GROUPS - 2, -1, -1):
        gsel = jnp.where(g[k] == gmax, k, gsel)
    row = lax.broadcasted_iota(jnp.int32, (SUBLANES, T), 0)
    onehot = row == gsel
    onehot_f = jnp.where(onehot, 1.0, 0.0)
    rank = _dot(onehot_f.astype(_BF), utri[...])
    cnt = cnt_s[...]
    loc = jnp.sum(jnp.where(onehot, cnt + rank, 0.0), axis=0, keepdims=True).astype(jnp.int32)
    cnt_new = cnt + jnp.sum(onehot_f, axis=1, keepdims=True)
    cnt_s[...] = cnt_new
    cnt_ref[...] = cnt_new[:, 0:LANES].astype(jnp.int32)
    code_ref[0] = jnp.where(row == 0, loc, jnp.where(row == 1, gsel, 0))


def _route_tokens(l, x, wr, br):
    N = x.shape[0]
    T = T_DISPATCH
    return pl.pallas_call(
        _route_kernel,
        grid=(N // T,),
        in_specs=[pl.BlockSpec((T, D_MODEL), lambda i: (i, 0)),
                  _layer_spec(wr, l), _layer_spec(br, l)],
        out_specs=[pl.BlockSpec((1, CODE_ROWS, T), lambda i: (i, 0, 0)),
                   pl.BlockSpec((SUBLANES, LANES), lambda i: (0, 0))],
        out_shape=[jax.ShapeDtypeStruct((N // T, CODE_ROWS, T), jnp.int32),
                   jax.ShapeDtypeStruct((SUBLANES, LANES), jnp.int32)],
        scratch_shapes=[
            pltpu.VMEM((T, T), _BF),
            pltpu.VMEM((SUBLANES, T), _F32),
        ],
        compiler_params=pltpu.CompilerParams(
            dimension_semantics=("arbitrary",), vmem_limit_bytes=VMEM_LIMIT),
        name="moe_route",
    )(x, wr, br)


def _to_token_rows(x, rows_ref):
    R = x.shape[0]
    for c in range(TOK_ROWS):
        rows_ref[pl.ds(c, R, stride=TOK_ROWS), :] = x[:, c * LANES:(c + 1) * LANES]


def _from_token_rows(rows_ref, R):
    return jnp.concatenate(
        [rows_ref[pl.ds(c, R, stride=TOK_ROWS), :] for c in range(TOK_ROWS)], axis=-1)


def _token_copy(src_ref, s, dst_ref, d, sem):
    return pltpu.make_async_copy(
        src_ref.at[pl.ds(pl.multiple_of(s * TOK_ROWS, TOK_ROWS), TOK_ROWS)],
        dst_ref.at[pl.ds(pl.multiple_of(d * TOK_ROWS, TOK_ROWS), TOK_ROWS)], sem)


def _load_dest(dest_ref, dest_sm, sem_idx):
    copies = [pltpu.make_async_copy(dest_ref.at[j, 0], dest_sm.at[pl.ds(j * T_DISPATCH, T_DISPATCH)], sem_idx)
              for j in range(T_PERM // T_DISPATCH)]
    for cp in copies:
        cp.start()
    for cp in copies:
        cp.wait()


def _scatter_kernel(off_ref, code_ref, x_ref, xs_ref, dest_ref, xrow, dest_sm, sem_idx, sem_rows):
    code = code_ref[...]
    loc = code[:, 0:1]
    grp = code[:, 1:2]
    off = jnp.zeros(loc.shape, jnp.int32)
    for g in range(N_GROUPS):
        off = jnp.where(grp == g, off_ref[g], off)
    dest_ref[...] = off + loc
    _to_token_rows(x_ref[...], xrow)
    _load_dest(dest_ref, dest_sm, sem_idx)

    def start(i, c):
        for k in range(DMA_UNROLL):
            t = i * DMA_UNROLL + k
            _token_copy(xrow, t, xs_ref, dest_sm[t], sem_rows).start(priority=k % 2)
        return c

    lax.fori_loop(0, T_PERM // DMA_UNROLL, start, 0)

    def wait(t, c):
        _token_copy(xrow, 0, xs_ref, 0, sem_rows).wait()
        return c

    lax.fori_loop(0, T_PERM, wait, 0, unroll=DMA_UNROLL)


def _scatter(off, code, x):
    N = x.shape[0]
    T = T_PERM
    nt = T // T_DISPATCH
    return pl.pallas_call(
        _scatter_kernel,
        grid=(N // T,),
        in_specs=[pl.BlockSpec(memory_space=pltpu.SMEM),
                  pl.BlockSpec((nt, CODE_ROWS, T_DISPATCH), lambda i: (i, 0, 0)),
                  pl.BlockSpec((T, D_MODEL), lambda i: (i, 0))],
        out_specs=[pl.BlockSpec(memory_space=pl.ANY),
                   pl.BlockSpec((nt, 1, T_DISPATCH), lambda i: (i, 0, 0))],
        out_shape=[jax.ShapeDtypeStruct((N * TOK_ROWS, LANES), _F32),
                   jax.ShapeDtypeStruct((N // T_DISPATCH, 1, T_DISPATCH), jnp.int32)],
        scratch_shapes=[pltpu.VMEM((T * TOK_ROWS, LANES), _F32),
                        pltpu.SMEM((T,), jnp.int32), pltpu.SemaphoreType.DMA, pltpu.SemaphoreType.DMA],
        compiler_params=pltpu.CompilerParams(
            dimension_semantics=("arbitrary",), vmem_limit_bytes=VMEM_LIMIT),
        name="moe_scatter",
    )(off, code, x)


def _route_in_group(logits, g):
    lane = lax.broadcasted_iota(jnp.int32, logits.shape, 1)
    big = jnp.int32(ROUTE_W)
    glog = jnp.where(lane < N_GROUPS, logits, -jnp.inf)
    gmax = jnp.max(glog, axis=-1, keepdims=True)
    lg = jnp.sum(jnp.where(lane == g, logits, 0.0), axis=-1, keepdims=True)
    gw = jnp.exp(lg - gmax) / jnp.sum(jnp.exp(glog - gmax), axis=-1, keepdims=True)
    e_lo = N_GROUPS + g * EXPERTS_PER_GROUP
    in_grp = (lane >= e_lo) & (lane < e_lo + EXPERTS_PER_GROUP)
    elog = jnp.where(in_grp, logits, -jnp.inf)
    v1 = jnp.max(elog, axis=-1, keepdims=True)
    i1 = jnp.min(jnp.where(elog == v1, lane, big), axis=-1, keepdims=True)
    elog2 = jnp.where(lane == i1, -jnp.inf, elog)
    v2 = jnp.max(elog2, axis=-1, keepdims=True)
    i2 = jnp.min(jnp.where(elog2 == v2, lane, big), axis=-1, keepdims=True)
    e2 = jnp.exp(v2 - v1)
    w1 = gw / (1.0 + e2)
    w2 = gw * e2 / (1.0 + e2)
    return jnp.where(lane == i1, w1, jnp.where(lane == i2, w2, 0.0))


def _moe_grouped_kernel(bidx_ref, bgrp_ref, blo_ref, bhi_ref, bfirst_ref, gfirst_ref, nstep_ref,
                        x_ref, wr_ref, br_ref, wg_ref, wu_ref, wd_ref, ln_ref, y_ref,
                        wg_b, wu_b, wd_b, hsc):
    i = pl.program_id(0)

    @pl.when(gfirst_ref[i] == 1)
    def _():
        wg_b[...] = wg_ref[...].astype(_BF)
        wu_b[...] = wu_ref[...].astype(_BF)
        wd_b[...] = wd_ref[...].astype(_BF).reshape(wd_b.shape)

    @pl.when(i < nstep_ref[0])
    def _():
        g = bgrp_ref[i]
        x = _from_token_rows(x_ref, BM)
        xb = x.astype(_BF)
        gates = _route_in_group(_dot(xb, wr_ref[...]) + br_ref[...], g)
        y = _group_experts(xb, gates, g, wg_b, wu_b, wd_b[...], hsc)
        y_new = _ln_rows(ALPHA * x + y, ln_ref[0:1, :], ln_ref[1:2, :])

        @pl.when(bfirst_ref[i] == 1)
        def _():
            _to_token_rows(y_new, y_ref)

        @pl.when(bfirst_ref[i] == 0)
        def _():
            row = lax.broadcasted_iota(jnp.int32, (BM, 1), 0)
            mine = (row >= blo_ref[i]) & (row < bhi_ref[i])
            _to_token_rows(jnp.where(mine, y_new, _from_token_rows(y_ref, BM)), y_ref)


def _moe_grouped(l, tables, xs, wr, br, wg, wu, wd, ln):
    nsteps = tables[0].shape[0]
    tok = pl.BlockSpec((BM * TOK_ROWS, LANES), lambda i, bidx, *_: (bidx[i], 0))
    gw = lambda w: _group_weight_spec(w, lambda i, bidx, bgrp, *_: (l, bgrp[i], 0, 0))
    return pl.pallas_call(
        _moe_grouped_kernel,
        grid_spec=pltpu.PrefetchScalarGridSpec(
            num_scalar_prefetch=len(tables),
            grid=(nsteps,),
            in_specs=[tok, _layer_spec(wr, l), _layer_spec(br, l), gw(wg), gw(wu), gw(wd),
                      _layer_spec(ln, l)],
            out_specs=tok,
            scratch_shapes=[
                pltpu.VMEM((EXPERTS_PER_GROUP, D_MODEL, D_EXPERT), _BF),
                pltpu.VMEM((EXPERTS_PER_GROUP, D_MODEL, D_EXPERT), _BF),
                pltpu.VMEM((EXPERTS_PER_GROUP * D_EXPERT, D_MODEL), _BF),
                pltpu.VMEM((BM, EXPERTS_PER_GROUP * D_EXPERT), _BF),
            ],
        ),
        out_shape=jax.ShapeDtypeStruct(xs.shape, _F32),
        compiler_params=pltpu.CompilerParams(
            dimension_semantics=("arbitrary",), vmem_limit_bytes=VMEM_LIMIT),
        name="moe_grouped",
    )(*tables, xs, wr, br, wg, wu, wd, ln)


def _combine_kernel(dest_ref, ys_ref, out_ref, yrow, dest_sm, sem_idx, sem_rows):
    _load_dest(dest_ref, dest_sm, sem_idx)

    def start(i, c):
        for k in range(DMA_UNROLL):
            t = i * DMA_UNROLL + k
            _token_copy(ys_ref, dest_sm[t], yrow, t, sem_rows).start(priority=k % 2)
        return c

    lax.fori_loop(0, T_PERM // DMA_UNROLL, start, 0)

    def wait(t, c):
        _token_copy(ys_ref, 0, yrow, 0, sem_rows).wait()
        return c

    lax.fori_loop(0, T_PERM, wait, 0, unroll=DMA_UNROLL)
    out_ref[...] = _from_token_rows(yrow, T_PERM)


def _combine(dest, ys):
    N = dest.shape[0] * T_DISPATCH
    T = T_PERM
    nt = T // T_DISPATCH
    return pl.pallas_call(
        _combine_kernel,
        grid=(N // T,),
        in_specs=[pl.BlockSpec((nt, 1, T_DISPATCH), lambda i: (i, 0, 0)),
                  pl.BlockSpec(memory_space=pl.ANY)],
        out_specs=pl.BlockSpec((T, D_MODEL), lambda i: (i, 0)),
        out_shape=jax.ShapeDtypeStruct((N, D_MODEL), _F32),
        scratch_shapes=[pltpu.VMEM((T * TOK_ROWS, LANES), _F32),
                        pltpu.SMEM((T,), jnp.int32), pltpu.SemaphoreType.DMA, pltpu.SemaphoreType.DMA],
        compiler_params=pltpu.CompilerParams(
            dimension_semantics=("arbitrary",), vmem_limit_bytes=VMEM_LIMIT),
        name="moe_combine",
    )(dest, ys)


def _step_tables(cnt, nsteps):
    n = cnt[:N_GROUPS, 0]
    off = jnp.cumsum(n) - n
    b_lo = off // BM
    nb = jnp.where(n > 0, (off + n - 1) // BM - b_lo + 1, 0)
    ends = jnp.cumsum(nb)
    total = ends[-1]
    step = jnp.minimum(jnp.arange(nsteps, dtype=jnp.int32), total - 1)
    grp = jnp.sum(step[:, None] >= ends[None, :], axis=1).astype(jnp.int32)
    bidx = b_lo[grp] + step - (ends - nb)[grp]
    lo = jnp.clip(off[grp] - bidx * BM, 0, BM)
    hi = jnp.clip(off[grp] + n[grp] - bidx * BM, 0, BM)
    one = jnp.ones((1,), jnp.int32)
    first = jnp.concatenate([one, (bidx[1:] != bidx[:-1]).astype(jnp.int32)])
    gfirst = jnp.concatenate([one, (grp[1:] != grp[:-1]).astype(jnp.int32)])
    tables = tuple(a.astype(jnp.int32) for a in (bidx, grp, lo, hi, first, gfirst, total[None]))
    return off.astype(jnp.int32), tables


def _moe_sorted(l, x, wr, br, wg, wu, wd, ln):
    N = x.shape[0]
    code, cnt = _route_tokens(l, x, wr, br)
    off, tables = _step_tables(cnt, N // BM + N_GROUPS - 1)
    xs, dest = _scatter(off, code, x)
    ys = _moe_grouped(l, tables, xs, wr, br, wg, wu, wd, ln)
    return _combine(dest, ys)


def _q_perm(w, axis):
    shape = w.shape
    w = w.reshape(shape[:axis] + (N_KV_HEADS, GQ, HEAD_DIM) + shape[axis + 1:])
    return jnp.swapaxes(w, axis, axis + 1).reshape(shape)


def _block_diag(w):
    eye = jnp.eye(LRU_BLOCKS, dtype=w.dtype)
    return jnp.einsum('nij,nm->nimj', w, eye).reshape(D_LRU, D_LRU)


def kernel(x_prompt, x_sample, cache_k, cache_v, state_lru_h, state_lru_conv, state_sc_conv, w_in, attn_sinks, lru_conv_w, lru_conv_b, lru_w_a, lru_b_a, lru_w_x, lru_b_x, lru_lambda, sc_conv_w, g_mix, w_out, ln1_g, ln1_b, w_grp, b_grp, w_route, b_route, w_gate, w_up, w_down, ln2_g, ln2_b):
    B, S, _ = x_prompt.shape
    DB, TS, _ = x_sample.shape
    assert S % T_PROMPT == 0 and (B * S) % T_PERM == 0 and (B * S) % BM == 0 and DB % BT_SAMPLE == 0
    assert T_PERM % T_DISPATCH == 0
    assert B * S < (1 << 24)
    assert TS <= SUBLANES and cache_k.shape[2] == WINDOW

    win = jnp.concatenate([_q_perm(w_in[:, :, :Q_W], 2), w_in[:, :, Q_W:]], axis=2).astype(_BF)
    wout = jnp.concatenate([_q_perm(w_out[:, :Q_W], 1), w_out[:, Q_W:]], axis=1).astype(_BF)
    gmix = jnp.concatenate([_q_perm(g_mix[:, :Q_W], 1), g_mix[:, Q_W:]], axis=1)[:, None, :]
    wlru = jnp.concatenate([jax.vmap(_block_diag)(lru_w_a), jax.vmap(_block_diag)(lru_w_x)], axis=2).astype(_BF)
    v256 = jnp.stack([lru_conv_b, lru_b_a, lru_b_x, lru_lambda], axis=1)
    ln1 = jnp.stack([ln1_g, ln1_b], axis=1)
    ln2 = jnp.stack([ln2_g, ln2_b], axis=1)
    pad = ROUTE_W - N_GROUPS - N_EXPERTS
    wr = jnp.pad(jnp.concatenate([w_grp, w_route], axis=2), ((0, 0), (0, 0), (0, pad))).astype(_BF)
    br = jnp.pad(jnp.concatenate([b_grp, b_route], axis=1), ((0, 0), (0, pad)))[:, None, :]
    mix_w = (win, wlru, wout, lru_conv_w, v256, sc_conv_w, gmix, ln1)
    moe_w = (wr, br, w_gate, w_up, w_down, ln2)
    ck = cache_k.reshape(DEPTH, DB, WINDOW, KV_W)
    cv = cache_v.reshape(DEPTH, DB, WINDOW, KV_W)
    lbuf = state_lru_conv.reshape(DEPTH, DB, -1)
    sbuf = state_sc_conv.reshape(DEPTH, DB, -1)

    yp = x_prompt
    ys = jnp.swapaxes(x_sample, 0, 1)
    outs_p = [[] for _ in range(5)]
    outs_s = [[] for _ in range(5)]
    for l in range(DEPTH):
        yp, kp, vp, hp, lbp, sbp = _mixer_prompt(l, yp, attn_sinks[l], *mix_w)
        yp = _moe_sorted(l, yp.reshape(B * S, D_MODEL), *moe_w).reshape(B, S, D_MODEL)
        ys, ks, vs, hs, lbs, sbs = _mixer_sample(l, ys, ck, cv, state_lru_h, lbuf, sbuf, attn_sinks[l], *mix_w)
        ys = _moe(l, ys.reshape(TS * DB, D_MODEL), *moe_w).reshape(TS, DB, D_MODEL)
        for lst, arr in zip(outs_p, (kp.reshape(B, WINDOW, N_KV_HEADS, HEAD_DIM),
                                     vp.reshape(B, WINDOW, N_KV_HEADS, HEAD_DIM),
                                     hp.reshape(B, D_LRU), lbp, sbp)):
            lst.append(arr)
        for lst, arr in zip(outs_s, (ks.reshape(DB, WINDOW, N_KV_HEADS, HEAD_DIM),
                                     vs.reshape(DB, WINDOW, N_KV_HEADS, HEAD_DIM),
                                     hs, lbs.reshape(DB, LRU_CONV - 1, D_LRU),
                                     sbs.reshape(DB, SC_CONV - 1, D_SC))):
            lst.append(arr)
    return (yp, jnp.swapaxes(ys, 0, 1),
            *[jnp.stack(o) for o in outs_p], *[jnp.stack(o) for o in outs_s])
```
